```python
import jax, jax.numpy as jnp
from jax import lax
import numpy as np

D_MODEL = 1024
BATCH = 8
SEQ = 4096
DEPTH = 2
DEC_BATCH = 32
DEC_SEQ = 4
PAST_LEN = 16384
PAGE_SIZE = 128

D_RNN = D_MODEL // 2
RG_BLOCKS = 8
RG_BLOCK_W = D_RNN // RG_BLOCKS
RG_CONV = 4
RG_C = 8.0
N_HEADS = 8
HEAD_DIM = (D_MODEL // 2) // N_HEADS
N_KV = 2
GROUP = N_HEADS // N_KV
KV_DIM = N_KV * HEAD_DIM
CMP_STRIDE = 16
CMP_BLOCK = 2 * CMP_STRIDE
CMP_HIDDEN = 2 * HEAD_DIM
SLC_BLOCK = 64
N_SELECT = 16
N_LOCAL = 2
WINDOW = 512
Q_BLOCK = 128
D_FF = 3 * D_MODEL
FFN_CONV = 3
ROPE_THETA = 10000.0
EPS = 1e-6
N_IN = 2 * D_RNN + N_HEADS * HEAD_DIM + 6 * KV_DIM + 3 * N_HEADS

kernel_name = 'hymba_rglru_nsa_convffn_step'


def _rms(x, g):
    x32 = x.astype(jnp.float32)
    y = x32 * lax.rsqrt(jnp.mean(x32 * x32, axis=-1, keepdims=True) + EPS)
    return (y * g.astype(jnp.float32)).astype(x.dtype)


def _rope(x, pos):
    half = HEAD_DIM // 2
    freq = ROPE_THETA ** (-jnp.arange(half, dtype=jnp.float32) / half)
    ang = pos.astype(jnp.float32)[:, None] * freq[None, :]
    shape = (1, pos.shape[0]) + (1,) * (x.ndim - 3) + (half,)
    cos, sin = jnp.cos(ang).reshape(shape), jnp.sin(ang).reshape(shape)
    x32 = x.astype(jnp.float32)
    x1, x2 = x32[..., :half], x32[..., half:]
    return jnp.concatenate([x1 * cos - x2 * sin, x2 * cos + x1 * sin], axis=-1).astype(x.dtype)


def _causal_dwconv(x, prefix, w, b):
    width, t = w.shape[0], x.shape[1]
    xp = jnp.concatenate([prefix.astype(x.dtype), x], axis=1)
    y = b
    for k in range(width):
        y = y + xp[:, k:k + t] * w[k]
    return y, xp[:, t:]


def _masked_softmax(s, mask):
    s = jnp.where(mask, s.astype(jnp.float32), -jnp.inf)
    m = jnp.max(s, axis=-1, keepdims=True)
    m = jnp.where(jnp.isfinite(m), m, 0.0)
    e = jnp.where(mask, jnp.exp(s - m), 0.0)
    return e / jnp.maximum(jnp.sum(e, axis=-1, keepdims=True), 1e-30)


def _compress(rows, pe, w1, b1, w2):
    bsz, length = rows.shape[:2]
    n = length // CMP_STRIDE
    ch = rows[:, :n * CMP_STRIDE].reshape(bsz, n, CMP_STRIDE, N_KV, HEAD_DIM)
    ua = jnp.einsum('bnshd,sdf->bnhf', ch + pe[:CMP_STRIDE, None, :], w1[:CMP_STRIDE])
    ub = jnp.einsum('bnshd,sdf->bnhf', ch + pe[CMP_STRIDE:, None, :], w1[CMP_STRIDE:])
    hid = jax.nn.gelu(ua[:, :-1] + ub[:, 1:] + b1)
    return jnp.einsum('bnhf,fd->bnhd', hid, w2)


def _cmp_to_slc(p, n_slc):
    r = SLC_BLOCK // CMP_STRIDE
    c = p.shape[-1]
    pp = jnp.pad(p, [(0, 0)] * (p.ndim - 1) + [(1, r * n_slc - c)])
    return pp[..., :r * n_slc].reshape(p.shape[:-1] + (n_slc, r)).sum(-1) + pp[..., r:r * n_slc + 1:r]


def _slc_blocks(rows):
    bsz, length = rows.shape[:2]
    n = -(-length // SLC_BLOCK)
    rows = jnp.pad(rows, ((0, 0), (0, n * SLC_BLOCK - length), (0, 0), (0, 0)))
    return rows.reshape(bsz, n, SLC_BLOCK, N_KV, HEAD_DIM).transpose(0, 3, 1, 2, 4)


def _nsa_core(q_c, q_r, q_pos, kc, vc, c_end, ks_blk, vs_blk, kw, vw, w_pos, gates):
    bsz, t = q_c.shape[:2]
    n_slc = ks_blk.shape[2]
    scale = HEAD_DIM ** -0.5
    s = jnp.einsum('bthgd,bchd->bthgc', q_c, kc) * scale
    mc = (c_end[None, :] <= q_pos[:, None])[None, :, None, None, :]
    p_cmp = _masked_softmax(s, mc)
    o_cmp = jnp.einsum('bthgc,bchd->bthgd', p_cmp.astype(vc.dtype), vc)
    score = _cmp_to_slc(p_cmp.sum(axis=3), n_slc)
    blk = jnp.arange(n_slc)
    lag = (q_pos // SLC_BLOCK)[:, None] - blk[None, :]
    forced = (blk[None, :] == 0) | ((lag >= 0) & (lag < N_LOCAL))
    score = jnp.where(forced[None, :, None, :], jnp.inf, score)
    score = jnp.where((lag >= 0)[None, :, None, :], score, -jnp.inf)
    k_sel = min(N_SELECT, n_slc)
    _, idx = lax.top_k(score, k_sel)
    idx_h = idx.transpose(0, 2, 1, 3)
    flat = idx_h.reshape(bsz, N_KV, t * k_sel)
    take = jax.vmap(jax.vmap(lambda blocks, ix: blocks[ix]))
    gk = take(ks_blk, flat).reshape(bsz, N_KV, t, k_sel, SLC_BLOCK, HEAD_DIM)
    gv = take(vs_blk, flat).reshape(bsz, N_KV, t, k_sel, SLC_BLOCK, HEAD_DIM)
    tok = idx_h[..., None] * SLC_BLOCK + jnp.arange(SLC_BLOCK)
    ms = (tok <= q_pos[None, None, :, None, None]).transpose(0, 2, 1, 3, 4)
    ms = ms.reshape(bsz, t, N_KV, 1, k_sel * SLC_BLOCK)
    s = jnp.einsum('bthgd,bhtkld->bthgkl', q_r, gk) * scale
    p = _masked_softmax(s.reshape(bsz, t, N_KV, GROUP, k_sel * SLC_BLOCK), ms).reshape(s.shape)
    o_slc = jnp.einsum('bthgkl,bhtkld->bthgd', p.astype(gv.dtype), gv)
    s = jnp.einsum('bthgd,bshd->bthgs', q_r, kw) * scale
    dist = q_pos[:, None] - w_pos[None, :]
    mw = ((dist >= 0) & (dist < WINDOW) & (w_pos[None, :] >= 0))[None, :, None, None, :]
    p = _masked_softmax(s, mw)
    o_win = jnp.einsum('bthgs,bshd->bthgd', p.astype(vw.dtype), vw)
    return gates[..., 0:1] * o_cmp + gates[..., 1:2] * o_slc + gates[..., 2:3] * o_win


def _project(h, w_in):
    sizes = [D_RNN, D_RNN, N_HEADS * HEAD_DIM, 2 * KV_DIM, 2 * KV_DIM, 2 * KV_DIM, 3 * N_HEADS]
    cuts = np.cumsum(sizes)[:-1].tolist()
    xr, gate, q, kvc, kvs, kvw, gl = jnp.split(h @ w_in, cuts, axis=-1)
    bsz, t = h.shape[:2]
    kv = lambda z: z.reshape(bsz, t, 2, N_KV, HEAD_DIM)
    gates = jax.nn.sigmoid(gl.astype(jnp.float32)).astype(h.dtype).reshape(bsz, t, N_KV, GROUP, 3)
    return xr, gate, q.reshape(bsz, t, N_KV, GROUP, HEAD_DIM), kv(kvc), kv(kvs), kv(kvw), gates


def _rglru(xr, gate, conv_prefix, h0, lp):
    xc, conv_state = _causal_dwconv(xr, conv_prefix, lp['rg_conv_w'], lp['rg_conv_b'])
    bsz, t = xc.shape[:2]
    xb = xc.reshape(bsz, t, RG_BLOCKS, RG_BLOCK_W)
    r = jax.nn.sigmoid(jnp.einsum('btnc,ncd->btnd', xb, lp['rg_wa']).reshape(bsz, t, D_RNN) + lp['rg_ba'])
    i = jax.nn.sigmoid(jnp.einsum('btnc,ncd->btnd', xb, lp['rg_wx']).reshape(bsz, t, D_RNN) + lp['rg_bx'])
    log_a = RG_C * r.astype(jnp.float32) * jax.nn.log_sigmoid(lp['rg_lambda'].astype(jnp.float32))
    a = jnp.exp(log_a)
    b = jnp.sqrt(-jnp.expm1(2.0 * log_a)) * (i * xc).astype(jnp.float32)
    b = b.at[:, 0].add(a[:, 0] * h0.astype(jnp.float32))
    comb = lambda e, l: (e[0] * l[0], l[0] * e[1] + l[1])
    _, hseq = lax.associative_scan(comb, (a, b), axis=1)
    y = (hseq * jax.nn.gelu(gate.astype(jnp.float32))).astype(xr.dtype)
    return y, conv_state, hseq[:, -1].astype(h0.dtype)


def _ffn(h, prefix, lp):
    u, g = jnp.split(h @ lp['ffn_w_in'], 2, axis=-1)
    gc, state = _causal_dwconv(g, prefix, lp['ffn_conv_w'], lp['ffn_conv_b'])
    return (jax.nn.gelu(gc) * u) @ lp['ffn_w_down'], state


def _merge(y_rnn, o, lp):
    bsz, t = y_rnn.shape[:2]
    o = o.reshape(bsz, t, N_HEADS * HEAD_DIM).astype(y_rnn.dtype)
    return jnp.concatenate([y_rnn, o], axis=-1) @ lp['w_out']


def _mixer_prompt(h, lp):
    bsz, t = h.shape[:2]
    xr, gate, q, kvc, kvs, kvw, gates = _project(h, lp['w_in'])
    y_rnn, rg_conv, rg_h = _rglru(xr, gate, jnp.zeros((bsz, RG_CONV - 1, D_RNN), h.dtype),
                                  jnp.zeros((bsz, D_RNN), h.dtype), lp)
    pos = jnp.arange(t)
    q_r = _rope(q, pos)
    kc = _compress(kvc[:, :, 0], lp['cmpk_pe'], lp['cmpk_w1'], lp['cmpk_b1'], lp['cmpk_w2'])
    vc = _compress(kvc[:, :, 1], lp['cmpv_pe'], lp['cmpv_w1'], lp['cmpv_b1'], lp['cmpv_w2'])
    c_end = jnp.arange(kc.shape[1]) * CMP_STRIDE + CMP_BLOCK - 1
    kvs_rows = jnp.stack([_rope(kvs[:, :, 0], pos), kvs[:, :, 1]], axis=2)
    ks_blk, vs_blk = _slc_blocks(kvs_rows[:, :, 0]), _slc_blocks(kvs_rows[:, :, 1])
    kvw_rows = jnp.stack([_rope(kvw[:, :, 0], pos), kvw[:, :, 1]], axis=2)
    kvw_pad = jnp.pad(kvw_rows, ((0, 0), (WINDOW, 0), (0, 0), (0, 0), (0, 0)))

    def one_block(start):
        qs = lambda z: lax.dynamic_slice_in_dim(z, start, Q_BLOCK, axis=1)
        win = lax.dynamic_slice_in_dim(kvw_pad, start, WINDOW + Q_BLOCK, axis=1)
        w_pos = start - WINDOW + jnp.arange(WINDOW + Q_BLOCK)
        q_pos = start + jnp.arange(Q_BLOCK)
        return _nsa_core(qs(q), qs(q_r), q_pos, kc, vc, c_end, ks_blk, vs_blk,
                         win[:, :, 0], win[:, :, 1], w_pos, qs(gates))

    o = lax.map(one_block, jnp.arange(t // Q_BLOCK) * Q_BLOCK)
    o = jnp.moveaxis(o, 0, 1)
    win_buf = min(WINDOW, PAST_LEN)
    win_state = jnp.pad(kvw_rows, ((0, 0), (max(win_buf - t, 0), 0), (0, 0), (0, 0), (0, 0)))[:, -win_buf:]
    return _merge(y_rnn, o, lp), (kvc, kvs_rows, win_state, rg_h, rg_conv)


def _mixer_sample(h, lp, cmp_pool, slc_pool, page_table, win_buf, rg_h0, rg_conv0):
    bsz, t = h.shape[:2]
    xr, gate, q, kvc, kvs, kvw, gates = _project(h, lp['w_in'])
    y_rnn, rg_conv, rg_h = _rglru(xr, gate, rg_conv0, rg_h0, lp)
    pos = PAST_LEN + jnp.arange(t)
    q_r = _rope(q, pos)
    past = lambda pool: pool[page_table].reshape(bsz, -1, 2, N_KV, HEAD_DIM)
    cmp_rows = jnp.concatenate([past(cmp_pool), kvc], axis=1)
    kc = _compress(cmp_rows[:, :, 0], lp['cmpk_pe'], lp['cmpk_w1'], lp['cmpk_b1'], lp['cmpk_w2'])
    vc = _compress(cmp_rows[:, :, 1], lp['cmpv_pe'], lp['cmpv_w1'], lp['cmpv_b1'], lp['cmpv_w2'])
    c_end = jnp.arange(kc.shape[1]) * CMP_STRIDE + CMP_BLOCK - 1
    kvs_rows = jnp.stack([_rope(kvs[:, :, 0], pos), kvs[:, :, 1]], axis=2)
    slc_rows = jnp.concatenate([past(slc_pool), kvs_rows], axis=1)
    ks_blk, vs_blk = _slc_blocks(slc_rows[:, :, 0]), _slc_blocks(slc_rows[:, :, 1])
    kvw_rows = jnp.stack([_rope(kvw[:, :, 0], pos), kvw[:, :, 1]], axis=2)
    win_all = jnp.concatenate([win_buf, kvw_rows], axis=1)
    nb = win_buf.shape[1]
    w_pos = PAST_LEN - nb + jnp.arange(nb + t)
    o = _nsa_core(q, q_r, pos, kc, vc, c_end, ks_blk, vs_blk, win_all[:, :, 0], win_all[:, :, 1], w_pos, gates)
    return _merge(y_rnn, o, lp), (kvc, kvs_rows, win_all[:, -nb:], rg_h, rg_conv)


def setup_inputs(seed: int = 0) -> dict:
    key = jax.random.key(seed)
    ks = iter(jax.random.split(key, 48))
    nrm = lambda shape, scale: scale * jax.random.normal(next(ks), shape, jnp.float32)
    n_pages = PAST_LEN // PAGE_SIZE
    n_used = DEC_BATCH * n_pages
    n_pool = n_used + n_used // 4
    win_buf = min(WINDOW, PAST_LEN)
    row = (2, N_KV, HEAD_DIM)
    page_table = jax.random.permutation(next(ks), n_pool)[:n_used].reshape(DEC_BATCH, n_pages).astype(jnp.int32)
    u = jax.random.uniform(next(ks), (DEPTH, D_RNN), jnp.float32, 0.9, 0.999)
    sa = u ** (1.0 / RG_C)
    rg_lambda = jnp.log(sa) - jnp.log1p(-sa)
    gain = lambda: 1.0 + nrm((DEPTH, D_MODEL), 0.02)
    return {
        'x_prompt': nrm((BATCH, SEQ, D_MODEL), 1.0),
        'x_sample': nrm((DEC_BATCH, DEC_SEQ, D_MODEL), 1.0),
        'cache_cmp_kv': nrm((DEPTH, n_pool, PAGE_SIZE) + row, 1.0),
        'cache_slc_kv': nrm((DEPTH, n_pool, PAGE_SIZE) + row, 1.0),
        'cache_win_kv': nrm((DEPTH, DEC_BATCH, win_buf) + row, 1.0),
        'state_rg_h': nrm((DEPTH, DEC_BATCH, D_RNN), 0.5),
        'state_rg_conv': nrm((DEPTH, DEC_BATCH, RG_CONV - 1, D_RNN), 1.0),
        'state_ffn_conv': nrm((DEPTH, DEC_BATCH, FFN_CONV - 1, D_FF), 1.0),
        'page_table': page_table,
        'norm_mix_pre': gain(),
        'norm_mix_post': gain(),
        'norm_ffn_pre': gain(),
        'norm_ffn_post': gain(),
        'w_in': nrm((DEPTH, D_MODEL, N_IN), D_MODEL ** -0.5),
        'rg_conv_w': nrm((DEPTH, RG_CONV, D_RNN), RG_CONV ** -0.5),
        'rg_conv_b': nrm((DEPTH, D_RNN), 0.01),
        'rg_wa': nrm((DEPTH, RG_BLOCKS, RG_BLOCK_W, RG_BLOCK_W), RG_BLOCK_W ** -0.5),
        'rg_ba': nrm((DEPTH, D_RNN), 0.01),
        'rg_wx': nrm((DEPTH, RG_BLOCKS, RG_BLOCK_W, RG_BLOCK_W), RG_BLOCK_W ** -0.5),
        'rg_bx': nrm((DEPTH, D_RNN), 0.01),
        'rg_lambda': rg_lambda,
        'cmpk_pe': nrm((DEPTH, CMP_BLOCK, HEAD_DIM), 0.1),
        'cmpk_w1': nrm((DEPTH, CMP_BLOCK, HEAD_DIM, CMP_HIDDEN), (CMP_BLOCK * HEAD_DIM) ** -0.5),
        'cmpk_b1': nrm((DEPTH, CMP_HIDDEN), 0.01),
        'cmpk_w2': nrm((DEPTH, CMP_HIDDEN, HEAD_DIM), CMP_HIDDEN ** -0.5),
        'cmpv_pe': nrm((DEPTH, CMP_BLOCK, HEAD_DIM), 0.1),
        'cmpv_w1': nrm((DEPTH, CMP_BLOCK, HEAD_DIM, CMP_HIDDEN), (CMP_BLOCK * HEAD_DIM) ** -0.5),
        'cmpv_b1': nrm((DEPTH, CMP_HIDDEN), 0.01),
        'cmpv_w2': nrm((DEPTH, CMP_HIDDEN, HEAD_DIM), CMP_HIDDEN ** -0.5),
        'w_out': nrm((DEPTH, D_MODEL, D_MODEL), D_MODEL ** -0.5),
        'ffn_w_in': nrm((DEPTH, D_MODEL, 2 * D_FF), D_MODEL ** -0.5),
        'ffn_conv_w': nrm((DEPTH, FFN_CONV, D_FF), FFN_CONV ** -0.5),
        'ffn_conv_b': nrm((DEPTH, D_FF), 0.01),
        'ffn_w_down': nrm((DEPTH, D_FF, D_MODEL), D_FF ** -0.5),
    }


def reference(x_prompt, x_sample, cache_cmp_kv, cache_slc_kv, cache_win_kv, state_rg_h, state_rg_conv,
              state_ffn_conv, page_table, norm_mix_pre, norm_mix_post, norm_ffn_pre, norm_ffn_post, w_in,
              rg_conv_w, rg_conv_b, rg_wa, rg_ba, rg_wx, rg_bx, rg_lambda, cmpk_pe, cmpk_w1, cmpk_b1, cmpk_w2,
              cmpv_pe, cmpv_w1, cmpv_b1, cmpv_w2, w_out, ffn_w_in, ffn_conv_w, ffn_conv_b, ffn_w_down):
    hp, hs = x_prompt, x_sample
    states_p, states_s = [], []
    for l in range(DEPTH):
        lp = dict(w_in=w_in[l], rg_conv_w=rg_conv_w[l], rg_conv_b=rg_conv_b[l], rg_wa=rg_wa[l], rg_ba=rg_ba[l],
                  rg_wx=rg_wx[l], rg_bx=rg_bx[l], rg_lambda=rg_lambda[l], cmpk_pe=cmpk_pe[l], cmpk_w1=cmpk_w1[l],
                  cmpk_b1=cmpk_b1[l], cmpk_w2=cmpk_w2[l], cmpv_pe=cmpv_pe[l], cmpv_w1=cmpv_w1[l],
                  cmpv_b1=cmpv_b1[l], cmpv_w2=cmpv_w2[l], w_out=w_out[l], ffn_w_in=ffn_w_in[l],
                  ffn_conv_w=ffn_conv_w[l], ffn_conv_b=ffn_conv_b[l], ffn_w_down=ffn_w_down[l])
        m, st = _mixer_prompt(_rms(hp, norm_mix_pre[l]), lp)
        hp = hp + _rms(m, norm_mix_post[l])
        f, fc = _ffn(_rms(hp, norm_ffn_pre[l]), jnp.zeros((hp.shape[0], FFN_CONV - 1, D_FF), hp.dtype), lp)
        hp = hp + _rms(f, norm_ffn_post[l])
        states_p.append(st + (fc,))
        m, st = _mixer_sample(_rms(hs, norm_mix_pre[l]), lp, cache_cmp_kv[l], cache_slc_kv[l], page_table,
                              cache_win_kv[l], state_rg_h[l], state_rg_conv[l])
        hs = hs + _rms(m, norm_mix_post[l])
        f, fc = _ffn(_rms(hs, norm_ffn_pre[l]), state_ffn_conv[l], lp)
        hs = hs + _rms(f, norm_ffn_post[l])
        states_s.append(st + (fc,))
    p_cmp, p_slc, p_win, p_rgh, p_rgc, p_ffc = [jnp.stack(z) for z in zip(*states_p)]
    s_cmp, s_slc, s_win, s_rgh, s_rgc, s_ffc = [jnp.stack(z) for z in zip(*states_s)]
    return (hp, hs, p_cmp, s_cmp, p_slc, s_slc, p_win, s_win, p_rgh, s_rgh, p_rgc, s_rgc, p_ffc, s_ffc)
```

```python
import functools

import jax
import jax.numpy as jnp
import numpy as np
from jax import lax
from jax.experimental import pallas as pl
from jax.experimental.pallas import tpu as pltpu

F32 = jnp.float32
BF16 = jnp.bfloat16

HEAD_DIM = 64
N_KV = 2
GROUP = 4
N_HEADS = N_KV * GROUP
KV_DIM = N_KV * HEAD_DIM
CMP_STRIDE = 16
SLC_BLOCK = 64
N_SELECT = 16
N_LOCAL = 2
WINDOW = 512
Q_BLOCK = 128
RG_C = 8.0
ROPE_THETA = 10000.0
EPS = 1e-6

V7X_VMEM_BYTES = 64 * 1024 * 1024
VMEM_LIMIT = 48 * 1024 * 1024
SUBLANES = 8
LANES = 128

NEG = -2.0 ** 100
BIG = 1e30


def _cparams(sem):
    return pltpu.CompilerParams(dimension_semantics=sem, vmem_limit_bytes=VMEM_LIMIT)


def _div_tile(n, pref):
    t = min(n, pref)
    while n % t:
        t -= 1
    return t


def _rms(x, g):
    y = x * lax.rsqrt(jnp.mean(x * x, axis=-1, keepdims=True) + EPS)
    return y * g


def _expm1(x):
    u = jnp.exp(x)
    near = (jnp.abs(x) < 0.5) & (u != 1.0)
    kahan = (u - 1.0) * x / jnp.where(near, jnp.log(u), 1.0)
    return jnp.where(u == 1.0, x, jnp.where(near, kahan, u - 1.0))


def _nt(a, b, precision=None):
    return lax.dot_general(a, b, (((1,), (1,)), ((), ())), preferred_element_type=F32, precision=precision)


def _mm(a, b):
    return jnp.dot(a, b, preferred_element_type=F32)


C_XR, C_GATE, C_Q, C_QROT, C_KVC, C_KVS, C_KSROT, C_KVW, C_KWROT, C_GL, C_END = (
    0, 512, 1024, 1536, 2048, 2304, 2560, 2688, 2944, 3072, 3200)


def _proj_kernel(x_ref, g_ref, w_ref, cos_ref, sin_ref,
                 xr_ref, gate_ref, qc_ref, qr_ref, kvc_ref, kvs_ref, kvw_ref, gl_ref):
    xn = _rms(x_ref[...], g_ref[...]).astype(BF16)

    def mm(c0, c1):
        return _mm(xn, w_ref[:, c0:c1])

    cos = cos_ref[...]
    sin = sin_ref[...]
    xr_ref[...] = mm(C_XR, C_GATE)
    gate_ref[...] = mm(C_GATE, C_Q)
    q = mm(C_Q, C_QROT)
    qrot = mm(C_QROT, C_KVC)
    cos4 = jnp.concatenate([cos] * 4, axis=1)
    sin4 = jnp.concatenate([sin] * 4, axis=1)
    scale = HEAD_DIM ** -0.5
    qc_ref[...] = (q * scale).astype(BF16)
    qr_ref[...] = ((q * cos4 + qrot * sin4) * scale).astype(BF16)
    kvc_ref[...] = mm(C_KVC, C_KVS)
    kvs = mm(C_KVS, C_KSROT)
    ksrot = mm(C_KSROT, C_KVW)
    kvs_ref[:, 0:KV_DIM] = kvs[:, 0:KV_DIM] * cos + ksrot * sin
    kvs_ref[:, KV_DIM:] = kvs[:, KV_DIM:]
    kvw = mm(C_KVW, C_KWROT)
    kwrot = mm(C_KWROT, C_GL)
    kvw_ref[:, 0:KV_DIM] = kvw[:, 0:KV_DIM] * cos + kwrot * sin
    kvw_ref[:, KV_DIM:] = kvw[:, KV_DIM:]
    gl_ref[...] = jax.nn.sigmoid(mm(C_GL, C_END))


def _proj(x2d, g, wcat, cos, sin, tm):
    m, d = x2d.shape
    nper = cos.shape[0] // tm
    row = lambda w: pl.BlockSpec((tm, w), lambda i: (i, 0))
    tab = pl.BlockSpec((tm, LANES), lambda i: (i % nper, 0))
    widths = (512, 512, 512, 512, 256, 256, 256, 128)
    dtypes = (F32, F32, BF16, BF16, F32, F32, F32, F32)
    return pl.pallas_call(
        _proj_kernel,
        grid=(m // tm,),
        in_specs=[row(d), pl.BlockSpec((1, d), lambda i: (0, 0)),
                  pl.BlockSpec(wcat.shape, lambda i: (0, 0)), tab, tab],
        out_specs=[row(w) for w in widths],
        out_shape=[jax.ShapeDtypeStruct((m, w), dt) for w, dt in zip(widths, dtypes)],
        compiler_params=_cparams(("arbitrary",)),
        name="proj",
    )(x2d, g, wcat, cos, sin)


def _rglru_kernel(xr_ref, gate_ref, pre_ref, h0_ref, cw_ref, cb_ref, wa_ref, wx_ref, ba_ref, bx_ref, lam_ref,
                  y_ref, htail_ref, xpad, hc, a_s, b_s):
    tt = xr_ref.shape[1]
    j = pl.program_id(1)

    @pl.when(j == 0)
    def _():
        xpad[0:8, :] = pre_ref[0]
        hc[...] = jnp.broadcast_to(h0_ref[0], hc.shape)

    @pl.when(j > 0)
    def _():
        xpad[0:8, :] = xpad[tt:tt + 8, :]

    xpad[8:8 + tt, :] = xr_ref[0]
    xc = cb_ref[...]
    for k in range(4):
        xc = xc + xpad[pl.ds(5 + k, tt), :] * cw_ref[k:k + 1, :]
    xb = xc.astype(BF16)
    r = jax.nn.sigmoid(_mm(xb, wa_ref[...]) + ba_ref[...])
    i = jax.nn.sigmoid(_mm(xb, wx_ref[...]) + bx_ref[...])
    log_a = RG_C * r * jax.nn.log_sigmoid(lam_ref[...])
    a_s[...] = jnp.exp(log_a)
    b_s[...] = jnp.sqrt(-_expm1(2.0 * log_a)) * (i * xc)

    row = lax.broadcasted_iota(jnp.int32, (8, a_s.shape[1]), 0)

    def body(gi, h):
        r0 = pl.multiple_of(gi * 8, 8)
        av = a_s[pl.ds(r0, 8), :]
        bv = b_s[pl.ds(r0, 8), :]
        for s in (1, 2, 4):
            keep = row >= s
            a_sh = pltpu.roll(av, s, 0)
            b_sh = pltpu.roll(bv, s, 0)
            bv = jnp.where(keep, av * b_sh + bv, bv)
            av = jnp.where(keep, av * a_sh, av)
        hs = av * h + bv
        b_s[pl.ds(r0, 8), :] = hs
        return jnp.broadcast_to(hs[7:8, :], hs.shape)

    h = lax.fori_loop(0, tt // 8, body, hc[...])
    hc[...] = h
    htail_ref[0] = b_s[tt - 8:tt, :]
    y_ref[0] = (b_s[...] * jax.nn.gelu(gate_ref[0])).astype(y_ref.dtype)


def _rglru(xr, gate, pre8, h0, cw, cb, wa, wx, ba, bx, lam, tt):
    b, t, c = xr.shape
    full = lambda a: pl.BlockSpec(a.shape, lambda i, j: (0,) * a.ndim)
    return pl.pallas_call(
        _rglru_kernel,
        grid=(b, t // tt),
        in_specs=[pl.BlockSpec((1, tt, c), lambda i, j: (i, j, 0)),
                  pl.BlockSpec((1, tt, c), lambda i, j: (i, j, 0)),
                  pl.BlockSpec((1, 8, c), lambda i, j: (i, 0, 0)),
                  pl.BlockSpec((1, 1, c), lambda i, j: (i, 0, 0)),
                  full(cw), full(cb), full(wa), full(wx), full(ba), full(bx), full(lam)],
        out_specs=[pl.BlockSpec((1, tt, c), lambda i, j: (i, j, 0)),
                   pl.BlockSpec((1, 8, c), lambda i, j: (i, 0, 0))],
        out_shape=[jax.ShapeDtypeStruct((b, t, c), BF16), jax.ShapeDtypeStruct((b, 8, c), F32)],
        scratch_shapes=[pltpu.VMEM((tt + 8, c), F32), pltpu.VMEM((8, c), F32),
                        pltpu.VMEM((tt, c), F32), pltpu.VMEM((tt, c), F32)],
        compiler_params=_cparams(("arbitrary", "arbitrary")),
        name="rglru",
    )(xr, gate, pre8, h0, cw, cb, wa, wx, ba, bx, lam)


def _cmp_copies(pt_ref, pool_ref, buf, sem, first_page, slot, p):
    page = pt_ref[first_page + p]
    return [pltpu.make_async_copy(pool_ref.at[page, :, pl.ds(kv * KV_DIM, KV_DIM)],
                                  buf.at[slot, kv, pl.ds(p * 128, 128), :], sem.at[slot]) for kv in range(2)]


def _cmp_kernel(pt_ref, pool_ref, pea_ref, peb_ref, w1a_ref, w1b_ref, b1_ref, w2_ref, out_ref,
                buf, sem, ua_s, *, pp):
    nb, nt = pl.num_programs(0), pl.num_programs(1)
    j = pl.program_id(1)
    step = pl.program_id(0) * nt + j
    slot = step % 2
    m = pp * 8

    def fetch(st, sl):
        for p in range(pp):
            for cp in _cmp_copies(pt_ref, pool_ref, buf, sem, st * pp, sl, p):
                cp.start()

    @pl.when(step == 0)
    def _():
        fetch(step, slot)

    @pl.when(step + 1 < nb * nt)
    def _():
        fetch(step + 1, 1 - slot)

    for p in range(pp):
        for cp in _cmp_copies(pt_ref, pool_ref, buf, sem, step * pp, slot, p):
            cp.wait()

    acc_a = [jnp.zeros((m, 256), F32)] * 2
    acc_b = [jnp.zeros((m, 256), F32)] * 2
    for s in range(CMP_STRIDE):
        for kv in range(2):
            xs = buf[slot, kv, pl.ds(s, m, stride=CMP_STRIDE), :]
            lanes = slice(kv * KV_DIM, (kv + 1) * KV_DIM)
            acc_a[kv] = acc_a[kv] + _mm((xs + pea_ref[s:s + 1, lanes]).astype(BF16), w1a_ref[s, kv])
            acc_b[kv] = acc_b[kv] + _mm((xs + peb_ref[s:s + 1, lanes]).astype(BF16), w1b_ref[s, kv])
    acc_a = jnp.concatenate(acc_a, axis=1)
    acc_b = jnp.concatenate(acc_b, axis=1)

    @pl.when(j == 0)
    def _():
        ua_s[0:8, :] = jnp.zeros((8, 512), F32)

    @pl.when(j > 0)
    def _():
        ua_s[0:8, :] = ua_s[m:m + 8, :]

    ua_s[8:8 + m, :] = acc_a
    hid = jax.nn.gelu(ua_s[pl.ds(7, m), :] + acc_b + b1_ref[...])
    out_ref[0] = _mm(hid.astype(BF16), w2_ref[...]).astype(out_ref.dtype)


def _compress(pool, page_table, cw, pp):
    b, npg = page_table.shape
    nt = npg // pp
    m = pp * 8
    full = lambda a: pl.BlockSpec(a.shape, lambda i, j, pt: (0,) * a.ndim)
    pea, peb, w1a, w1b, b1, w2 = cw
    gs = pltpu.PrefetchScalarGridSpec(
        num_scalar_prefetch=1,
        grid=(b, nt),
        in_specs=[pl.BlockSpec(memory_space=pl.ANY), full(pea), full(peb), full(w1a), full(w1b), full(b1), full(w2)],
        out_specs=pl.BlockSpec((1, m, 256), lambda i, j, pt: (i, j, 0)),
        scratch_shapes=[pltpu.VMEM((2, 2, pp * 128, KV_DIM), F32), pltpu.SemaphoreType.DMA((2,)),
                        pltpu.VMEM((m + 8, 512), F32)],
    )
    return pl.pallas_call(
        functools.partial(_cmp_kernel, pp=pp),
        grid_spec=gs,
        out_shape=jax.ShapeDtypeStruct((b, nt * m, 256), BF16),
        compiler_params=_cparams(("arbitrary", "arbitrary")),
        name="compress",
    )(page_table.reshape(-1), pool, pea, peb, w1a, w1b, b1, w2)


def _softmax_rows(s, mask):
    sm = jnp.where(mask, s, -BIG)
    mx = jnp.max(sm, axis=-1, keepdims=True)
    e = jnp.where(mask, jnp.exp(sm - mx), 0.0)
    return e / jnp.maximum(jnp.sum(e, axis=-1, keepdims=True), 1e-30)


def _select_bias_t(score_t, lag, jidx, n_blocks, k_sel):
    forced = (jidx == 0) | ((lag >= 0) & (lag < N_LOCAL))
    sc = jnp.where(forced, BIG, score_t)
    sc = jnp.where(lag >= 0, sc, -BIG)
    rank = jnp.zeros(sc.shape, F32)
    for i in range(n_blocks):
        si = sc[i:i + 1, :]
        rank = rank + jnp.where(jidx > i, jnp.where(si >= sc, 1.0, 0.0), jnp.where(si > sc, 1.0, 0.0))
    return jnp.where(rank < k_sel, 0.0, NEG)


def _nsa_prompt_kernel(qc_ref, qr_ref, g_ref, kck_ref, kcv_ref, ksa_ref, vs_ref, kw_ref, vw_ref, pool_ref,
                       o_ref, *, t_len, kc_len, wl):
    qb = pl.program_id(1)
    start = qb * Q_BLOCK
    nc = kck_ref.shape[2]
    nj = pool_ref.shape[0]
    rows = GROUP * Q_BLOCK
    qpos = start + (lax.broadcasted_iota(jnp.int32, (rows, 1), 0) & (Q_BLOCK - 1))

    for hkv in range(N_KV):
        q4c = qc_ref[0, hkv, 0]
        s = _nt(q4c, kck_ref[0, hkv])
        mrow = lax.broadcasted_iota(jnp.int32, (rows, nc), 1)
        valid = (mrow >= 1) & (mrow * CMP_STRIDE + (CMP_STRIDE - 1) <= qpos)
        p = _softmax_rows(s, valid)
        o_cmp = _mm(p.astype(BF16), kcv_ref[0, hkv])
        psum = p[0:Q_BLOCK] + p[Q_BLOCK:2 * Q_BLOCK] + p[2 * Q_BLOCK:3 * Q_BLOCK] + p[3 * Q_BLOCK:]
        score_t = _nt(pool_ref[...], psum, precision=lax.Precision.HIGHEST)
        jidx = lax.broadcasted_iota(jnp.int32, (nj, Q_BLOCK), 0)
        qp_l = start + lax.broadcasted_iota(jnp.int32, (nj, Q_BLOCK), 1)
        lag = (qp_l >> 6) - jidx
        bias_t = _select_bias_t(score_t, lag, jidx, nj, min(N_SELECT, nj))
        pad_t = jnp.concatenate([jnp.zeros((HEAD_DIM, Q_BLOCK), F32), bias_t]
                                + ([jnp.zeros((HEAD_DIM - nj, Q_BLOCK), F32)] if nj < HEAD_DIM else []), axis=0)
        bias = pad_t.T
        lane = lax.broadcasted_iota(jnp.int32, (Q_BLOCK, LANES), 1)
        qaug = []
        for g in range(GROUP):
            qg = qr_ref[0, hkv, 0, g * Q_BLOCK:(g + 1) * Q_BLOCK, :].astype(F32)
            qaug.append(jnp.where(lane < HEAD_DIM, qg, bias).astype(BF16))
        qaug = jnp.concatenate(qaug, axis=0)

        n_chunks = (start + Q_BLOCK + kc_len - 1) // kc_len

        def chunk(c, carry):
            m_i, l_i, acc = carry
            k0 = pl.multiple_of(c * kc_len, kc_len)
            sc = _nt(qaug, ksa_ref[0, hkv, pl.ds(k0, kc_len), :])
            kpos = k0 + lax.broadcasted_iota(jnp.int32, (rows, kc_len), 1)
            sc = jnp.where(kpos <= qpos, sc, NEG)
            m_n = jnp.maximum(m_i, jnp.max(sc, axis=-1, keepdims=True))
            alpha = jnp.exp(m_i - m_n)
            pe = jnp.exp(sc - m_n)
            l_n = alpha * l_i + jnp.sum(pe, axis=-1, keepdims=True)
            acc = alpha * acc + _mm(pe.astype(BF16), vs_ref[0, hkv, pl.ds(k0, kc_len), :])
            return m_n, l_n, acc

        m0 = jnp.full((rows, 1), -3e38, F32)
        l0 = jnp.zeros((rows, 1), F32)
        a0 = jnp.zeros((rows, HEAD_DIM), F32)
        _, l_f, acc_f = lax.fori_loop(0, n_chunks, chunk, (m0, l0, a0))
        o_slc = acc_f / jnp.maximum(l_f, 1e-30)

        base = pl.multiple_of(jnp.maximum(start + Q_BLOCK - wl, 0), Q_BLOCK)
        q4r = qr_ref[0, hkv, 0][:, 0:HEAD_DIM]
        sw = _nt(q4r, kw_ref[0, hkv, pl.ds(base, wl), :])
        dist = qpos - (base + lax.broadcasted_iota(jnp.int32, (rows, wl), 1))
        pw = _softmax_rows(sw, (dist >= 0) & (dist < WINDOW))
        o_win = _mm(pw.astype(BF16), vw_ref[0, hkv, pl.ds(base, wl), :])

        gt = g_ref[0, hkv, 0]
        o = gt[:, 0:1] * o_cmp + gt[:, 1:2] * o_slc + gt[:, 2:3] * o_win
        o_ref[0, hkv, 0] = o.astype(o_ref.dtype)


def _nsa_prompt(qc, qr, gt, kck, kcv, ksa, vs, kw, vw, pool_t, t_len):
    b = qc.shape[0]
    nqb = t_len // Q_BLOCK
    rows = GROUP * Q_BLOCK
    kc_len = _div_tile(t_len, 512)
    wl = min(WINDOW + Q_BLOCK, t_len)
    qspec = lambda w: pl.BlockSpec((1, N_KV, 1, rows, w), lambda i, j: (i, 0, j, 0, 0))
    seq = lambda a: pl.BlockSpec((1,) + a.shape[1:], lambda i, j: (i,) + (0,) * (a.ndim - 1))
    return pl.pallas_call(
        functools.partial(_nsa_prompt_kernel, t_len=t_len, kc_len=kc_len, wl=wl),
        grid=(b, nqb),
        in_specs=[qspec(HEAD_DIM), qspec(LANES), qspec(8), seq(kck), seq(kcv), seq(ksa), seq(vs), seq(kw), seq(vw),
                  pl.BlockSpec(pool_t.shape, lambda i, j: (0, 0))],
        out_specs=qspec(HEAD_DIM),
        out_shape=jax.ShapeDtypeStruct((b, N_KV, nqb, rows, HEAD_DIM), BF16),
        compiler_params=_cparams(("arbitrary", "arbitrary")),
        name="nsa_prompt",
    )(qc, qr, gt, kck, kcv, ksa, vs, kw, vw, pool_t)


def _nsa_copy(pt_ref, pool_ref, buf, sem, first_page, slot, p):
    page = pt_ref[first_page + p]
    return pltpu.make_async_copy(pool_ref.at[page], buf.at[slot, pl.ds(p * 128, 128), :], sem.at[slot])


def _nsa_sample_kernel(pt_ref, qc_ref, qr_ref, g_ref, kc_ref, pool_ref, exp_ref, pmap_ref, new_ref, win_ref,
                       o_ref, buf, sem, bias_s, sc_s, m_s, l_s, acc_s, ocmp_s, *, pp, past_len, n_tok, nj):
    nb, nt = pl.num_programs(0), pl.num_programs(1)
    j = pl.program_id(1)
    step = pl.program_id(0) * nt + j
    slot = step % 2
    rows = GROUP * N_KV * n_tok
    tk = pp * 128
    bpt = tk // SLC_BLOCK

    def fetch(st, sl):
        for p in range(pp):
            _nsa_copy(pt_ref, pool_ref, buf, sem, st * pp, sl, p).start()

    @pl.when(step == 0)
    def _():
        fetch(step, slot)

    @pl.when(step + 1 < nb * nt)
    def _():
        fetch(step + 1, 1 - slot)

    rid = lax.broadcasted_iota(jnp.int32, (rows, 1), 0)
    tok = rid % n_tok
    top_half = ((rid // n_tok) % N_KV) == 0
    qpos = past_len + tok

    @pl.when(j == 0)
    def _():
        kc = kc_ref[0]
        nc = kc.shape[0]
        s = _nt(qc_ref[0], kc[:, 0:KV_DIM])
        mrow = lax.broadcasted_iota(jnp.int32, (rows, nc), 1)
        valid = (mrow >= 1) & (mrow * CMP_STRIDE + (CMP_STRIDE - 1) <= qpos)
        p = _softmax_rows(s, valid)
        ocmp_s[...] = _mm(p.astype(BF16), kc[:, KV_DIM:])
        r8 = N_KV * n_tok
        psum = p[0:r8] + p[r8:2 * r8] + p[2 * r8:3 * r8] + p[3 * r8:]
        score_t = _nt(pmap_ref[...], psum, precision=lax.Precision.HIGHEST)
        njp = score_t.shape[0]
        jidx = lax.broadcasted_iota(jnp.int32, (njp, r8), 0)
        qp_l = past_len + lax.broadcasted_iota(jnp.int32, (njp, r8), 1) % n_tok
        lag = (qp_l >> 6) - jidx
        forced = (jidx == 0) | ((lag >= 0) & (lag < N_LOCAL))
        sc = jnp.where(forced, BIG, score_t)
        sc = jnp.where((lag >= 0) & (jidx < nj), sc, -BIG)
        sc_s[...] = jnp.zeros(sc_s.shape, F32)
        sc_s[:, 0:r8] = sc
        scv = sc_s[...]
        jfull = lax.broadcasted_iota(jnp.int32, scv.shape, 0)

        def rank_body(i, rank):
            si = sc_s[pl.ds(i, 1), :]
            return rank + jnp.where(jfull > i, jnp.where(si >= scv, 1.0, 0.0), jnp.where(si > scv, 1.0, 0.0))

        rank = lax.fori_loop(0, nj, rank_body, jnp.zeros(scv.shape, F32))
        bias_t = jnp.where(rank < min(N_SELECT, nj), 0.0, NEG)
        for tj in range(njp // bpt):
            blk = bias_t[tj * bpt:(tj + 1) * bpt, :]
            sq = jnp.concatenate([blk, jnp.zeros((LANES - bpt, LANES), F32)], axis=0).T
            b8 = sq[0:r8, :]
            bias_s[tj] = jnp.concatenate([b8] * GROUP, axis=0)
        m_s[...] = jnp.full(m_s.shape, -3e38, F32)
        l_s[...] = jnp.zeros(l_s.shape, F32)
        acc_s[...] = jnp.zeros(acc_s.shape, F32)

    for p in range(pp):
        _nsa_copy(pt_ref, pool_ref, buf, sem, step * pp, slot, p).wait()

    def online(sc, v_bf):
        m_i = m_s[...]
        m_n = jnp.maximum(m_i, jnp.max(sc, axis=-1, keepdims=True))
        alpha = jnp.exp(m_i - m_n)
        pe = jnp.exp(sc - m_n)
        l_s[...] = alpha * l_s[...] + jnp.sum(pe, axis=-1, keepdims=True)
        acc_s[...] = alpha * acc_s[...] + _mm(pe.astype(BF16), v_bf)
        m_s[...] = m_n

    qr = qr_ref[0]
    kt = buf[slot, :, 0:KV_DIM].astype(BF16)
    vt = buf[slot, :, KV_DIM:].astype(BF16)
    sc = _nt(qr, kt) + _mm(bias_s[j].astype(BF16), exp_ref[...])
    online(sc, vt)

    @pl.when(j == nt - 1)
    def _():
        new = new_ref[0]
        nk = new.shape[0]
        kidx = lax.broadcasted_iota(jnp.int32, (rows, nk), 1)
        last_bias = bias_s[(past_len // SLC_BLOCK) // bpt][:, (past_len // SLC_BLOCK) % bpt:(past_len // SLC_BLOCK) % bpt + 1]
        scn = _nt(qr, new[:, 0:KV_DIM].astype(BF16)) + last_bias
        scn = jnp.where((kidx <= tok) & (kidx < n_tok), scn, NEG)
        online(scn, new[:, KV_DIM:2 * KV_DIM].astype(BF16))
        o_slc = acc_s[...] / jnp.maximum(l_s[...], 1e-30)

        wk = win_ref[0]
        nbuf = wk.shape[0]
        s1 = _nt(qr, wk[:, 0:KV_DIM].astype(BF16))
        d1 = tok + nbuf - lax.broadcasted_iota(jnp.int32, (rows, nbuf), 1)
        ok1 = (d1 >= 0) & (d1 < WINDOW)
        s2 = _nt(qr, new[:, 2 * KV_DIM:3 * KV_DIM].astype(BF16))
        d2 = tok - kidx
        ok2 = (d2 >= 0) & (d2 < WINDOW) & (kidx < n_tok)
        s1 = jnp.where(ok1, s1, -BIG)
        s2 = jnp.where(ok2, s2, -BIG)
        mx = jnp.maximum(jnp.max(s1, axis=-1, keepdims=True), jnp.max(s2, axis=-1, keepdims=True))
        e1 = jnp.where(ok1, jnp.exp(s1 - mx), 0.0)
        e2 = jnp.where(ok2, jnp.exp(s2 - mx), 0.0)
        den = jnp.maximum(jnp.sum(e1, axis=-1, keepdims=True) + jnp.sum(e2, axis=-1, keepdims=True), 1e-30)
        o_win = (_mm(e1.astype(BF16), wk[:, KV_DIM:].astype(BF16))
                 + _mm(e2.astype(BF16), new[:, 3 * KV_DIM:].astype(BF16))) / den
        gt = g_ref[0]
        o_ref[0] = gt[:, 0:1] * ocmp_s[...] + gt[:, 1:2] * o_slc + gt[:, 2:3] * o_win


def _nsa_sample(page_table, qc, qr, gt, kc, pool, expand, pmap, new, win, pp, past_len, n_tok, nj):
    b, npg = page_table.shape
    nt = npg // pp
    rows = GROUP * N_KV * n_tok
    tk = pp * 128
    bpt = tk // SLC_BLOCK
    njp = pmap.shape[0]
    seq = lambda a: pl.BlockSpec((1,) + a.shape[1:], lambda i, j, pt: (i,) + (0,) * (a.ndim - 1))
    full = lambda a: pl.BlockSpec(a.shape, lambda i, j, pt: (0,) * a.ndim)
    gs = pltpu.PrefetchScalarGridSpec(
        num_scalar_prefetch=1,
        grid=(b, nt),
        in_specs=[seq(qc), seq(qr), seq(gt), seq(kc), pl.BlockSpec(memory_space=pl.ANY), full(expand), full(pmap),
                  seq(new), seq(win)],
        out_specs=pl.BlockSpec((1, rows, LANES), lambda i, j, pt: (i, 0, 0)),
        scratch_shapes=[pltpu.VMEM((2, tk, 256), F32), pltpu.SemaphoreType.DMA((2,)),
                        pltpu.VMEM((njp // bpt, rows, LANES), F32), pltpu.VMEM((njp, LANES), F32),
                        pltpu.VMEM((rows, 1), F32), pltpu.VMEM((rows, 1), F32),
                        pltpu.VMEM((rows, LANES), F32), pltpu.VMEM((rows, LANES), F32)],
    )
    return pl.pallas_call(
        functools.partial(_nsa_sample_kernel, pp=pp, past_len=past_len, n_tok=n_tok, nj=nj),
        grid_spec=gs,
        out_shape=jax.ShapeDtypeStruct((b, rows, LANES), F32),
        compiler_params=_cparams(("arbitrary", "arbitrary")),
        name="nsa_sample",
    )(page_table.reshape(-1), qc, qr, gt, kc, pool, expand, pmap, new, win)


def _merge_kernel(y_ref, o_ref, w_ref, g_ref, x_ref, out_ref):
    half = y_ref.shape[1]
    m = _mm(y_ref[...], w_ref[0:half, :]) + _mm(o_ref[...], w_ref[half:, :])
    out_ref[...] = x_ref[...] + _rms(m, g_ref[...])


def _merge(y, o, w, g, x, tm):
    m, d = x.shape
    half = y.shape[1]
    return pl.pallas_call(
        _merge_kernel,
        grid=(m // tm,),
        in_specs=[pl.BlockSpec((tm, half), lambda i: (i, 0)), pl.BlockSpec((tm, half), lambda i: (i, 0)),
                  pl.BlockSpec(w.shape, lambda i: (0, 0)), pl.BlockSpec((1, d), lambda i: (0, 0)),
                  pl.BlockSpec((tm, d), lambda i: (i, 0))],
        out_specs=pl.BlockSpec((tm, d), lambda i: (i, 0)),
        out_shape=jax.ShapeDtypeStruct((m, d), F32),
        compiler_params=_cparams(("arbitrary",)),
        name="merge",
    )(y, o, w, g, x)


def _ffn_kernel(x_ref, pre_ref, gpre_ref, gpost_ref, wu_ref, wg_ref, cw_ref, cb_ref, wd_ref,
                out_ref, tail_ref, xn_s, gpad, carry, acc, *, shift, padr):
    j, f = pl.program_id(1), pl.program_id(2)
    nf = pl.num_programs(2)
    tm = x_ref.shape[1]

    @pl.when(f == 0)
    def _():
        xn_s[...] = _rms(x_ref[0], gpre_ref[...]).astype(BF16)
        acc[...] = jnp.zeros(acc.shape, F32)

    @pl.when(j == 0)
    def _():
        gpad[0:padr, :] = pre_ref[0]

    @pl.when(j > 0)
    def _():
        gpad[0:padr, :] = carry[f]

    xn = xn_s[...]
    u = _mm(xn, wu_ref[...])
    gpad[padr:padr + tm, :] = _mm(xn, wg_ref[...])
    gc = cb_ref[...]
    for k in range(3):
        gc = gc + gpad[pl.ds(padr - (2 - k) * shift, tm), :] * cw_ref[k:k + 1, :]
    tail = gpad[tm:tm + padr, :]
    carry[f] = tail
    tail_ref[0, 0] = tail
    acc[...] += _mm((jax.nn.gelu(gc) * u).astype(BF16), wd_ref[...])

    @pl.when(f == nf - 1)
    def _():
        out_ref[0] = x_ref[0] + _rms(acc[...], gpost_ref[...])


def _ffn(x, pre, gpre, gpost, wu, wg, cw, cb, wd, tm, tf, shift):
    b, t, d = x.shape
    dff = wu.shape[1]
    padr = pre.shape[1]
    nt, nf = t // tm, dff // tf
    return pl.pallas_call(
        functools.partial(_ffn_kernel, shift=shift, padr=padr),
        grid=(b, nt, nf),
        in_specs=[pl.BlockSpec((1, tm, d), lambda i, j, f: (i, j, 0)),
                  pl.BlockSpec((1, padr, tf), lambda i, j, f: (i, 0, f)),
                  pl.BlockSpec((1, d), lambda i, j, f: (0, 0)),
                  pl.BlockSpec((1, d), lambda i, j, f: (0, 0)),
                  pl.BlockSpec((d, tf), lambda i, j, f: (0, f)),
                  pl.BlockSpec((d, tf), lambda i, j, f: (0, f)),
                  pl.BlockSpec((3, tf), lambda i, j, f: (0, f)),
                  pl.BlockSpec((1, tf), lambda i, j, f: (0, f)),
                  pl.BlockSpec((tf, d), lambda i, j, f: (f, 0))],
        out_specs=[pl.BlockSpec((1, tm, d), lambda i, j, f: (i, j, 0)),
                   pl.BlockSpec((1, 1, padr, tf), lambda i, j, f: (i, j, 0, f))],
        out_shape=[jax.ShapeDtypeStruct((b, t, d), F32), jax.ShapeDtypeStruct((b, nt, padr, dff), F32)],
        scratch_shapes=[pltpu.VMEM((tm, d), BF16), pltpu.VMEM((padr + tm, tf), F32),
                        pltpu.VMEM((nf, padr, tf), F32), pltpu.VMEM((tm, d), F32)],
        compiler_params=_cparams(("arbitrary", "arbitrary", "arbitrary")),
        name="ffn",
    )(x, pre, gpre, gpost, wu, wg, cw, cb, wd)


def _rot_cols(w):
    d, n = w.shape
    w4 = w.reshape(d, n // HEAD_DIM, 2, HEAD_DIM // 2)
    return jnp.concatenate([-w4[:, :, 1], w4[:, :, 0]], axis=2).reshape(d, n)


def _rope_tables(pos):
    half = HEAD_DIM // 2
    freq = ROPE_THETA ** (-jnp.arange(half, dtype=F32) / half)
    ang = pos.astype(F32)[:, None] * freq[None, :]
    cos, sin = jnp.cos(ang), jnp.sin(ang)
    return jnp.concatenate([cos] * 4, axis=1), jnp.concatenate([sin] * 4, axis=1)


def _block_diag(w):
    n, c, d = w.shape
    return jnp.einsum('ncd,nm->ncmd', w, jnp.eye(n, dtype=w.dtype)).reshape(n * c, n * d)


def _layer_weights(l, p):
    d_model = p['w_in'].shape[1]
    w = p['w_in'][l]
    sizes = [512, 512, 512, 256, 256, 256, 3 * N_HEADS]
    cuts = np.cumsum([0] + sizes)
    xr, gate, q, kvc, kvs, kvw, gl = [w[:, cuts[i]:cuts[i + 1]] for i in range(7)]
    gl_pad = jnp.pad(gl, ((0, 0), (0, LANES - gl.shape[1])))
    wcat = jnp.concatenate([xr, gate, q, _rot_cols(q), kvc, kvs, _rot_cols(kvs[:, :KV_DIM]),
                            kvw, _rot_cols(kvw[:, :KV_DIM]), gl_pad], axis=1).astype(BF16)
    row = lambda v: v.reshape(1, -1)
    rg = (p['rg_conv_w'][l], row(p['rg_conv_b'][l]), _block_diag(p['rg_wa'][l]).astype(BF16),
          _block_diag(p['rg_wx'][l]).astype(BF16), row(p['rg_ba'][l]), row(p['rg_bx'][l]), row(p['rg_lambda'][l]))
    sel = lambda a, b: jnp.stack([a, a, b, b])
    eye2 = jnp.eye(N_KV, dtype=F32)
    w1 = jnp.stack([p['cmpk_w1'][l], p['cmpv_w1'][l]])
    w1bd = jnp.einsum('ksdf,he->skhdef', w1, eye2).reshape(2 * CMP_STRIDE, 2, KV_DIM, 2 * KV_DIM)
    pe = sel(p['cmpk_pe'][l], p['cmpv_pe'][l])
    pe = pe.transpose(1, 0, 2).reshape(2 * CMP_STRIDE, 4 * HEAD_DIM)
    b1 = sel(p['cmpk_b1'][l], p['cmpv_b1'][l]).reshape(1, -1)
    w2 = _block_diag(sel(p['cmpk_w2'][l], p['cmpv_w2'][l]))
    cmpw = (pe[:CMP_STRIDE], pe[CMP_STRIDE:], w1bd[:CMP_STRIDE].astype(BF16), w1bd[CMP_STRIDE:].astype(BF16),
            b1, w2.astype(BF16))
    dff = p['ffn_w_in'].shape[2] // 2
    ffn = (p['ffn_w_in'][l][:, :dff].astype(BF16), p['ffn_w_in'][l][:, dff:].astype(BF16),
           p['ffn_conv_w'][l], row(p['ffn_conv_b'][l]), p['ffn_w_down'][l].astype(BF16))
    norms = tuple(row(p[k][l]) for k in ('norm_mix_pre', 'norm_mix_post', 'norm_ffn_pre', 'norm_ffn_post'))
    return dict(wcat=wcat, rg=rg, cmp=cmpw, w_out=p['w_out'][l].astype(BF16), ffn=ffn, norms=norms)


def _pool_map(n_slc_pad, n_rows):
    j = np.arange(n_slc_pad)[:, None]
    m = np.arange(n_rows)[None, :]
    r = SLC_BLOCK // CMP_STRIDE
    return jnp.asarray(((m >= r * j) & (m <= r * j + r)).astype(np.float32))


def _heads_major(a, b, t):
    a5 = a.reshape(b, t, 2, N_KV, HEAD_DIM)
    return a5[:, :, 0].transpose(0, 2, 1, 3), a5[:, :, 1].transpose(0, 2, 1, 3)


def _prompt_mixer(hp, lw, t_cos, t_sin):
    b, t, d = hp.shape
    g_pre, g_post = lw['norms'][0], lw['norms'][1]
    xr, gate, qc, qr, kvc, kvs, kvw, gl = _proj(hp.reshape(b * t, d), g_pre, lw['wcat'], t_cos, t_sin,
                                                 _div_tile(t, 512))
    c = xr.shape[1]
    y, htail = _rglru(xr.reshape(b, t, c), gate.reshape(b, t, c), jnp.zeros((b, 8, c), F32),
                      jnp.zeros((b, 1, c), F32), *lw['rg'], _div_tile(t, 512))
    npg = t // 128
    ident = jnp.arange(b * npg, dtype=jnp.int32).reshape(b, npg)
    kcs = _compress(kvc.reshape(b * npg, 128, 256), ident, lw['cmp'], _div_tile(npg, 32))
    kck, kcv = _heads_major(kcs, b, t // CMP_STRIDE)
    nqb = t // Q_BLOCK
    n_slc = -(-t // SLC_BLOCK)
    ks, vs = _heads_major(kvs, b, t)
    onehot = (jnp.arange(t)[:, None] // SLC_BLOCK == jnp.arange(HEAD_DIM)[None, :]).astype(BF16)
    ksa = jnp.concatenate([ks.astype(BF16), jnp.broadcast_to(onehot, (b, N_KV, t, HEAD_DIM))], axis=-1)
    kw, vw = _heads_major(kvw, b, t)
    regroup = lambda a, w: a.reshape(b, nqb, Q_BLOCK, N_KV, GROUP, w).transpose(0, 3, 1, 4, 2, 5).reshape(
        b, N_KV, nqb, GROUP * Q_BLOCK, w)
    qc5 = regroup(qc, HEAD_DIM)
    qr5 = jnp.pad(regroup(qr, HEAD_DIM), ((0, 0),) * 4 + ((0, LANES - HEAD_DIM),))
    gt5 = jnp.pad(regroup(gl[:, :3 * N_HEADS], 3), ((0, 0),) * 4 + ((0, 5),))
    o5 = _nsa_prompt(qc5, qr5, gt5, kck, kcv, ksa, vs.astype(BF16), kw.astype(BF16), vw.astype(BF16),
                     _pool_map(n_slc, t // CMP_STRIDE), t)
    o = o5.reshape(b, N_KV, nqb, GROUP, Q_BLOCK, HEAD_DIM).transpose(0, 2, 4, 1, 3, 5).reshape(b * t, N_HEADS * HEAD_DIM)
    hp = _merge(y.reshape(b * t, c), o, lw['w_out'], g_post, hp.reshape(b * t, d), _div_tile(b * t, 512))
    win_buf = min(WINDOW, t)
    st = (kvc.reshape(b, t, 2, N_KV, HEAD_DIM), kvs.reshape(b, t, 2, N_KV, HEAD_DIM),
          kvw.reshape(b, t, 2, N_KV, HEAD_DIM)[:, -win_buf:], htail[:, 7], xr.reshape(b, t, c)[:, t - 3:])
    return hp.reshape(b, t, d), st


def _sample_mixer(hs, lw, s_cos, s_sin, cmp_pool, slc_pool, page_table, win_cache, rg_h0, rg_conv0):
    b, t, d = hs.shape
    past_len = page_table.shape[1] * cmp_pool.shape[1]
    g_pre, g_post = lw['norms'][0], lw['norms'][1]
    xr, gate, qc, qr, kvc, kvs, kvw, gl = _proj(hs.reshape(b * t, d), g_pre, lw['wcat'], s_cos, s_sin, b * t)
    c = xr.shape[1]
    padt = lambda a: jnp.pad(a.reshape(b, t, -1), ((0, 0), (0, 8 - t), (0, 0)))
    pre8 = jnp.pad(rg_conv0, ((0, 0), (5, 0), (0, 0)))
    y8, htail = _rglru(padt(xr), padt(gate), pre8, rg_h0.reshape(b, 1, c), *lw['rg'], 8)
    y = y8[:, :t].reshape(b * t, c)
    npg = page_table.shape[1]
    pool_c = cmp_pool.reshape(cmp_pool.shape[0], 128, 256)
    kcs = _compress(pool_c, page_table, lw['cmp'], _div_tile(npg, 32))
    nj = -(-(past_len + t) // SLC_BLOCK)
    pp = _div_tile(npg, 16)
    bpt = pp * 128 // SLC_BLOCK
    njp = -(-nj // bpt) * bpt
    rows = GROUP * N_KV * t

    def qrows(a):
        a5 = a.reshape(b, t, N_KV, GROUP, HEAD_DIM).transpose(0, 3, 2, 1, 4)
        z = jnp.zeros_like(a5[:, :, 0])
        top = jnp.concatenate([a5[:, :, 0], z], axis=-1)
        bot = jnp.concatenate([z, a5[:, :, 1]], axis=-1)
        return jnp.stack([top, bot], axis=2).reshape(b, rows, LANES)

    gts = gl[:, :3 * N_HEADS].reshape(b, t, N_KV, GROUP, 3).transpose(0, 3, 2, 1, 4).reshape(b, rows, 3)
    gts = jnp.pad(gts, ((0, 0), (0, 0), (0, 5)))
    expand = (jnp.arange(LANES)[:, None] == jnp.arange(pp * 128)[None, :] // SLC_BLOCK).astype(BF16)
    new = jnp.concatenate([kvs.reshape(b, t, 256), kvw.reshape(b, t, 256)], axis=-1)
    new = jnp.pad(new, ((0, 0), (0, 8 - t), (0, 0)))
    win = win_cache.reshape(b, win_cache.shape[1], 256)
    o32 = _nsa_sample(page_table, qrows(qc), qrows(qr), gts, kcs, slc_pool.reshape(slc_pool.shape[0], 128, 256),
                      expand, _pool_map(njp, kcs.shape[1]), new, win, pp, past_len, t, nj)
    o5 = o32.reshape(b, GROUP, N_KV, t, N_KV, HEAD_DIM)
    o = jnp.stack([o5[:, :, 0, :, 0], o5[:, :, 1, :, 1]], axis=1)
    o = o.transpose(0, 3, 1, 2, 4).reshape(b * t, N_HEADS * HEAD_DIM).astype(BF16)
    hs = _merge(y, o, lw['w_out'], g_post, hs.reshape(b * t, d), b * t)
    kvw5 = kvw.reshape(b, t, 2, N_KV, HEAD_DIM)
    nbuf = win_cache.shape[1]
    win_state = jnp.concatenate([win_cache, kvw5], axis=1)[:, -nbuf:]
    st = (kvc.reshape(b, t, 2, N_KV, HEAD_DIM), kvs.reshape(b, t, 2, N_KV, HEAD_DIM), win_state,
          htail[:, (t - 1) % 8], jnp.concatenate([rg_conv0, xr.reshape(b, t, c)], axis=1)[:, -3:])
    return hs.reshape(b, t, d), st


def kernel(x_prompt, x_sample, cache_cmp_kv, cache_slc_kv, cache_win_kv, state_rg_h, state_rg_conv, state_ffn_conv, page_table, norm_mix_pre, norm_mix_post, norm_ffn_pre, norm_ffn_post, w_in, rg_conv_w, rg_conv_b, rg_wa, rg_ba, rg_wx, rg_bx, rg_lambda, cmpk_pe, cmpk_w1, cmpk_b1, cmpk_w2, cmpv_pe, cmpv_w1, cmpv_b1, cmpv_w2, w_out, ffn_w_in, ffn_conv_w, ffn_conv_b, ffn_w_down):
    params = dict(norm_mix_pre=norm_mix_pre, norm_mix_post=norm_mix_post, norm_ffn_pre=norm_ffn_pre,
                  norm_ffn_post=norm_ffn_post, w_in=w_in, rg_conv_w=rg_conv_w, rg_conv_b=rg_conv_b, rg_wa=rg_wa,
                  rg_ba=rg_ba, rg_wx=rg_wx, rg_bx=rg_bx, rg_lambda=rg_lambda, cmpk_pe=cmpk_pe, cmpk_w1=cmpk_w1,
                  cmpk_b1=cmpk_b1, cmpk_w2=cmpk_w2, cmpv_pe=cmpv_pe, cmpv_w1=cmpv_w1, cmpv_b1=cmpv_b1,
                  cmpv_w2=cmpv_w2, w_out=w_out, ffn_w_in=ffn_w_in, ffn_conv_w=ffn_conv_w, ffn_conv_b=ffn_conv_b,
                  ffn_w_down=ffn_w_down)
    depth = w_in.shape[0]
    bp, tp, d = x_prompt.shape
    bs, ts, _ = x_sample.shape
    past_len = page_table.shape[1] * cache_cmp_kv.shape[2]
    dff = ffn_w_in.shape[2] // 2
    p_cos, p_sin = _rope_tables(jnp.arange(tp))
    s_cos, s_sin = _rope_tables(past_len + jnp.arange(ts))
    s_cos, s_sin = jnp.tile(s_cos, (bs, 1)), jnp.tile(s_sin, (bs, 1))
    hp, hs = x_prompt, x_sample
    states_p, states_s = [], []
    for l in range(depth):
        lw = _layer_weights(l, params)
        g_fpre, g_fpost = lw['norms'][2], lw['norms'][3]
        hp, st = _prompt_mixer(hp, lw, p_cos, p_sin)
        tmf = _div_tile(tp, 512)
        hp, tail = _ffn(hp, jnp.zeros((bp, 8, dff), F32), g_fpre, g_fpost, *lw['ffn'], tmf, _div_tile(dff, 1024), 1)
        states_p.append(st + (tail[:, -1, 6:],))
        hs, st = _sample_mixer(hs, lw, s_cos, s_sin, cache_cmp_kv[l], cache_slc_kv[l], page_table,
                               cache_win_kv[l], state_rg_h[l], state_rg_conv[l])
        x_tm = hs.transpose(1, 0, 2).reshape(1, ts * bs, d)
        pre_tm = state_ffn_conv[l].transpose(1, 0, 2).reshape(1, 2 * bs, dff)
        out_tm, tail = _ffn(x_tm, pre_tm, g_fpre, g_fpost, *lw['ffn'], ts * bs, _div_tile(dff, 1024), bs)
        hs = out_tm.reshape(ts, bs, d).transpose(1, 0, 2)
        states_s.append(st + (tail.reshape(2, bs, dff).transpose(1, 0, 2),))
    p_cmp, p_slc, p_win, p_rgh, p_rgc, p_ffc = [jnp.stack(z) for z in zip(*states_p)]
    s_cmp, s_slc, s_win, s_rgh, s_rgc, s_ffc = [jnp.stack(z) for z in zip(*states_s)]
    return (hp, hs, p_cmp, s_cmp, p_slc, s_slc, p_win, s_win, p_rgh, s_rgh, p_rgc, s_rgc, p_ffc, s_ffc)
```

```python
import functools

import jax
import jax.numpy as jnp
import numpy as np
from jax import lax
from jax.experimental import pallas as pl
from jax.experimental.pallas import tpu as pltpu

F32 = jnp.float32
BF16 = jnp.bfloat16

HEAD_DIM = 64
N_KV = 2
GROUP = 4
N_HEADS = N_KV * GROUP
KV_DIM = N_KV * HEAD_DIM
CMP_STRIDE = 16
SLC_BLOCK = 64
N_SELECT = 16
N_LOCAL = 2
WINDOW = 512
Q_BLOCK = 128
RG_C = 8.0
ROPE_THETA = 10000.0
EPS = 1e-6

V7X_VMEM_BYTES = 64 * 1024 * 1024
VMEM_LIMIT = 48 * 1024 * 1024
SUBLANES = 8
LANES = 128

NEG = -2.0 ** 100
BIG = 1e30


def _cparams(sem):
    return pltpu.CompilerParams(dimension_semantics=sem, vmem_limit_bytes=VMEM_LIMIT)


def _div_tile(n, pref):
    t = min(n, pref)
    while n % t:
        t -= 1
    return t


def _rms(x, g):
    y = x * lax.rsqrt(jnp.mean(x * x, axis=-1, keepdims=True) + EPS)
    return y * g


def _expm1(x):
    u = jnp.exp(x)
    near = (jnp.abs(x) < 0.5) & (u != 1.0)
    kahan = (u - 1.0) * x / jnp.where(near, jnp.log(u), 1.0)
    return jnp.where(u == 1.0, x, jnp.where(near, kahan, u - 1.0))


def _nt(a, b, precision=None):
    return lax.dot_general(a, b, (((1,), (1,)), ((), ())), preferred_element_type=F32, precision=precision)


def _mm(a, b):
    return jnp.dot(a, b, preferred_element_type=F32)


C_XR, C_GATE, C_Q, C_QROT, C_KVC, C_KVS, C_KSROT, C_KVW, C_KWROT, C_GL, C_END = (
    0, 512, 1024, 1536, 2048, 2304, 2560, 2688, 2944, 3072, 3200)


def _proj_kernel(x_ref, g_ref, w_ref, cos_ref, sin_ref,
                 xr_ref, gate_ref, qc_ref, qr_ref, kvc_ref, kvs_ref, kvw_ref, gl_ref):
    xn = _rms(x_ref[...], g_ref[...]).astype(BF16)

    def mm(c0, c1):
        return _mm(xn, w_ref[:, c0:c1])

    cos = cos_ref[...]
    sin = sin_ref[...]
    xr_ref[...] = mm(C_XR, C_GATE)
    gate_ref[...] = mm(C_GATE, C_Q)
    q = mm(C_Q, C_QROT)
    qrot = mm(C_QROT, C_KVC)
    cos4 = jnp.concatenate([cos] * 4, axis=1)
    sin4 = jnp.concatenate([sin] * 4, axis=1)
    scale = HEAD_DIM ** -0.5
    qc_ref[...] = (q * scale).astype(BF16)
    qr_ref[...] = ((q * cos4 + qrot * sin4) * scale).astype(BF16)
    kvc_ref[...] = mm(C_KVC, C_KVS)
    kvs = mm(C_KVS, C_KSROT)
    ksrot = mm(C_KSROT, C_KVW)
    kvs_ref[:, 0:KV_DIM] = kvs[:, 0:KV_DIM] * cos + ksrot * sin
    kvs_ref[:, KV_DIM:] = kvs[:, KV_DIM:]
    kvw = mm(C_KVW, C_KWROT)
    kwrot = mm(C_KWROT, C_GL)
    kvw_ref[:, 0:KV_DIM] = kvw[:, 0:KV_DIM] * cos + kwrot * sin
    kvw_ref[:, KV_DIM:] = kvw[:, KV_DIM:]
    gl_ref[...] = jax.nn.sigmoid(mm(C_GL, C_END))


R_KVC, R_KVS, R_KSROT, R_KVW, R_KWROT, R_END = 0, 256, 512, 640, 896, 1024


def _proj_t_kernel(x_ref, g_ref, w_ref, wt_ref, cos_ref, sin_ref, cost_ref, sint_ref,
                   xr_ref, gate_ref, qc_ref, qr_ref, gl_ref, kvct_ref, kvst_ref, kvstb_ref, kvwt_ref, kvwtb_ref):
    xn = _rms(x_ref[...], g_ref[...]).astype(BF16)

    def mm(c0, c1):
        return _mm(xn, w_ref[:, c0:c1])

    def mt(r0, r1):
        return _nt(wt_ref[r0:r1, :], xn)

    xr_ref[...] = mm(0, 512)
    gate_ref[...] = mm(512, 1024)
    q = mm(1024, 1536)
    qrot = mm(1536, 2048)
    cos4 = jnp.concatenate([cos_ref[...]] * 4, axis=1)
    sin4 = jnp.concatenate([sin_ref[...]] * 4, axis=1)
    scale = HEAD_DIM ** -0.5
    qc_ref[...] = (q * scale).astype(BF16)
    qr_ref[...] = ((q * cos4 + qrot * sin4) * scale).astype(BF16)
    gl_ref[...] = jax.nn.sigmoid(mm(2048, 2176))
    kvct_ref[0] = mt(R_KVC, R_KVS)
    cost, sint = cost_ref[...], sint_ref[...]
    for lo, rot, end, f_ref, b_ref in ((R_KVS, R_KSROT, R_KVW, kvst_ref, kvstb_ref),
                                       (R_KVW, R_KWROT, R_END, kvwt_ref, kvwtb_ref)):
        kv = mt(lo, rot)
        k = kv[0:KV_DIM] * cost + mt(rot, end) * sint
        f_ref[0, 0:KV_DIM, :] = k
        f_ref[0, KV_DIM:, :] = kv[KV_DIM:]
        b_ref[0, 0:KV_DIM, :] = k.astype(BF16)
        b_ref[0, KV_DIM:, :] = kv[KV_DIM:].astype(BF16)


def _proj_t(x2d, g, wrow, wt, cos, sin, cost, sint, b, tm):
    m, d = x2d.shape
    t = m // b
    nper = t // tm
    row = lambda w: pl.BlockSpec((tm, w), lambda i: (i, 0))
    tab = pl.BlockSpec((tm, LANES), lambda i: (i % nper, 0))
    tabt = pl.BlockSpec((KV_DIM, tm), lambda i: (0, i % nper))
    tr = pl.BlockSpec((1, 2 * KV_DIM, tm), lambda i: (i // nper, 0, i % nper))
    full = lambda a: pl.BlockSpec(a.shape, lambda i: (0,) * a.ndim)
    widths = (512, 512, 512, 512, 128)
    dtypes = (F32, F32, BF16, BF16, F32)
    tdt = (F32, F32, BF16, F32, BF16)
    return pl.pallas_call(
        _proj_t_kernel,
        grid=(m // tm,),
        in_specs=[row(d), full(g), full(wrow), full(wt), tab, tab, tabt, tabt],
        out_specs=[row(w) for w in widths] + [tr] * 5,
        out_shape=[jax.ShapeDtypeStruct((m, w), dt) for w, dt in zip(widths, dtypes)]
        + [jax.ShapeDtypeStruct((b, 2 * KV_DIM, t), dt) for dt in tdt],
        compiler_params=_cparams(("arbitrary",)),
        name="proj_t",
    )(x2d, g, wrow, wt, cos, sin, cost, sint)


def _proj(x2d, g, wcat, cos, sin, tm):
    m, d = x2d.shape
    nper = cos.shape[0] // tm
    row = lambda w: pl.BlockSpec((tm, w), lambda i: (i, 0))
    tab = pl.BlockSpec((tm, LANES), lambda i: (i % nper, 0))
    widths = (512, 512, 512, 512, 256, 256, 256, 128)
    dtypes = (F32, F32, BF16, BF16, F32, F32, F32, F32)
    return pl.pallas_call(
        _proj_kernel,
        grid=(m // tm,),
        in_specs=[row(d), pl.BlockSpec((1, d), lambda i: (0, 0)),
                  pl.BlockSpec(wcat.shape, lambda i: (0, 0)), tab, tab],
        out_specs=[row(w) for w in widths],
        out_shape=[jax.ShapeDtypeStruct((m, w), dt) for w, dt in zip(widths, dtypes)],
        compiler_params=_cparams(("arbitrary",)),
        name="proj",
    )(x2d, g, wcat, cos, sin)


def _rglru_kernel(xr_ref, gate_ref, pre_ref, h0_ref, cw_ref, cb_ref, wa_ref, wx_ref, ba_ref, bx_ref, lam_ref,
                  y_ref, htail_ref, xpad, hc, a_s, b_s):
    tt = xr_ref.shape[1]
    j = pl.program_id(1)

    @pl.when(j == 0)
    def _():
        xpad[0:8, :] = pre_ref[0]
        hc[...] = jnp.broadcast_to(h0_ref[0], hc.shape)

    @pl.when(j > 0)
    def _():
        xpad[0:8, :] = xpad[tt:tt + 8, :]

    xpad[8:8 + tt, :] = xr_ref[0]
    xc = cb_ref[...]
    for k in range(4):
        xc = xc + xpad[pl.ds(5 + k, tt), :] * cw_ref[k:k + 1, :]
    xb = xc.astype(BF16)
    r = jax.nn.sigmoid(_mm(xb, wa_ref[...]) + ba_ref[...])
    i = jax.nn.sigmoid(_mm(xb, wx_ref[...]) + bx_ref[...])
    log_a = RG_C * r * jax.nn.log_sigmoid(lam_ref[...])
    a_s[...] = jnp.exp(log_a)
    b_s[...] = jnp.sqrt(-_expm1(2.0 * log_a)) * (i * xc)

    row = lax.broadcasted_iota(jnp.int32, (8, a_s.shape[1]), 0)

    def body(gi, h):
        r0 = pl.multiple_of(gi * 8, 8)
        av = a_s[pl.ds(r0, 8), :]
        bv = b_s[pl.ds(r0, 8), :]
        for s in (1, 2, 4):
            keep = row >= s
            a_sh = pltpu.roll(av, s, 0)
            b_sh = pltpu.roll(bv, s, 0)
            bv = jnp.where(keep, av * b_sh + bv, bv)
            av = jnp.where(keep, av * a_sh, av)
        hs = av * h + bv
        b_s[pl.ds(r0, 8), :] = hs
        return jnp.broadcast_to(hs[7:8, :], hs.shape)

    h = lax.fori_loop(0, tt // 8, body, hc[...])
    hc[...] = h
    htail_ref[0] = b_s[tt - 8:tt, :]
    y_ref[0] = (b_s[...] * jax.nn.gelu(gate_ref[0])).astype(y_ref.dtype)


def _rglru(xr, gate, pre8, h0, cw, cb, wa, wx, ba, bx, lam, tt):
    b, t, c = xr.shape
    full = lambda a: pl.BlockSpec(a.shape, lambda i, j: (0,) * a.ndim)
    return pl.pallas_call(
        _rglru_kernel,
        grid=(b, t // tt),
        in_specs=[pl.BlockSpec((1, tt, c), lambda i, j: (i, j, 0)),
                  pl.BlockSpec((1, tt, c), lambda i, j: (i, j, 0)),
                  pl.BlockSpec((1, 8, c), lambda i, j: (i, 0, 0)),
                  pl.BlockSpec((1, 1, c), lambda i, j: (i, 0, 0)),
                  full(cw), full(cb), full(wa), full(wx), full(ba), full(bx), full(lam)],
        out_specs=[pl.BlockSpec((1, tt, c), lambda i, j: (i, j, 0)),
                   pl.BlockSpec((1, 8, c), lambda i, j: (i, 0, 0))],
        out_shape=[jax.ShapeDtypeStruct((b, t, c), BF16), jax.ShapeDtypeStruct((b, 8, c), F32)],
        scratch_shapes=[pltpu.VMEM((tt + 8, c), F32), pltpu.VMEM((8, c), F32),
                        pltpu.VMEM((tt, c), F32), pltpu.VMEM((tt, c), F32)],
        compiler_params=_cparams(("arbitrary", "arbitrary")),
        name="rglru",
    )(xr, gate, pre8, h0, cw, cb, wa, wx, ba, bx, lam)


def _page_copy(pt_ref, src_ref, buf, sem, idx, slot, p, seq_pages):
    page = pt_ref[idx]
    if seq_pages is None:
        src = src_ref.at[page]
    else:
        src = src_ref.at[page // seq_pages, :, pl.ds(pl.multiple_of((page % seq_pages) * 128, 128), 128)]
    return pltpu.make_async_copy(src, buf.at[slot, p], sem.at[slot])


def _cmp_kernel(pt_ref, pool_ref, pea_ref, peb_ref, w1a_ref, w1b_ref, b1_ref, w2_ref, out_ref,
                buf, sem, xbuf, ua_s, *, pp, seq_pages):
    nb, nt = pl.num_programs(0), pl.num_programs(1)
    j = pl.program_id(1)
    step = pl.program_id(0) * nt + j
    slot = step % 2
    m = pp * 8

    def fetch(st, sl):
        for p in range(pp):
            _page_copy(pt_ref, pool_ref, buf, sem, st * pp + p, sl, p, seq_pages).start()

    @pl.when(step == 0)
    def _():
        fetch(step, slot)

    @pl.when(step + 1 < nb * nt)
    def _():
        fetch(step + 1, 1 - slot)

    for p in range(pp):
        _page_copy(pt_ref, pool_ref, buf, sem, step * pp + p, slot, p, seq_pages).wait()

    def to_rows(p, carry):
        r0 = pl.multiple_of(p * 128, 128)
        for kv in range(2):
            xbuf[kv, pl.ds(r0, 128), :] = buf[slot, p, kv * KV_DIM:(kv + 1) * KV_DIM, :].T
        return carry

    lax.fori_loop(0, pp, to_rows, 0)

    acc_a = [jnp.zeros((m, 256), F32)] * 2
    acc_b = [jnp.zeros((m, 256), F32)] * 2
    for s in range(CMP_STRIDE):
        for kv in range(2):
            xs = xbuf[kv, pl.ds(s, m, stride=CMP_STRIDE), :]
            lanes = slice(kv * KV_DIM, (kv + 1) * KV_DIM)
            acc_a[kv] = acc_a[kv] + _mm((xs + pea_ref[s:s + 1, lanes]).astype(BF16), w1a_ref[s, kv])
            acc_b[kv] = acc_b[kv] + _mm((xs + peb_ref[s:s + 1, lanes]).astype(BF16), w1b_ref[s, kv])
    acc_a = jnp.concatenate(acc_a, axis=1)
    acc_b = jnp.concatenate(acc_b, axis=1)

    @pl.when(j == 0)
    def _():
        ua_s[0:8, :] = jnp.zeros((8, 512), F32)

    @pl.when(j > 0)
    def _():
        ua_s[0:8, :] = ua_s[m:m + 8, :]

    ua_s[8:8 + m, :] = acc_a
    hid = jax.nn.gelu(ua_s[pl.ds(7, m), :] + acc_b + b1_ref[...])
    out_ref[0] = _mm(hid.astype(BF16), w2_ref[...]).astype(out_ref.dtype)


def _compress(pool, page_table, cw, pp, paged):
    b, npg = page_table.shape
    nt = npg // pp
    m = pp * 8
    full = lambda a: pl.BlockSpec(a.shape, lambda i, j, pt: (0,) * a.ndim)
    pea, peb, w1a, w1b, b1, w2 = cw
    gs = pltpu.PrefetchScalarGridSpec(
        num_scalar_prefetch=1,
        grid=(b, nt),
        in_specs=[pl.BlockSpec(memory_space=pl.ANY), full(pea), full(peb), full(w1a), full(w1b), full(b1), full(w2)],
        out_specs=pl.BlockSpec((1, m, 256), lambda i, j, pt: (i, j, 0)),
        scratch_shapes=[pltpu.VMEM((2, pp, 2 * KV_DIM, 128), F32), pltpu.SemaphoreType.DMA((2,)),
                        pltpu.VMEM((2, pp * 128, KV_DIM), F32), pltpu.VMEM((m + 8, 512), F32)],
    )
    return pl.pallas_call(
        functools.partial(_cmp_kernel, pp=pp, seq_pages=None if paged else npg),
        grid_spec=gs,
        out_shape=jax.ShapeDtypeStruct((b, nt * m, 256), BF16),
        compiler_params=_cparams(("arbitrary", "arbitrary")),
        name="compress",
    )(page_table.reshape(-1), pool, pea, peb, w1a, w1b, b1, w2)


def _softmax_rows(s, mask):
    sm = jnp.where(mask, s, -BIG)
    mx = jnp.max(sm, axis=-1, keepdims=True)
    e = jnp.where(mask, jnp.exp(sm - mx), 0.0)
    return e / jnp.maximum(jnp.sum(e, axis=-1, keepdims=True), 1e-30)


def _select_bias_t(score_t, lag, jidx, n_blocks, k_sel):
    forced = (jidx == 0) | ((lag >= 0) & (lag < N_LOCAL))
    sc = jnp.where(forced, BIG, score_t)
    sc = jnp.where(lag >= 0, sc, -BIG)
    rank = jnp.zeros(sc.shape, F32)
    for i in range(n_blocks):
        si = sc[i:i + 1, :]
        rank = rank + jnp.where(jidx > i, jnp.where(si >= sc, 1.0, 0.0), jnp.where(si > sc, 1.0, 0.0))
    return jnp.where(rank < k_sel, 0.0, NEG)


def _nsa_prompt_kernel(qc_ref, qr_ref, g_ref, kck_ref, kcv_ref, kvs_ref, kvw_ref, oh_ref, pool_ref,
                       o_ref, *, t_len, kc_len, wl):
    qb = pl.program_id(1)
    start = qb * Q_BLOCK
    nc = kck_ref.shape[2]
    nj = pool_ref.shape[0]
    rows = GROUP * Q_BLOCK
    qpos = start + (lax.broadcasted_iota(jnp.int32, (rows, 1), 0) & (Q_BLOCK - 1))

    for hkv in range(N_KV):
        q4c = qc_ref[0, hkv, 0]
        s = _nt(q4c, kck_ref[0, hkv])
        mrow = lax.broadcasted_iota(jnp.int32, (rows, nc), 1)
        valid = (mrow >= 1) & (mrow * CMP_STRIDE + (CMP_STRIDE - 1) <= qpos)
        p = _softmax_rows(s, valid)
        o_cmp = _mm(p.astype(BF16), kcv_ref[0, hkv])
        psum = p[0:Q_BLOCK] + p[Q_BLOCK:2 * Q_BLOCK] + p[2 * Q_BLOCK:3 * Q_BLOCK] + p[3 * Q_BLOCK:]
        score_t = _nt(pool_ref[...], psum, precision=lax.Precision.HIGHEST)
        jidx = lax.broadcasted_iota(jnp.int32, (nj, Q_BLOCK), 0)
        qp_l = start + lax.broadcasted_iota(jnp.int32, (nj, Q_BLOCK), 1)
        lag = (qp_l >> 6) - jidx
        bias_t = _select_bias_t(score_t, lag, jidx, nj, min(N_SELECT, nj))
        pad_t = jnp.concatenate([jnp.zeros((HEAD_DIM, Q_BLOCK), F32), bias_t]
                                + ([jnp.zeros((HEAD_DIM - nj, Q_BLOCK), F32)] if nj < HEAD_DIM else []), axis=0)
        bias = pad_t.T
        lane = lax.broadcasted_iota(jnp.int32, (Q_BLOCK, LANES), 1)
        qaug = []
        for g in range(GROUP):
            qg = qr_ref[0, hkv, 0, g * Q_BLOCK:(g + 1) * Q_BLOCK, :].astype(F32)
            qaug.append(jnp.where(lane < HEAD_DIM, qg, bias).astype(BF16))
        qaug = jnp.concatenate(qaug, axis=0)

        n_chunks = (start + Q_BLOCK + kc_len - 1) // kc_len
        k_rows = slice(hkv * HEAD_DIM, (hkv + 1) * HEAD_DIM)
        v_rows = slice(KV_DIM + hkv * HEAD_DIM, KV_DIM + (hkv + 1) * HEAD_DIM)

        def chunk(c, carry, diagonal):
            m_i, l_i, acc = carry
            k0 = pl.multiple_of(c * kc_len, kc_len)
            kaug = jnp.concatenate([kvs_ref[0, k_rows, pl.ds(k0, kc_len)], oh_ref[:, pl.ds(k0, kc_len)]], axis=0)
            sc = _mm(qaug, kaug)
            if diagonal:
                kpos = k0 + lax.broadcasted_iota(jnp.int32, (rows, kc_len), 1)
                sc = jnp.where(kpos <= qpos, sc, NEG)
            m_n = jnp.maximum(m_i, jnp.max(sc, axis=-1, keepdims=True))
            alpha = jnp.exp(m_i - m_n)
            pe = jnp.exp(sc - m_n)
            l_n = alpha * l_i + jnp.sum(pe, axis=-1, keepdims=True)
            acc = alpha * acc + _nt(pe.astype(BF16), kvs_ref[0, v_rows, pl.ds(k0, kc_len)])
            return m_n, l_n, acc

        m0 = jnp.full((rows, 1), -3e38, F32)
        l0 = jnp.zeros((rows, 1), F32)
        a0 = jnp.zeros((rows, HEAD_DIM), F32)
        carry = lax.fori_loop(0, n_chunks - 1, functools.partial(chunk, diagonal=False), (m0, l0, a0))
        _, l_f, acc_f = chunk(n_chunks - 1, carry, True)
        o_slc = acc_f / jnp.maximum(l_f, 1e-30)

        base = pl.multiple_of(jnp.maximum(start + Q_BLOCK - wl, 0), Q_BLOCK)
        q4r = qr_ref[0, hkv, 0][:, 0:HEAD_DIM]
        sw = _mm(q4r, kvw_ref[0, k_rows, pl.ds(base, wl)])
        dist = qpos - (base + lax.broadcasted_iota(jnp.int32, (rows, wl), 1))
        pw = _softmax_rows(sw, (dist >= 0) & (dist < WINDOW))
        o_win = _nt(pw.astype(BF16), kvw_ref[0, v_rows, pl.ds(base, wl)])

        gt = g_ref[0, hkv, 0]
        o = gt[:, 0:1] * o_cmp + gt[:, 1:2] * o_slc + gt[:, 2:3] * o_win
        o_ref[0, hkv, 0] = o.astype(o_ref.dtype)


def _nsa_prompt(qc, qr, gt, kck, kcv, kvst, kvwt, onehot_t, pool_t, t_len):
    b = qc.shape[0]
    nqb = t_len // Q_BLOCK
    rows = GROUP * Q_BLOCK
    kc_len = _div_tile(t_len, 512)
    wl = min(WINDOW + Q_BLOCK, t_len)
    qspec = lambda w: pl.BlockSpec((1, N_KV, 1, rows, w), lambda i, j: (i, 0, j, 0, 0))
    seq = lambda a: pl.BlockSpec((1,) + a.shape[1:], lambda i, j: (i,) + (0,) * (a.ndim - 1))
    return pl.pallas_call(
        functools.partial(_nsa_prompt_kernel, t_len=t_len, kc_len=kc_len, wl=wl),
        grid=(b, nqb),
        in_specs=[qspec(HEAD_DIM), qspec(LANES), qspec(8), seq(kck), seq(kcv), seq(kvst), seq(kvwt),
                  pl.BlockSpec(onehot_t.shape, lambda i, j: (0, 0)), pl.BlockSpec(pool_t.shape, lambda i, j: (0, 0))],
        out_specs=qspec(HEAD_DIM),
        out_shape=jax.ShapeDtypeStruct((b, N_KV, nqb, rows, HEAD_DIM), BF16),
        compiler_params=_cparams(("arbitrary", "arbitrary")),
        name="nsa_prompt",
    )(qc, qr, gt, kck, kcv, kvst, kvwt, onehot_t, pool_t)


def _nsa_sample_kernel(pt_ref, qc_ref, qr_ref, g_ref, kc_ref, pool_ref, exp_ref, pmap_ref, new_ref, win_ref,
                       o_ref, buf, sem, bias_s, sc_s, m_s, l_s, acc_s, ocmp_s, *, pp, past_len, n_tok, nj):
    nb, nt = pl.num_programs(0), pl.num_programs(1)
    j = pl.program_id(1)
    step = pl.program_id(0) * nt + j
    slot = step % 2
    rows = GROUP * N_KV * n_tok
    tk = pp * 128
    bpt = tk // SLC_BLOCK

    def fetch(st, sl):
        for p in range(pp):
            _page_copy(pt_ref, pool_ref, buf, sem, st * pp + p, sl, p, None).start()

    @pl.when(step == 0)
    def _():
        fetch(step, slot)

    @pl.when(step + 1 < nb * nt)
    def _():
        fetch(step + 1, 1 - slot)

    rid = lax.broadcasted_iota(jnp.int32, (rows, 1), 0)
    tok = rid % n_tok
    qpos = past_len + tok

    @pl.when(j == 0)
    def _():
        kc = kc_ref[0]
        nc = kc.shape[0]
        s = _nt(qc_ref[0], kc[:, 0:KV_DIM])
        mrow = lax.broadcasted_iota(jnp.int32, (rows, nc), 1)
        valid = (mrow >= 1) & (mrow * CMP_STRIDE + (CMP_STRIDE - 1) <= qpos)
        p = _softmax_rows(s, valid)
        ocmp_s[...] = _mm(p.astype(BF16), kc[:, KV_DIM:])
        r8 = N_KV * n_tok
        psum = p[0:r8] + p[r8:2 * r8] + p[2 * r8:3 * r8] + p[3 * r8:]
        score_t = _nt(pmap_ref[...], psum, precision=lax.Precision.HIGHEST)
        njp = score_t.shape[0]
        jidx = lax.broadcasted_iota(jnp.int32, (njp, r8), 0)
        qp_l = past_len + lax.broadcasted_iota(jnp.int32, (njp, r8), 1) % n_tok
        lag = (qp_l >> 6) - jidx
        forced = (jidx == 0) | ((lag >= 0) & (lag < N_LOCAL))
        sc = jnp.where(forced, BIG, score_t)
        sc = jnp.where((lag >= 0) & (jidx < nj), sc, -BIG)
        sc_s[...] = jnp.zeros(sc_s.shape, F32)
        sc_s[:, 0:r8] = sc
        scv = sc_s[...]
        jfull = lax.broadcasted_iota(jnp.int32, scv.shape, 0)

        def rank_body(i, rank):
            si = sc_s[pl.ds(i, 1), :]
            return rank + jnp.where(jfull > i, jnp.where(si >= scv, 1.0, 0.0), jnp.where(si > scv, 1.0, 0.0))

        rank = lax.fori_loop(0, nj, rank_body, jnp.zeros(scv.shape, F32))
        bias_t = jnp.where(rank < min(N_SELECT, nj), 0.0, NEG)
        for tj in range(njp // bpt):
            blk = bias_t[tj * bpt:(tj + 1) * bpt, :]
            sq = jnp.concatenate([blk, jnp.zeros((LANES - bpt, LANES), F32)], axis=0).T
            b8 = sq[0:r8, :]
            bias_s[tj] = jnp.concatenate([b8] * GROUP, axis=0)
        m_s[...] = jnp.full(m_s.shape, -3e38, F32)
        l_s[...] = jnp.zeros(l_s.shape, F32)
        acc_s[...] = jnp.zeros(acc_s.shape, F32)

    for p in range(pp):
        _page_copy(pt_ref, pool_ref, buf, sem, step * pp + p, slot, p, None).wait()

    def online(sc, pv):
        m_i = m_s[...]
        m_n = jnp.maximum(m_i, jnp.max(sc, axis=-1, keepdims=True))
        alpha = jnp.exp(m_i - m_n)
        pe = jnp.exp(sc - m_n)
        l_s[...] = alpha * l_s[...] + jnp.sum(pe, axis=-1, keepdims=True)
        acc_s[...] = alpha * acc_s[...] + pv(pe.astype(BF16))
        m_s[...] = m_n

    qr = qr_ref[0]
    sc = jnp.concatenate([_mm(qr, buf[slot, p, 0:KV_DIM, :].astype(BF16)) for p in range(pp)], axis=1)
    sc = sc + _mm(bias_s[j].astype(BF16), exp_ref[...])

    def pv_pages(pe):
        o = _nt(pe[:, 0:128], buf[slot, 0, KV_DIM:, :].astype(BF16))
        for p in range(1, pp):
            o = o + _nt(pe[:, p * 128:(p + 1) * 128], buf[slot, p, KV_DIM:, :].astype(BF16))
        return o

    online(sc, pv_pages)

    @pl.when(j == nt - 1)
    def _():
        new = new_ref[0]
        nk = new.shape[0]
        kidx = lax.broadcasted_iota(jnp.int32, (rows, nk), 1)
        last_bias = bias_s[(past_len // SLC_BLOCK) // bpt][:, (past_len // SLC_BLOCK) % bpt:(past_len // SLC_BLOCK) % bpt + 1]
        scn = _nt(qr, new[:, 0:KV_DIM].astype(BF16)) + last_bias
        scn = jnp.where((kidx <= tok) & (kidx < n_tok), scn, NEG)
        online(scn, lambda pe: _mm(pe, new[:, KV_DIM:2 * KV_DIM].astype(BF16)))
        o_slc = acc_s[...] / jnp.maximum(l_s[...], 1e-30)

        nbuf = win_ref.shape[3]
        s1 = _mm(qr, win_ref[0, 0, 0:KV_DIM, :].astype(BF16))
        d1 = tok + nbuf - lax.broadcasted_iota(jnp.int32, (rows, nbuf), 1)
        ok1 = (d1 >= 0) & (d1 < WINDOW)
        s2 = _nt(qr, new[:, 2 * KV_DIM:3 * KV_DIM].astype(BF16))
        d2 = tok - kidx
        ok2 = (d2 >= 0) & (d2 < WINDOW) & (kidx < n_tok)
        s1 = jnp.where(ok1, s1, -BIG)
        s2 = jnp.where(ok2, s2, -BIG)
        mx = jnp.maximum(jnp.max(s1, axis=-1, keepdims=True), jnp.max(s2, axis=-1, keepdims=True))
        e1 = jnp.where(ok1, jnp.exp(s1 - mx), 0.0)
        e2 = jnp.where(ok2, jnp.exp(s2 - mx), 0.0)
        den = jnp.maximum(jnp.sum(e1, axis=-1, keepdims=True) + jnp.sum(e2, axis=-1, keepdims=True), 1e-30)
        o_win = (_nt(e1.astype(BF16), win_ref[0, 0, KV_DIM:, :].astype(BF16))
                 + _mm(e2.astype(BF16), new[:, 3 * KV_DIM:].astype(BF16))) / den
        gt = g_ref[0]
        o_ref[0] = gt[:, 0:1] * ocmp_s[...] + gt[:, 1:2] * o_slc + gt[:, 2:3] * o_win


def _nsa_sample(page_table, qc, qr, gt, kc, pool, expand, pmap, new, win, layer, pp, past_len, n_tok, nj):
    b, npg = page_table.shape
    nt = npg // pp
    rows = GROUP * N_KV * n_tok
    tk = pp * 128
    bpt = tk // SLC_BLOCK
    njp = pmap.shape[0]
    seq = lambda a: pl.BlockSpec((1,) + a.shape[1:], lambda i, j, pt: (i,) + (0,) * (a.ndim - 1))
    full = lambda a: pl.BlockSpec(a.shape, lambda i, j, pt: (0,) * a.ndim)
    gs = pltpu.PrefetchScalarGridSpec(
        num_scalar_prefetch=1,
        grid=(b, nt),
        in_specs=[seq(qc), seq(qr), seq(gt), seq(kc), pl.BlockSpec(memory_space=pl.ANY), full(expand), full(pmap),
                  seq(new), pl.BlockSpec((1, 1) + win.shape[2:], lambda i, j, pt: (layer, i, 0, 0))],
        out_specs=pl.BlockSpec((1, rows, LANES), lambda i, j, pt: (i, 0, 0)),
        scratch_shapes=[pltpu.VMEM((2, pp, 2 * KV_DIM, 128), F32), pltpu.SemaphoreType.DMA((2,)),
                        pltpu.VMEM((njp // bpt, rows, LANES), F32), pltpu.VMEM((njp, LANES), F32),
                        pltpu.VMEM((rows, 1), F32), pltpu.VMEM((rows, 1), F32),
                        pltpu.VMEM((rows, LANES), F32), pltpu.VMEM((rows, LANES), F32)],
    )
    return pl.pallas_call(
        functools.partial(_nsa_sample_kernel, pp=pp, past_len=past_len, n_tok=n_tok, nj=nj),
        grid_spec=gs,
        out_shape=jax.ShapeDtypeStruct((b, rows, LANES), F32),
        compiler_params=_cparams(("arbitrary", "arbitrary")),
        name="nsa_sample",
    )(page_table.reshape(-1), qc, qr, gt, kc, pool, expand, pmap, new, win)


def _merge_kernel(y_ref, o_ref, w_ref, g_ref, x_ref, out_ref):
    half = y_ref.shape[1]
    m = _mm(y_ref[...], w_ref[0:half, :]) + _mm(o_ref[...], w_ref[half:, :])
    out_ref[...] = x_ref[...] + _rms(m, g_ref[...])


def _merge(y, o, w, g, x, tm):
    m, d = x.shape
    half = y.shape[1]
    return pl.pallas_call(
        _merge_kernel,
        grid=(m // tm,),
        in_specs=[pl.BlockSpec((tm, half), lambda i: (i, 0)), pl.BlockSpec((tm, half), lambda i: (i, 0)),
                  pl.BlockSpec(w.shape, lambda i: (0, 0)), pl.BlockSpec((1, d), lambda i: (0, 0)),
                  pl.BlockSpec((tm, d), lambda i: (i, 0))],
        out_specs=pl.BlockSpec((tm, d), lambda i: (i, 0)),
        out_shape=jax.ShapeDtypeStruct((m, d), F32),
        compiler_params=_cparams(("arbitrary",)),
        name="merge",
    )(y, o, w, g, x)


def _ffn_kernel(x_ref, pre_ref, gpre_ref, gpost_ref, wu_ref, wg_ref, cw_ref, cb_ref, wd_ref,
                out_ref, tail_ref, xn_s, gpad, carry, acc, *, shift, padr):
    j, f = pl.program_id(1), pl.program_id(2)
    nf = pl.num_programs(2)
    tm = x_ref.shape[1]

    @pl.when(f == 0)
    def _():
        xn_s[...] = _rms(x_ref[0], gpre_ref[...]).astype(BF16)
        acc[...] = jnp.zeros(acc.shape, F32)

    @pl.when(j == 0)
    def _():
        gpad[0:padr, :] = pre_ref[0]

    @pl.when(j > 0)
    def _():
        gpad[0:padr, :] = carry[f]

    xn = xn_s[...]
    u = _mm(xn, wu_ref[...])
    gpad[padr:padr + tm, :] = _mm(xn, wg_ref[...])
    gc = cb_ref[...]
    for k in range(3):
        gc = gc + gpad[pl.ds(padr - (2 - k) * shift, tm), :] * cw_ref[k:k + 1, :]
    tail = gpad[tm:tm + padr, :]
    carry[f] = tail
    tail_ref[0, 0] = tail
    acc[...] += _mm((jax.nn.gelu(gc) * u).astype(BF16), wd_ref[...])

    @pl.when(f == nf - 1)
    def _():
        out_ref[0] = x_ref[0] + _rms(acc[...], gpost_ref[...])


def _ffn(x, pre, gpre, gpost, wu, wg, cw, cb, wd, tm, tf, shift):
    b, t, d = x.shape
    dff = wu.shape[1]
    padr = pre.shape[1]
    nt, nf = t // tm, dff // tf
    return pl.pallas_call(
        functools.partial(_ffn_kernel, shift=shift, padr=padr),
        grid=(b, nt, nf),
        in_specs=[pl.BlockSpec((1, tm, d), lambda i, j, f: (i, j, 0)),
                  pl.BlockSpec((1, padr, tf), lambda i, j, f: (i, 0, f)),
                  pl.BlockSpec((1, d), lambda i, j, f: (0, 0)),
                  pl.BlockSpec((1, d), lambda i, j, f: (0, 0)),
                  pl.BlockSpec((d, tf), lambda i, j, f: (0, f)),
                  pl.BlockSpec((d, tf), lambda i, j, f: (0, f)),
                  pl.BlockSpec((3, tf), lambda i, j, f: (0, f)),
                  pl.BlockSpec((1, tf), lambda i, j, f: (0, f)),
                  pl.BlockSpec((tf, d), lambda i, j, f: (f, 0))],
        out_specs=[pl.BlockSpec((1, tm, d), lambda i, j, f: (i, j, 0)),
                   pl.BlockSpec((1, 1, padr, tf), lambda i, j, f: (i, j, 0, f))],
        out_shape=[jax.ShapeDtypeStruct((b, t, d), F32), jax.ShapeDtypeStruct((b, nt, padr, dff), F32)],
        scratch_shapes=[pltpu.VMEM((tm, d), BF16), pltpu.VMEM((padr + tm, tf), F32),
                        pltpu.VMEM((nf, padr, tf), F32), pltpu.VMEM((tm, d), F32)],
        compiler_params=_cparams(("arbitrary", "arbitrary", "arbitrary")),
        name="ffn",
    )(x, pre, gpre, gpost, wu, wg, cw, cb, wd)


def _rot_cols(w):
    d, n = w.shape
    w4 = w.reshape(d, n // HEAD_DIM, 2, HEAD_DIM // 2)
    return jnp.concatenate([-w4[:, :, 1], w4[:, :, 0]], axis=2).reshape(d, n)


def _rope_tables(pos):
    half = HEAD_DIM // 2
    freq = ROPE_THETA ** (-jnp.arange(half, dtype=F32) / half)
    ang = pos.astype(F32)[:, None] * freq[None, :]
    cos, sin = jnp.cos(ang), jnp.sin(ang)
    return jnp.concatenate([cos] * 4, axis=1), jnp.concatenate([sin] * 4, axis=1)


def _block_diag(w):
    n, c, d = w.shape
    return jnp.einsum('ncd,nm->ncmd', w, jnp.eye(n, dtype=w.dtype)).reshape(n * c, n * d)


def _layer_weights(l, p):
    d_model = p['w_in'].shape[1]
    w = p['w_in'][l]
    sizes = [512, 512, 512, 256, 256, 256, 3 * N_HEADS]
    cuts = np.cumsum([0] + sizes)
    xr, gate, q, kvc, kvs, kvw, gl = [w[:, cuts[i]:cuts[i + 1]] for i in range(7)]
    gl_pad = jnp.pad(gl, ((0, 0), (0, LANES - gl.shape[1])))
    wcat = jnp.concatenate([xr, gate, q, _rot_cols(q), kvc, kvs, _rot_cols(kvs[:, :KV_DIM]),
                            kvw, _rot_cols(kvw[:, :KV_DIM]), gl_pad], axis=1).astype(BF16)
    wrow = jnp.concatenate([xr, gate, q, _rot_cols(q), gl_pad], axis=1).astype(BF16)
    wt = jnp.concatenate([kvc, kvs, _rot_cols(kvs[:, :KV_DIM]), kvw, _rot_cols(kvw[:, :KV_DIM])], axis=1).T.astype(BF16)
    row = lambda v: v.reshape(1, -1)
    rg = (p['rg_conv_w'][l], row(p['rg_conv_b'][l]), _block_diag(p['rg_wa'][l]).astype(BF16),
          _block_diag(p['rg_wx'][l]).astype(BF16), row(p['rg_ba'][l]), row(p['rg_bx'][l]), row(p['rg_lambda'][l]))
    sel = lambda a, b: jnp.stack([a, a, b, b])
    eye2 = jnp.eye(N_KV, dtype=F32)
    w1 = jnp.stack([p['cmpk_w1'][l], p['cmpv_w1'][l]])
    w1bd = jnp.einsum('ksdf,he->skhdef', w1, eye2).reshape(2 * CMP_STRIDE, 2, KV_DIM, 2 * KV_DIM)
    pe = sel(p['cmpk_pe'][l], p['cmpv_pe'][l])
    pe = pe.transpose(1, 0, 2).reshape(2 * CMP_STRIDE, 4 * HEAD_DIM)
    b1 = sel(p['cmpk_b1'][l], p['cmpv_b1'][l]).reshape(1, -1)
    w2 = _block_diag(sel(p['cmpk_w2'][l], p['cmpv_w2'][l]))
    cmpw = (pe[:CMP_STRIDE], pe[CMP_STRIDE:], w1bd[:CMP_STRIDE].astype(BF16), w1bd[CMP_STRIDE:].astype(BF16),
            b1, w2.astype(BF16))
    dff = p['ffn_w_in'].shape[2] // 2
    ffn = (p['ffn_w_in'][l][:, :dff].astype(BF16), p['ffn_w_in'][l][:, dff:].astype(BF16),
           p['ffn_conv_w'][l], row(p['ffn_conv_b'][l]), p['ffn_w_down'][l].astype(BF16))
    norms = tuple(row(p[k][l]) for k in ('norm_mix_pre', 'norm_mix_post', 'norm_ffn_pre', 'norm_ffn_post'))
    return dict(wcat=wcat, wrow=wrow, wt=wt, rg=rg, cmp=cmpw, w_out=p['w_out'][l].astype(BF16), ffn=ffn, norms=norms)


def _pool_map(n_slc_pad, n_rows):
    j = np.arange(n_slc_pad)[:, None]
    m = np.arange(n_rows)[None, :]
    r = SLC_BLOCK // CMP_STRIDE
    return jnp.asarray(((m >= r * j) & (m <= r * j + r)).astype(np.float32))


def _heads_major(a, b, t):
    a5 = a.reshape(b, t, 2, N_KV, HEAD_DIM)
    return a5[:, :, 0].transpose(0, 2, 1, 3), a5[:, :, 1].transpose(0, 2, 1, 3)


def _rows_to_state(a_t):
    b, _, t = a_t.shape
    return a_t.reshape(b, 2, N_KV, HEAD_DIM, t).transpose(0, 4, 1, 2, 3)


def _prompt_mixer(hp, lw, tabs):
    b, t, d = hp.shape
    g_pre, g_post = lw['norms'][0], lw['norms'][1]
    xr, gate, qc, qr, gl, kvct, kvst, kvstb, kvwt, kvwtb = _proj_t(
        hp.reshape(b * t, d), g_pre, lw['wrow'], lw['wt'], *tabs, b, _div_tile(t, 512))
    c = xr.shape[1]
    y, htail = _rglru(xr.reshape(b, t, c), gate.reshape(b, t, c), jnp.zeros((b, 8, c), F32),
                      jnp.zeros((b, 1, c), F32), *lw['rg'], _div_tile(t, 512))
    npg = t // 128
    ident = jnp.arange(b * npg, dtype=jnp.int32).reshape(b, npg)
    kcs = _compress(kvct, ident, lw['cmp'], _div_tile(npg, 32), False)
    kck, kcv = _heads_major(kcs, b, t // CMP_STRIDE)
    nqb = t // Q_BLOCK
    n_slc = -(-t // SLC_BLOCK)
    onehot_t = (jnp.arange(HEAD_DIM)[:, None] == jnp.arange(t)[None, :] // SLC_BLOCK).astype(BF16)
    regroup = lambda a, w: a.reshape(b, nqb, Q_BLOCK, N_KV, GROUP, w).transpose(0, 3, 1, 4, 2, 5).reshape(
        b, N_KV, nqb, GROUP * Q_BLOCK, w)
    qc5 = regroup(qc, HEAD_DIM)
    qr5 = jnp.pad(regroup(qr, HEAD_DIM), ((0, 0),) * 4 + ((0, LANES - HEAD_DIM),))
    gt5 = jnp.pad(regroup(gl[:, :3 * N_HEADS], 3), ((0, 0),) * 4 + ((0, 5),))
    o5 = _nsa_prompt(qc5, qr5, gt5, kck, kcv, kvstb, kvwtb, onehot_t, _pool_map(n_slc, t // CMP_STRIDE), t)
    o = o5.reshape(b, N_KV, nqb, GROUP, Q_BLOCK, HEAD_DIM).transpose(0, 2, 4, 1, 3, 5).reshape(b * t, N_HEADS * HEAD_DIM)
    hp = _merge(y.reshape(b * t, c), o, lw['w_out'], g_post, hp.reshape(b * t, d), _div_tile(b * t, 512))
    win_buf = min(WINDOW, t)
    st = (_rows_to_state(kvct), _rows_to_state(kvst), _rows_to_state(kvwt[:, :, t - win_buf:]), htail[:, 7],
          xr.reshape(b, t, c)[:, t - 3:])
    return hp.reshape(b, t, d), st


def _sample_mixer(hs, lw, s_cos, s_sin, cmp_pool, slc_pool, page_table, win_t, layer, past_len, rg_h0, rg_conv0):
    b, t, d = hs.shape
    g_pre, g_post = lw['norms'][0], lw['norms'][1]
    xr, gate, qc, qr, kvc, kvs, kvw, gl = _proj(hs.reshape(b * t, d), g_pre, lw['wcat'], s_cos, s_sin, b * t)
    c = xr.shape[1]
    padt = lambda a: jnp.pad(a.reshape(b, t, -1), ((0, 0), (0, 8 - t), (0, 0)))
    pre8 = jnp.pad(rg_conv0, ((0, 0), (5, 0), (0, 0)))
    y8, htail = _rglru(padt(xr), padt(gate), pre8, rg_h0.reshape(b, 1, c), *lw['rg'], 8)
    y = y8[:, :t].reshape(b * t, c)
    npg = page_table.shape[1]
    kcs = _compress(cmp_pool, page_table, lw['cmp'], _div_tile(npg, 32), True)
    nj = -(-(past_len + t) // SLC_BLOCK)
    pp = _div_tile(npg, 16)
    bpt = pp * 128 // SLC_BLOCK
    njp = -(-nj // bpt) * bpt
    rows = GROUP * N_KV * t

    def qrows(a):
        a5 = a.reshape(b, t, N_KV, GROUP, HEAD_DIM).transpose(0, 3, 2, 1, 4)
        z = jnp.zeros_like(a5[:, :, 0])
        top = jnp.concatenate([a5[:, :, 0], z], axis=-1)
        bot = jnp.concatenate([z, a5[:, :, 1]], axis=-1)
        return jnp.stack([top, bot], axis=2).reshape(b, rows, LANES)

    gts = gl[:, :3 * N_HEADS].reshape(b, t, N_KV, GROUP, 3).transpose(0, 3, 2, 1, 4).reshape(b, rows, 3)
    gts = jnp.pad(gts, ((0, 0), (0, 0), (0, 5)))
    expand = (jnp.arange(LANES)[:, None] == jnp.arange(pp * 128)[None, :] // SLC_BLOCK).astype(BF16)
    new = jnp.concatenate([kvs.reshape(b, t, 256), kvw.reshape(b, t, 256)], axis=-1)
    new = jnp.pad(new, ((0, 0), (0, 8 - t), (0, 0)))
    o32 = _nsa_sample(page_table, qrows(qc), qrows(qr), gts, kcs, slc_pool, expand, _pool_map(njp, kcs.shape[1]),
                      new, win_t, layer, pp, past_len, t, nj)
    o5 = o32.reshape(b, GROUP, N_KV, t, N_KV, HEAD_DIM)
    o = jnp.stack([o5[:, :, 0, :, 0], o5[:, :, 1, :, 1]], axis=1)
    o = o.transpose(0, 3, 1, 2, 4).reshape(b * t, N_HEADS * HEAD_DIM).astype(BF16)
    hs = _merge(y, o, lw['w_out'], g_post, hs.reshape(b * t, d), b * t)
    nbuf = win_t.shape[3]
    win_state_t = jnp.concatenate([win_t[layer], kvw.reshape(b, t, 256).transpose(0, 2, 1)], axis=2)[:, :, -nbuf:]
    st = (kvc.reshape(b, t, 2, N_KV, HEAD_DIM), kvs.reshape(b, t, 2, N_KV, HEAD_DIM), _rows_to_state(win_state_t),
          htail[:, (t - 1) % 8], jnp.concatenate([rg_conv0, xr.reshape(b, t, c)], axis=1)[:, -3:])
    return hs.reshape(b, t, d), st


def kernel(x_prompt, x_sample, cache_cmp_kv, cache_slc_kv, cache_win_kv, state_rg_h, state_rg_conv, state_ffn_conv, page_table, norm_mix_pre, norm_mix_post, norm_ffn_pre, norm_ffn_post, w_in, rg_conv_w, rg_conv_b, rg_wa, rg_ba, rg_wx, rg_bx, rg_lambda, cmpk_pe, cmpk_w1, cmpk_b1, cmpk_w2, cmpv_pe, cmpv_w1, cmpv_b1, cmpv_w2, w_out, ffn_w_in, ffn_conv_w, ffn_conv_b, ffn_w_down):
    params = dict(norm_mix_pre=norm_mix_pre, norm_mix_post=norm_mix_post, norm_ffn_pre=norm_ffn_pre,
                  norm_ffn_post=norm_ffn_post, w_in=w_in, rg_conv_w=rg_conv_w, rg_conv_b=rg_conv_b, rg_wa=rg_wa,
                  rg_ba=rg_ba, rg_wx=rg_wx, rg_bx=rg_bx, rg_lambda=rg_lambda, cmpk_pe=cmpk_pe, cmpk_w1=cmpk_w1,
                  cmpk_b1=cmpk_b1, cmpk_w2=cmpk_w2, cmpv_pe=cmpv_pe, cmpv_w1=cmpv_w1, cmpv_b1=cmpv_b1,
                  cmpv_w2=cmpv_w2, w_out=w_out, ffn_w_in=ffn_w_in, ffn_conv_w=ffn_conv_w, ffn_conv_b=ffn_conv_b,
                  ffn_w_down=ffn_w_down)
    depth = w_in.shape[0]
    bp, tp, d = x_prompt.shape
    bs, ts, _ = x_sample.shape
    past_len = page_table.shape[1] * cache_cmp_kv.shape[2]
    dff = ffn_w_in.shape[2] // 2
    p_cos, p_sin = _rope_tables(jnp.arange(tp))
    p_tabs = (p_cos, p_sin, p_cos.T, p_sin.T)
    s_cos, s_sin = _rope_tables(past_len + jnp.arange(ts))
    s_cos, s_sin = jnp.tile(s_cos, (bs, 1)), jnp.tile(s_sin, (bs, 1))
    n_pool, page = cache_cmp_kv.shape[1], cache_cmp_kv.shape[2]
    pages_t = lambda c: c.transpose(0, 1, 3, 4, 5, 2).reshape(depth * n_pool, 2 * KV_DIM, page)
    cmp_pool, slc_pool = pages_t(cache_cmp_kv), pages_t(cache_slc_kv)
    win_t = cache_win_kv.transpose(0, 1, 3, 4, 5, 2).reshape(depth, bs, 2 * KV_DIM, cache_win_kv.shape[2])
    hp, hs = x_prompt, x_sample
    states_p, states_s = [], []
    for l in range(depth):
        lw = _layer_weights(l, params)
        g_fpre, g_fpost = lw['norms'][2], lw['norms'][3]
        hp, st = _prompt_mixer(hp, lw, p_tabs)
        tmf = _div_tile(tp, 512)
        hp, tail = _ffn(hp, jnp.zeros((bp, 8, dff), F32), g_fpre, g_fpost, *lw['ffn'], tmf, _div_tile(dff, 1024), 1)
        states_p.append(st + (tail[:, -1, 6:],))
        hs, st = _sample_mixer(hs, lw, s_cos, s_sin, cmp_pool, slc_pool, page_table + l * n_pool, win_t, l,
                               past_len, state_rg_h[l], state_rg_conv[l])
        x_tm = hs.transpose(1, 0, 2).reshape(1, ts * bs, d)
        pre_tm = state_ffn_conv[l].transpose(1, 0, 2).reshape(1, 2 * bs, dff)
        out_tm, tail = _ffn(x_tm, pre_tm, g_fpre, g_fpost, *lw['ffn'], ts * bs, _div_tile(dff, 1024), bs)
        hs = out_tm.reshape(ts, bs, d).transpose(1, 0, 2)
        states_s.append(st + (tail.reshape(2, bs, dff).transpose(1, 0, 2),))
    p_cmp, p_slc, p_win, p_rgh, p_rgc, p_ffc = [jnp.stack(z) for z in zip(*states_p)]
    s_cmp, s_slc, s_win, s_rgh, s_rgc, s_ffc = [jnp.stack(z) for z in zip(*states_s)]
    return (hp, hs, p_cmp, s_cmp, p_slc, s_slc, p_win, s_win, p_rgh, s_rgh, p_rgc, s_rgc, p_ffc, s_ffc)
```

```python
import functools

import jax
import jax.numpy as jnp
import numpy as np
from jax import lax
from jax.experimental import pallas as pl
from jax.experimental.pallas import tpu as pltpu

F32 = jnp.float32
BF16 = jnp.bfloat16

HEAD_DIM = 64
N_KV = 2
GROUP = 4
N_HEADS = N_KV * GROUP
KV_DIM = N_KV * HEAD_DIM
CMP_STRIDE = 16
SLC_BLOCK = 64
N_SELECT = 16
N_LOCAL = 2
WINDOW = 512
Q_BLOCK = 128
RG_C = 8.0
ROPE_THETA = 10000.0
EPS = 1e-6

V7X_VMEM_BYTES = 64 * 1024 * 1024
VMEM_LIMIT = 48 * 1024 * 1024
SUBLANES = 8
LANES = 128

LOG2E = 1.4426950408889634
NEG = -2.0 ** 100
BIG = 1e30


def _cparams(sem):
    return pltpu.CompilerParams(dimension_semantics=sem, vmem_limit_bytes=VMEM_LIMIT)


def _div_tile(n, pref):
    t = min(n, pref)
    while n % t:
        t -= 1
    return t


def _rms(x, g):
    y = x * lax.rsqrt(jnp.mean(x * x, axis=-1, keepdims=True) + EPS)
    return y * g


def _expm1(x):
    u = jnp.exp(x)
    near = (jnp.abs(x) < 0.5) & (u != 1.0)
    kahan = (u - 1.0) * x / jnp.where(near, jnp.log(u), 1.0)
    return jnp.where(u == 1.0, x, jnp.where(near, kahan, u - 1.0))


def _nt(a, b, precision=None):
    return lax.dot_general(a, b, (((1,), (1,)), ((), ())), preferred_element_type=F32, precision=precision)


def _mm(a, b):
    return jnp.dot(a, b, preferred_element_type=F32)


C_XR, C_GATE, C_Q, C_QROT, C_KVC, C_KVS, C_KSROT, C_KVW, C_KWROT, C_GL, C_END = (
    0, 512, 1024, 1536, 2048, 2304, 2560, 2688, 2944, 3072, 3200)


def _proj_kernel(x_ref, g_ref, w_ref, cos_ref, sin_ref,
                 xr_ref, gate_ref, qc_ref, qr_ref, kvc_ref, kvs_ref, kvw_ref, gl_ref):
    xn = _rms(x_ref[...], g_ref[...]).astype(BF16)

    def mm(c0, c1):
        return _mm(xn, w_ref[:, c0:c1])

    cos = cos_ref[...]
    sin = sin_ref[...]
    xr_ref[...] = mm(C_XR, C_GATE)
    gate_ref[...] = mm(C_GATE, C_Q)
    q = mm(C_Q, C_QROT)
    qrot = mm(C_QROT, C_KVC)
    cos4 = jnp.concatenate([cos] * 4, axis=1)
    sin4 = jnp.concatenate([sin] * 4, axis=1)
    scale = HEAD_DIM ** -0.5
    qc_ref[...] = (q * scale).astype(BF16)
    qr_ref[...] = ((q * cos4 + qrot * sin4) * scale).astype(BF16)
    kvc_ref[...] = mm(C_KVC, C_KVS)
    kvs = mm(C_KVS, C_KSROT)
    ksrot = mm(C_KSROT, C_KVW)
    kvs_ref[:, 0:KV_DIM] = kvs[:, 0:KV_DIM] * cos + ksrot * sin
    kvs_ref[:, KV_DIM:] = kvs[:, KV_DIM:]
    kvw = mm(C_KVW, C_KWROT)
    kwrot = mm(C_KWROT, C_GL)
    kvw_ref[:, 0:KV_DIM] = kvw[:, 0:KV_DIM] * cos + kwrot * sin
    kvw_ref[:, KV_DIM:] = kvw[:, KV_DIM:]
    gl_ref[...] = jax.nn.sigmoid(mm(C_GL, C_END))


R_KVC, R_KVS, R_KSROT, R_KVW, R_KWROT, R_END = 0, 256, 512, 640, 896, 1024


def _proj_t_kernel(x_ref, g_ref, w_ref, wt_ref, cos_ref, sin_ref, cost_ref, sint_ref,
                   xr_ref, gate_ref, qc_ref, qr_ref, gl_ref, kvct_ref, kvst_ref, kvstb_ref, kvwt_ref, kvwtb_ref):
    xn = _rms(x_ref[...], g_ref[...]).astype(BF16)

    def mm(c0, c1):
        return _mm(xn, w_ref[:, c0:c1])

    def mt(r0, r1):
        return _nt(wt_ref[r0:r1, :], xn)

    xr_ref[...] = mm(0, 512)
    gate_ref[...] = mm(512, 1024)
    q = mm(1024, 1536)
    qrot = mm(1536, 2048)
    cos4 = jnp.concatenate([cos_ref[...]] * 4, axis=1)
    sin4 = jnp.concatenate([sin_ref[...]] * 4, axis=1)
    scale = HEAD_DIM ** -0.5 * LOG2E
    qc_ref[...] = (q * scale).astype(BF16)
    qr_ref[...] = ((q * cos4 + qrot * sin4) * scale).astype(BF16)
    gl_ref[...] = jax.nn.sigmoid(mm(2048, 2176))
    kvct_ref[0] = mt(R_KVC, R_KVS)
    cost, sint = cost_ref[...], sint_ref[...]
    for lo, rot, end, f_ref, b_ref in ((R_KVS, R_KSROT, R_KVW, kvst_ref, kvstb_ref),
                                       (R_KVW, R_KWROT, R_END, kvwt_ref, kvwtb_ref)):
        kv = mt(lo, rot)
        k = kv[0:KV_DIM] * cost + mt(rot, end) * sint
        f_ref[0, 0:KV_DIM, :] = k
        f_ref[0, KV_DIM:, :] = kv[KV_DIM:]
        b_ref[0, 0:KV_DIM, :] = k.astype(BF16)
        b_ref[0, KV_DIM:, :] = kv[KV_DIM:].astype(BF16)


def _proj_t(x2d, g, wrow, wt, cos, sin, cost, sint, b, tm):
    m, d = x2d.shape
    t = m // b
    nper = t // tm
    row = lambda w: pl.BlockSpec((tm, w), lambda i: (i, 0))
    tab = pl.BlockSpec((tm, LANES), lambda i: (i % nper, 0))
    tabt = pl.BlockSpec((KV_DIM, tm), lambda i: (0, i % nper))
    tr = pl.BlockSpec((1, 2 * KV_DIM, tm), lambda i: (i // nper, 0, i % nper))
    full = lambda a: pl.BlockSpec(a.shape, lambda i: (0,) * a.ndim)
    widths = (512, 512, 512, 512, 128)
    dtypes = (F32, F32, BF16, BF16, F32)
    tdt = (F32, F32, BF16, F32, BF16)
    return pl.pallas_call(
        _proj_t_kernel,
        grid=(m // tm,),
        in_specs=[row(d), full(g), full(wrow), full(wt), tab, tab, tabt, tabt],
        out_specs=[row(w) for w in widths] + [tr] * 5,
        out_shape=[jax.ShapeDtypeStruct((m, w), dt) for w, dt in zip(widths, dtypes)]
        + [jax.ShapeDtypeStruct((b, 2 * KV_DIM, t), dt) for dt in tdt],
        compiler_params=_cparams(("arbitrary",)),
        name="proj_t",
    )(x2d, g, wrow, wt, cos, sin, cost, sint)


def _proj(x2d, g, wcat, cos, sin, tm):
    m, d = x2d.shape
    nper = cos.shape[0] // tm
    row = lambda w: pl.BlockSpec((tm, w), lambda i: (i, 0))
    tab = pl.BlockSpec((tm, LANES), lambda i: (i % nper, 0))
    widths = (512, 512, 512, 512, 256, 256, 256, 128)
    dtypes = (F32, F32, BF16, BF16, F32, F32, F32, F32)
    return pl.pallas_call(
        _proj_kernel,
        grid=(m // tm,),
        in_specs=[row(d), pl.BlockSpec((1, d), lambda i: (0, 0)),
                  pl.BlockSpec(wcat.shape, lambda i: (0, 0)), tab, tab],
        out_specs=[row(w) for w in widths],
        out_shape=[jax.ShapeDtypeStruct((m, w), dt) for w, dt in zip(widths, dtypes)],
        compiler_params=_cparams(("arbitrary",)),
        name="proj",
    )(x2d, g, wcat, cos, sin)


def _rglru_kernel(xr_ref, gate_ref, pre_ref, h0_ref, cw_ref, cb_ref, wa_ref, wx_ref, ba_ref, bx_ref, lam_ref,
                  y_ref, htail_ref, xpad, hc, a_s, b_s):
    tt = xr_ref.shape[1]
    j = pl.program_id(1)

    @pl.when(j == 0)
    def _():
        xpad[0:8, :] = pre_ref[0]
        hc[...] = jnp.broadcast_to(h0_ref[0], hc.shape)

    @pl.when(j > 0)
    def _():
        xpad[0:8, :] = xpad[tt:tt + 8, :]

    xpad[8:8 + tt, :] = xr_ref[0]
    xc = cb_ref[...]
    for k in range(4):
        xc = xc + xpad[pl.ds(5 + k, tt), :] * cw_ref[k:k + 1, :]
    xb = xc.astype(BF16)
    r = jax.nn.sigmoid(_mm(xb, wa_ref[...]) + ba_ref[...])
    i = jax.nn.sigmoid(_mm(xb, wx_ref[...]) + bx_ref[...])
    log_a = RG_C * r * jax.nn.log_sigmoid(lam_ref[...])
    a_s[...] = jnp.exp(log_a)
    b_s[...] = jnp.sqrt(-_expm1(2.0 * log_a)) * (i * xc)

    row = lax.broadcasted_iota(jnp.int32, (8, a_s.shape[1]), 0)

    def body(gi, h):
        r0 = pl.multiple_of(gi * 8, 8)
        av = a_s[pl.ds(r0, 8), :]
        bv = b_s[pl.ds(r0, 8), :]
        for s in (1, 2, 4):
            keep = row >= s
            a_sh = pltpu.roll(av, s, 0)
            b_sh = pltpu.roll(bv, s, 0)
            bv = jnp.where(keep, av * b_sh + bv, bv)
            av = jnp.where(keep, av * a_sh, av)
        hs = av * h + bv
        b_s[pl.ds(r0, 8), :] = hs
        return jnp.broadcast_to(hs[7:8, :], hs.shape)

    h = lax.fori_loop(0, tt // 8, body, hc[...])
    hc[...] = h
    htail_ref[0] = b_s[tt - 8:tt, :]
    y_ref[0] = (b_s[...] * jax.nn.gelu(gate_ref[0])).astype(y_ref.dtype)


def _rglru(xr, gate, pre8, h0, cw, cb, wa, wx, ba, bx, lam, tt):
    b, t, c = xr.shape
    full = lambda a: pl.BlockSpec(a.shape, lambda i, j: (0,) * a.ndim)
    return pl.pallas_call(
        _rglru_kernel,
        grid=(b, t // tt),
        in_specs=[pl.BlockSpec((1, tt, c), lambda i, j: (i, j, 0)),
                  pl.BlockSpec((1, tt, c), lambda i, j: (i, j, 0)),
                  pl.BlockSpec((1, 8, c), lambda i, j: (i, 0, 0)),
                  pl.BlockSpec((1, 1, c), lambda i, j: (i, 0, 0)),
                  full(cw), full(cb), full(wa), full(wx), full(ba), full(bx), full(lam)],
        out_specs=[pl.BlockSpec((1, tt, c), lambda i, j: (i, j, 0)),
                   pl.BlockSpec((1, 8, c), lambda i, j: (i, 0, 0))],
        out_shape=[jax.ShapeDtypeStruct((b, t, c), BF16), jax.ShapeDtypeStruct((b, 8, c), F32)],
        scratch_shapes=[pltpu.VMEM((tt + 8, c), F32), pltpu.VMEM((8, c), F32),
                        pltpu.VMEM((tt, c), F32), pltpu.VMEM((tt, c), F32)],
        compiler_params=_cparams(("arbitrary", "arbitrary")),
        name="rglru",
    )(xr, gate, pre8, h0, cw, cb, wa, wx, ba, bx, lam)


def _page_copy(pt_ref, src_ref, buf, sem, idx, slot, p, seq_pages):
    page = pt_ref[idx]
    if seq_pages is None:
        src = src_ref.at[page]
    else:
        src = src_ref.at[page // seq_pages, :, pl.ds(pl.multiple_of((page % seq_pages) * 128, 128), 128)]
    return pltpu.make_async_copy(src, buf.at[slot, p], sem.at[slot])


def _cmp_kernel(pt_ref, pool_ref, pea_ref, peb_ref, w1a_ref, w1b_ref, b1_ref, w2_ref, out_ref,
                buf, sem, xbuf, ua_s, *, pp, seq_pages):
    nb, nt = pl.num_programs(0), pl.num_programs(1)
    j = pl.program_id(1)
    step = pl.program_id(0) * nt + j
    slot = step % 2
    m = pp * 8

    def fetch(st, sl):
        for p in range(pp):
            _page_copy(pt_ref, pool_ref, buf, sem, st * pp + p, sl, p, seq_pages).start()

    @pl.when(step == 0)
    def _():
        fetch(step, slot)

    @pl.when(step + 1 < nb * nt)
    def _():
        fetch(step + 1, 1 - slot)

    for p in range(pp):
        _page_copy(pt_ref, pool_ref, buf, sem, step * pp + p, slot, p, seq_pages).wait()

    def to_rows(p, carry):
        r0 = pl.multiple_of(p * 128, 128)
        for kv in range(2):
            xbuf[kv, pl.ds(r0, 128), :] = buf[slot, p, kv * KV_DIM:(kv + 1) * KV_DIM, :].T
        return carry

    lax.fori_loop(0, pp, to_rows, 0, unroll=min(pp, 4))

    acc_a = [jnp.zeros((m, 256), F32)] * 2
    acc_b = [jnp.zeros((m, 256), F32)] * 2
    for s in range(CMP_STRIDE):
        for kv in range(2):
            xs = xbuf[kv, pl.ds(s, m, stride=CMP_STRIDE), :]
            lanes = slice(kv * KV_DIM, (kv + 1) * KV_DIM)
            acc_a[kv] = acc_a[kv] + _mm((xs + pea_ref[s:s + 1, lanes]).astype(BF16), w1a_ref[s, kv])
            acc_b[kv] = acc_b[kv] + _mm((xs + peb_ref[s:s + 1, lanes]).astype(BF16), w1b_ref[s, kv])
    acc_a = jnp.concatenate(acc_a, axis=1)
    acc_b = jnp.concatenate(acc_b, axis=1)

    @pl.when(j == 0)
    def _():
        ua_s[0:8, :] = jnp.zeros((8, 512), F32)

    @pl.when(j > 0)
    def _():
        ua_s[0:8, :] = ua_s[m:m + 8, :]

    ua_s[8:8 + m, :] = acc_a
    hid = jax.nn.gelu(ua_s[pl.ds(7, m), :] + acc_b + b1_ref[...])
    out_ref[0] = _mm(hid.astype(BF16), w2_ref[...]).astype(out_ref.dtype)


def _compress(pool, page_table, cw, pp, paged):
    b, npg = page_table.shape
    nt = npg // pp
    m = pp * 8
    full = lambda a: pl.BlockSpec(a.shape, lambda i, j, pt: (0,) * a.ndim)
    pea, peb, w1a, w1b, b1, w2 = cw
    gs = pltpu.PrefetchScalarGridSpec(
        num_scalar_prefetch=1,
        grid=(b, nt),
        in_specs=[pl.BlockSpec(memory_space=pl.ANY), full(pea), full(peb), full(w1a), full(w1b), full(b1), full(w2)],
        out_specs=pl.BlockSpec((1, m, 256), lambda i, j, pt: (i, j, 0)),
        scratch_shapes=[pltpu.VMEM((2, pp, 2 * KV_DIM, 128), F32), pltpu.SemaphoreType.DMA((2,)),
                        pltpu.VMEM((2, pp * 128, KV_DIM), F32), pltpu.VMEM((m + 8, 512), F32)],
    )
    return pl.pallas_call(
        functools.partial(_cmp_kernel, pp=pp, seq_pages=None if paged else npg),
        grid_spec=gs,
        out_shape=jax.ShapeDtypeStruct((b, nt * m, 256), BF16),
        compiler_params=_cparams(("arbitrary", "arbitrary")),
        name="compress",
    )(page_table.reshape(-1), pool, pea, peb, w1a, w1b, b1, w2)


def _softmax_rows(s, mask):
    sm = jnp.where(mask, s, -BIG)
    mx = jnp.max(sm, axis=-1, keepdims=True)
    e = jnp.where(mask, jnp.exp(sm - mx), 0.0)
    return e / jnp.maximum(jnp.sum(e, axis=-1, keepdims=True), 1e-30)


def _nsa_prompt_kernel(qc_ref, qr_ref, g_ref, kck_ref, kcv_ref, kvs_ref, kvw_ref, oh_ref, pool_ref,
                       o_ref, sc_s, rank_s, *, t_len, kc_len, wl):
    qb = pl.program_id(1)
    start = qb * Q_BLOCK
    nc = kck_ref.shape[2]
    nj = pool_ref.shape[0]
    rows = GROUP * Q_BLOCK
    qpos = start + (lax.broadcasted_iota(jnp.int32, (rows, 1), 0) & (Q_BLOCK - 1))

    heads = range(N_KV)
    k_rows = [slice(h * HEAD_DIM, (h + 1) * HEAD_DIM) for h in heads]
    v_rows = [slice(KV_DIM + h * HEAD_DIM, KV_DIM + (h + 1) * HEAD_DIM) for h in heads]
    n_chunks = (start + Q_BLOCK + kc_len - 1) // kc_len
    n_full = n_chunks - 1
    k_diag = pl.multiple_of(n_full * kc_len, kc_len)
    base = pl.multiple_of(jnp.maximum(start + Q_BLOCK - wl, 0), Q_BLOCK)

    qpos1 = start + lax.broadcasted_iota(jnp.int32, (Q_BLOCK, 1), 0)
    mrow = lax.broadcasted_iota(jnp.int32, (Q_BLOCK, nc), 1)
    cbias = jnp.where((mrow >= 1) & (mrow * CMP_STRIDE + (CMP_STRIDE - 1) <= qpos1), 0.0, NEG)
    dbias = jnp.where(k_diag + lax.broadcasted_iota(jnp.int32, (Q_BLOCK, kc_len), 1) <= qpos1, 0.0, NEG)
    dist = qpos1 - (base + lax.broadcasted_iota(jnp.int32, (Q_BLOCK, wl), 1))
    wbias = jnp.where((dist >= 0) & (dist < WINDOW), 0.0, NEG)
    any_cmp = jnp.where(qpos >= 2 * CMP_STRIDE - 1, 1.0, 0.0)

    def add_bias(s, bias):
        return (s.reshape(GROUP, Q_BLOCK, s.shape[1]) + bias[None]).reshape(s.shape)

    def with_ones(vt):
        return jnp.concatenate([vt, jnp.ones(vt.shape, vt.dtype)], axis=0)

    r_cmp, inv_cmp = [], []
    jidx = lax.broadcasted_iota(jnp.int32, (nj, Q_BLOCK), 0)
    lag = ((start + lax.broadcasted_iota(jnp.int32, (nj, Q_BLOCK), 1)) >> 6) - jidx
    forced = (jidx == 0) | ((lag >= 0) & (lag < N_LOCAL))
    for h in heads:
        s = add_bias(_nt(qc_ref[0, h, 0], kck_ref[0, h]), cbias)
        e = jnp.exp2(s - jnp.max(s, axis=-1, keepdims=True))
        r = _mm(e.astype(BF16), kcv_ref[0, h])
        inv = any_cmp / jnp.maximum(r[:, HEAD_DIM:HEAD_DIM + 1], 1e-30)
        p = e * inv
        psum = p[0:Q_BLOCK] + p[Q_BLOCK:2 * Q_BLOCK] + p[2 * Q_BLOCK:3 * Q_BLOCK] + p[3 * Q_BLOCK:]
        score_t = _nt(pool_ref[...], psum, precision=lax.Precision.HIGHEST)
        sc_s[h] = jnp.where(lag >= 0, jnp.where(forced, BIG, score_t), -BIG)
        rank_s[h] = jnp.zeros((nj, Q_BLOCK), F32)
        r_cmp.append(r)
        inv_cmp.append(inv)

    j_last = (start + Q_BLOCK - 1) >> 6
    sub = lax.broadcasted_iota(jnp.int32, (SUBLANES, Q_BLOCK), 0)
    n_grp = nj // SUBLANES
    for gi in range(n_grp):
        @pl.when(gi * SUBLANES <= j_last)
        def _():
            for h in heads:
                scv = sc_s[h]
                cnt = [jnp.zeros((SUBLANES, Q_BLOCK), F32)] * n_grp
                for ii in range(SUBLANES):
                    si = sc_s[h, gi * SUBLANES + ii:gi * SUBLANES + ii + 1, :]
                    for r in range(n_grp):
                        blk = scv[r * SUBLANES:(r + 1) * SUBLANES]
                        if r == gi:
                            beat = jnp.where(sub > ii, jnp.where(si >= blk, 1.0, 0.0), jnp.where(si > blk, 1.0, 0.0))
                        else:
                            beat = jnp.where(si >= blk, 1.0, 0.0) if r > gi else jnp.where(si > blk, 1.0, 0.0)
                        cnt[r] = cnt[r] + beat
                rank_s[h] = rank_s[h] + jnp.concatenate(cnt, axis=0)

    qaug = []
    for h in heads:
        bias_t = jnp.where(rank_s[h] < min(N_SELECT, nj), 0.0, NEG)
        pad_t = jnp.concatenate([jnp.zeros((HEAD_DIM, Q_BLOCK), F32), bias_t]
                                + ([jnp.zeros((HEAD_DIM - nj, Q_BLOCK), F32)] if nj < HEAD_DIM else []), axis=0)
        bias = pad_t.T[:, HEAD_DIM:]
        qa = [jnp.concatenate([qr_ref[0, h, 0, g * Q_BLOCK:(g + 1) * Q_BLOCK, :].astype(F32), bias], axis=1)
              for g in range(GROUP)]
        qaug.append(jnp.concatenate(qa, axis=0).astype(BF16))

    def chunk(k0, carry, diagonal):
        out = []
        for h in heads:
            m_i, acc = carry[h]
            kaug = jnp.concatenate([kvs_ref[0, k_rows[h], pl.ds(k0, kc_len)], oh_ref[:, pl.ds(k0, kc_len)]], axis=0)
            sc = _mm(qaug[h], kaug)
            if diagonal:
                sc = add_bias(sc, dbias)
            m_n = jnp.maximum(m_i, jnp.max(sc, axis=-1, keepdims=True))
            pe = jnp.exp2(sc - m_n).astype(BF16)
            acc = jnp.exp2(m_i - m_n) * acc + _nt(pe, with_ones(kvs_ref[0, v_rows[h], pl.ds(k0, kc_len)]))
            out.append((m_n, acc))
        return tuple(out)

    init = tuple((jnp.full((rows, 1), -3e38, F32), jnp.zeros((rows, LANES), F32)) for _ in heads)
    carry = lax.fori_loop(0, n_full, lambda c, cr: chunk(pl.multiple_of(c * kc_len, kc_len), cr, False), init)
    carry = chunk(k_diag, carry, True)

    for h in heads:
        sw = add_bias(_mm(qr_ref[0, h, 0], kvw_ref[0, k_rows[h], pl.ds(base, wl)]), wbias)
        ew = jnp.exp2(sw - jnp.max(sw, axis=-1, keepdims=True)).astype(BF16)
        r_win = _nt(ew, with_ones(kvw_ref[0, v_rows[h], pl.ds(base, wl)]))
        r_slc = carry[h][1]
        gt = g_ref[0, h, 0]
        den = lambda r: jnp.maximum(r[:, HEAD_DIM:HEAD_DIM + 1], 1e-30)
        o = (r_cmp[h] * (gt[:, 0:1] * inv_cmp[h]) + r_slc * (gt[:, 1:2] / den(r_slc))
             + r_win * (gt[:, 2:3] / den(r_win)))
        o_ref[0, h, 0] = o[:, 0:HEAD_DIM].astype(o_ref.dtype)


def _nsa_prompt(qc, qr, gt, kck, kcv, kvst, kvwt, onehot_t, pool_t, t_len):
    b = qc.shape[0]
    nqb = t_len // Q_BLOCK
    rows = GROUP * Q_BLOCK
    kc_len = _div_tile(t_len, 512)
    wl = min(WINDOW + Q_BLOCK, t_len)
    qspec = lambda w: pl.BlockSpec((1, N_KV, 1, rows, w), lambda i, j: (i, 0, j, 0, 0))
    seq = lambda a: pl.BlockSpec((1,) + a.shape[1:], lambda i, j: (i,) + (0,) * (a.ndim - 1))
    return pl.pallas_call(
        functools.partial(_nsa_prompt_kernel, t_len=t_len, kc_len=kc_len, wl=wl),
        grid=(b, nqb),
        in_specs=[qspec(HEAD_DIM), qspec(HEAD_DIM), qspec(8), seq(kck), seq(kcv), seq(kvst), seq(kvwt),
                  pl.BlockSpec(onehot_t.shape, lambda i, j: (0, 0)), pl.BlockSpec(pool_t.shape, lambda i, j: (0, 0))],
        out_specs=qspec(HEAD_DIM),
        out_shape=jax.ShapeDtypeStruct((b, N_KV, nqb, rows, HEAD_DIM), BF16),
        scratch_shapes=[pltpu.VMEM((N_KV,) + pool_t.shape[:1] + (Q_BLOCK,), F32)] * 2,
        compiler_params=_cparams(("arbitrary", "arbitrary")),
        name="nsa_prompt",
    )(qc, qr, gt, kck, kcv, kvst, kvwt, onehot_t, pool_t)


def _nsa_sample_kernel(pt_ref, qc_ref, qr_ref, g_ref, kc_ref, pool_ref, exp_ref, pmap_ref, new_ref, win_ref,
                       o_ref, buf, sem, bias_s, m_s, l_s, acc_s, ocmp_s, *, pp, past_len, n_tok, nj):
    nb, nt = pl.num_programs(0), pl.num_programs(1)
    j = pl.program_id(1)
    step = pl.program_id(0) * nt + j
    slot = step % 2
    rows = GROUP * N_KV * n_tok
    tk = pp * 128
    bpt = tk // SLC_BLOCK

    def fetch(st, sl):
        for p in range(pp):
            _page_copy(pt_ref, pool_ref, buf, sem, st * pp + p, sl, p, None).start()

    @pl.when(step == 0)
    def _():
        fetch(step, slot)

    @pl.when(step + 1 < nb * nt)
    def _():
        fetch(step + 1, 1 - slot)

    rid = lax.broadcasted_iota(jnp.int32, (rows, 1), 0)
    tok = rid % n_tok
    qpos = past_len + tok

    @pl.when(j == 0)
    def _():
        kc = kc_ref[0]
        nc = kc.shape[0]
        s = _nt(qc_ref[0], kc[:, 0:KV_DIM])
        mrow = lax.broadcasted_iota(jnp.int32, (rows, nc), 1)
        valid = (mrow >= 1) & (mrow * CMP_STRIDE + (CMP_STRIDE - 1) <= qpos)
        p = _softmax_rows(s, valid)
        ocmp_s[...] = _mm(p.astype(BF16), kc[:, KV_DIM:])
        r8 = N_KV * n_tok
        psum = p[0:r8] + p[r8:2 * r8] + p[2 * r8:3 * r8] + p[3 * r8:]
        score = jnp.dot(psum, pmap_ref[...], precision=lax.Precision.HIGHEST,
                        preferred_element_type=F32)
        njp = score.shape[1]
        jidx = lax.broadcasted_iota(jnp.int32, (r8, njp), 1)
        qp_r = past_len + lax.broadcasted_iota(jnp.int32, (r8, njp), 0) % n_tok
        lag = (qp_r >> 6) - jidx
        forced = (jidx == 0) | ((lag >= 0) & (lag < N_LOCAL))
        sc = jnp.where((lag >= 0) & (jidx < nj), jnp.where(forced, BIG, score), -BIG)
        lane = lax.broadcasted_iota(jnp.int32, (r8, LANES), 1)
        cnt = [jnp.zeros((r8, LANES), F32)] * (njp // LANES)
        for i in range(nj):
            si = sc[:, i:i + 1]
            for v in range(njp // LANES):
                blk = sc[:, v * LANES:(v + 1) * LANES]
                if v * LANES > i:
                    beat = jnp.where(si >= blk, 1.0, 0.0)
                elif (v + 1) * LANES <= i:
                    beat = jnp.where(si > blk, 1.0, 0.0)
                else:
                    beat = jnp.where(lane + v * LANES > i, jnp.where(si >= blk, 1.0, 0.0), jnp.where(si > blk, 1.0, 0.0))
                cnt[v] = cnt[v] + beat
        bias = jnp.where(jnp.concatenate(cnt, axis=1) < min(N_SELECT, nj), 0.0, NEG)
        for tj in range(njp // bpt):
            b8 = pltpu.roll(bias, (njp - bpt * tj) % njp, 1)[:, 0:LANES]
            bias_s[tj] = jnp.concatenate([b8] * GROUP, axis=0)
        m_s[...] = jnp.full(m_s.shape, -3e38, F32)
        l_s[...] = jnp.zeros(l_s.shape, F32)
        acc_s[...] = jnp.zeros(acc_s.shape, F32)

    for p in range(pp):
        _page_copy(pt_ref, pool_ref, buf, sem, step * pp + p, slot, p, None).wait()

    def online(sc, pv):
        m_i = m_s[...]
        m_n = jnp.maximum(m_i, jnp.max(sc, axis=-1, keepdims=True))
        alpha = jnp.exp(m_i - m_n)
        pe = jnp.exp(sc - m_n)
        l_s[...] = alpha * l_s[...] + jnp.sum(pe, axis=-1, keepdims=True)
        acc_s[...] = alpha * acc_s[...] + pv(pe.astype(BF16))
        m_s[...] = m_n

    qr = qr_ref[0]
    sc = jnp.concatenate([_mm(qr, buf[slot, p, 0:KV_DIM, :].astype(BF16)) for p in range(pp)], axis=1)
    sc = sc + _mm(bias_s[j].astype(BF16), exp_ref[...])

    def pv_pages(pe):
        o = _nt(pe[:, 0:128], buf[slot, 0, KV_DIM:, :].astype(BF16))
        for p in range(1, pp):
            o = o + _nt(pe[:, p * 128:(p + 1) * 128], buf[slot, p, KV_DIM:, :].astype(BF16))
        return o

    online(sc, pv_pages)

    @pl.when(j == nt - 1)
    def _():
        new = new_ref[0]
        nk = new.shape[0]
        kidx = lax.broadcasted_iota(jnp.int32, (rows, nk), 1)
        last_bias = bias_s[(past_len // SLC_BLOCK) // bpt][:, (past_len // SLC_BLOCK) % bpt:(past_len // SLC_BLOCK) % bpt + 1]
        scn = _nt(qr, new[:, 0:KV_DIM].astype(BF16)) + last_bias
        scn = jnp.where((kidx <= tok) & (kidx < n_tok), scn, NEG)
        online(scn, lambda pe: _mm(pe, new[:, KV_DIM:2 * KV_DIM].astype(BF16)))
        o_slc = acc_s[...] / jnp.maximum(l_s[...], 1e-30)

        nbuf = win_ref.shape[3]
        s1 = _mm(qr, win_ref[0, 0, 0:KV_DIM, :].astype(BF16))
        d1 = tok + nbuf - lax.broadcasted_iota(jnp.int32, (rows, nbuf), 1)
        ok1 = (d1 >= 0) & (d1 < WINDOW)
        s2 = _nt(qr, new[:, 2 * KV_DIM:3 * KV_DIM].astype(BF16))
        d2 = tok - kidx
        ok2 = (d2 >= 0) & (d2 < WINDOW) & (kidx < n_tok)
        s1 = jnp.where(ok1, s1, -BIG)
        s2 = jnp.where(ok2, s2, -BIG)
        mx = jnp.maximum(jnp.max(s1, axis=-1, keepdims=True), jnp.max(s2, axis=-1, keepdims=True))
        e1 = jnp.where(ok1, jnp.exp(s1 - mx), 0.0)
        e2 = jnp.where(ok2, jnp.exp(s2 - mx), 0.0)
        den = jnp.maximum(jnp.sum(e1, axis=-1, keepdims=True) + jnp.sum(e2, axis=-1, keepdims=True), 1e-30)
        o_win = (_nt(e1.astype(BF16), win_ref[0, 0, KV_DIM:, :].astype(BF16))
                 + _mm(e2.astype(BF16), new[:, 3 * KV_DIM:].astype(BF16))) / den
        gt = g_ref[0]
        o_ref[0] = gt[:, 0:1] * ocmp_s[...] + gt[:, 1:2] * o_slc + gt[:, 2:3] * o_win


def _nsa_sample(page_table, qc, qr, gt, kc, pool, expand, pmap, new, win, layer, pp, past_len, n_tok, nj):
    b, npg = page_table.shape
    nt = npg // pp
    rows = GROUP * N_KV * n_tok
    tk = pp * 128
    bpt = tk // SLC_BLOCK
    njp = pmap.shape[1]
    seq = lambda a: pl.BlockSpec((1,) + a.shape[1:], lambda i, j, pt: (i,) + (0,) * (a.ndim - 1))
    full = lambda a: pl.BlockSpec(a.shape, lambda i, j, pt: (0,) * a.ndim)
    gs = pltpu.PrefetchScalarGridSpec(
        num_scalar_prefetch=1,
        grid=(b, nt),
        in_specs=[seq(qc), seq(qr), seq(gt), seq(kc), pl.BlockSpec(memory_space=pl.ANY), full(expand), full(pmap),
                  seq(new), pl.BlockSpec((1, 1) + win.shape[2:], lambda i, j, pt: (layer, i, 0, 0))],
        out_specs=pl.BlockSpec((1, rows, LANES), lambda i, j, pt: (i, 0, 0)),
        scratch_shapes=[pltpu.VMEM((2, pp, 2 * KV_DIM, 128), F32), pltpu.SemaphoreType.DMA((2,)),
                        pltpu.VMEM((njp // bpt, rows, LANES), F32),
                        pltpu.VMEM((rows, 1), F32), pltpu.VMEM((rows, 1), F32),
                        pltpu.VMEM((rows, LANES), F32), pltpu.VMEM((rows, LANES), F32)],
    )
    return pl.pallas_call(
        functools.partial(_nsa_sample_kernel, pp=pp, past_len=past_len, n_tok=n_tok, nj=nj),
        grid_spec=gs,
        out_shape=jax.ShapeDtypeStruct((b, rows, LANES), F32),
        compiler_params=_cparams(("arbitrary", "arbitrary")),
        name="nsa_sample",
    )(page_table.reshape(-1), qc, qr, gt, kc, pool, expand, pmap, new, win)


def _merge_kernel(y_ref, o_ref, w_ref, g_ref, x_ref, out_ref):
    half = y_ref.shape[1]
    m = _mm(y_ref[...], w_ref[0:half, :]) + _mm(o_ref[...], w_ref[half:, :])
    out_ref[...] = x_ref[...] + _rms(m, g_ref[...])


def _merge(y, o, w, g, x, tm):
    m, d = x.shape
    half = y.shape[1]
    return pl.pallas_call(
        _merge_kernel,
        grid=(m // tm,),
        in_specs=[pl.BlockSpec((tm, half), lambda i: (i, 0)), pl.BlockSpec((tm, half), lambda i: (i, 0)),
                  pl.BlockSpec(w.shape, lambda i: (0, 0)), pl.BlockSpec((1, d), lambda i: (0, 0)),
                  pl.BlockSpec((tm, d), lambda i: (i, 0))],
        out_specs=pl.BlockSpec((tm, d), lambda i: (i, 0)),
        out_shape=jax.ShapeDtypeStruct((m, d), F32),
        compiler_params=_cparams(("arbitrary",)),
        name="merge",
    )(y, o, w, g, x)


def _ffn_kernel(x_ref, pre_ref, gpre_ref, gpost_ref, wu_ref, wg_ref, cw_ref, cb_ref, wd_ref,
                out_ref, tail_ref, xn_s, gpad, carry, acc, *, shift, padr):
    j, f = pl.program_id(1), pl.program_id(2)
    nf = pl.num_programs(2)
    tm = x_ref.shape[1]

    @pl.when(f == 0)
    def _():
        xn_s[...] = _rms(x_ref[0], gpre_ref[...]).astype(BF16)
        acc[...] = jnp.zeros(acc.shape, F32)

    @pl.when(j == 0)
    def _():
        gpad[0:padr, :] = pre_ref[0]

    @pl.when(j > 0)
    def _():
        gpad[0:padr, :] = carry[f]

    xn = xn_s[...]
    u = _mm(xn, wu_ref[...])
    gpad[padr:padr + tm, :] = _mm(xn, wg_ref[...])
    gc = cb_ref[...]
    for k in range(3):
        gc = gc + gpad[pl.ds(padr - (2 - k) * shift, tm), :] * cw_ref[k:k + 1, :]
    tail = gpad[tm:tm + padr, :]
    carry[f] = tail
    tail_ref[0, 0] = tail
    acc[...] += _mm((jax.nn.gelu(gc) * u).astype(BF16), wd_ref[...])

    @pl.when(f == nf - 1)
    def _():
        out_ref[0] = x_ref[0] + _rms(acc[...], gpost_ref[...])


def _ffn(x, pre, gpre, gpost, wu, wg, cw, cb, wd, tm, tf, shift):
    b, t, d = x.shape
    dff = wu.shape[1]
    padr = pre.shape[1]
    nt, nf = t // tm, dff // tf
    return pl.pallas_call(
        functools.partial(_ffn_kernel, shift=shift, padr=padr),
        grid=(b, nt, nf),
        in_specs=[pl.BlockSpec((1, tm, d), lambda i, j, f: (i, j, 0)),
                  pl.BlockSpec((1, padr, tf), lambda i, j, f: (i, 0, f)),
                  pl.BlockSpec((1, d), lambda i, j, f: (0, 0)),
                  pl.BlockSpec((1, d), lambda i, j, f: (0, 0)),
                  pl.BlockSpec((d, tf), lambda i, j, f: (0, f)),
                  pl.BlockSpec((d, tf), lambda i, j, f: (0, f)),
                  pl.BlockSpec((3, tf), lambda i, j, f: (0, f)),
                  pl.BlockSpec((1, tf), lambda i, j, f: (0, f)),
                  pl.BlockSpec((tf, d), lambda i, j, f: (f, 0))],
        out_specs=[pl.BlockSpec((1, tm, d), lambda i, j, f: (i, j, 0)),
                   pl.BlockSpec((1, 1, padr, tf), lambda i, j, f: (i, j, 0, f))],
        out_shape=[jax.ShapeDtypeStruct((b, t, d), F32), jax.ShapeDtypeStruct((b, nt, padr, dff), F32)],
        scratch_shapes=[pltpu.VMEM((tm, d), BF16), pltpu.VMEM((padr + tm, tf), F32),
                        pltpu.VMEM((nf, padr, tf), F32), pltpu.VMEM((tm, d), F32)],
        compiler_params=_cparams(("arbitrary", "arbitrary", "arbitrary")),
        name="ffn",
    )(x, pre, gpre, gpost, wu, wg, cw, cb, wd)


def _rot_cols(w):
    d, n = w.shape
    w4 = w.reshape(d, n // HEAD_DIM, 2, HEAD_DIM // 2)
    return jnp.concatenate([-w4[:, :, 1], w4[:, :, 0]], axis=2).reshape(d, n)


def _rope_tables(pos):
    half = HEAD_DIM // 2
    freq = ROPE_THETA ** (-jnp.arange(half, dtype=F32) / half)
    ang = pos.astype(F32)[:, None] * freq[None, :]
    cos, sin = jnp.cos(ang), jnp.sin(ang)
    return jnp.concatenate([cos] * 4, axis=1), jnp.concatenate([sin] * 4, axis=1)


def _block_diag(w):
    n, c, d = w.shape
    return jnp.einsum('ncd,nm->ncmd', w, jnp.eye(n, dtype=w.dtype)).reshape(n * c, n * d)


def _layer_weights(l, p):
    d_model = p['w_in'].shape[1]
    w = p['w_in'][l]
    sizes = [512, 512, 512, 256, 256, 256, 3 * N_HEADS]
    cuts = np.cumsum([0] + sizes)
    xr, gate, q, kvc, kvs, kvw, gl = [w[:, cuts[i]:cuts[i + 1]] for i in range(7)]
    gl_pad = jnp.pad(gl, ((0, 0), (0, LANES - gl.shape[1])))
    wcat = jnp.concatenate([xr, gate, q, _rot_cols(q), kvc, kvs, _rot_cols(kvs[:, :KV_DIM]),
                            kvw, _rot_cols(kvw[:, :KV_DIM]), gl_pad], axis=1).astype(BF16)
    wrow = jnp.concatenate([xr, gate, q, _rot_cols(q), gl_pad], axis=1).astype(BF16)
    wt = jnp.concatenate([kvc, kvs, _rot_cols(kvs[:, :KV_DIM]), kvw, _rot_cols(kvw[:, :KV_DIM])], axis=1).T.astype(BF16)
    row = lambda v: v.reshape(1, -1)
    rg = (p['rg_conv_w'][l], row(p['rg_conv_b'][l]), _block_diag(p['rg_wa'][l]).astype(BF16),
          _block_diag(p['rg_wx'][l]).astype(BF16), row(p['rg_ba'][l]), row(p['rg_bx'][l]), row(p['rg_lambda'][l]))
    sel = lambda a, b: jnp.stack([a, a, b, b])
    eye2 = jnp.eye(N_KV, dtype=F32)
    w1 = jnp.stack([p['cmpk_w1'][l], p['cmpv_w1'][l]])
    w1bd = jnp.einsum('ksdf,he->skhdef', w1, eye2).reshape(2 * CMP_STRIDE, 2, KV_DIM, 2 * KV_DIM)
    pe = sel(p['cmpk_pe'][l], p['cmpv_pe'][l])
    pe = pe.transpose(1, 0, 2).reshape(2 * CMP_STRIDE, 4 * HEAD_DIM)
    b1 = sel(p['cmpk_b1'][l], p['cmpv_b1'][l]).reshape(1, -1)
    w2 = _block_diag(sel(p['cmpk_w2'][l], p['cmpv_w2'][l]))
    cmpw = (pe[:CMP_STRIDE], pe[CMP_STRIDE:], w1bd[:CMP_STRIDE].astype(BF16), w1bd[CMP_STRIDE:].astype(BF16),
            b1, w2.astype(BF16))
    dff = p['ffn_w_in'].shape[2] // 2
    ffn = (p['ffn_w_in'][l][:, :dff].astype(BF16), p['ffn_w_in'][l][:, dff:].astype(BF16),
           p['ffn_conv_w'][l], row(p['ffn_conv_b'][l]), p['ffn_w_down'][l].astype(BF16))
    norms = tuple(row(p[k][l]) for k in ('norm_mix_pre', 'norm_mix_post', 'norm_ffn_pre', 'norm_ffn_post'))
    return dict(wcat=wcat, wrow=wrow, wt=wt, rg=rg, cmp=cmpw, w_out=p['w_out'][l].astype(BF16), ffn=ffn, norms=norms)


def _pool_map(n_slc_pad, n_rows):
    j = np.arange(n_slc_pad)[:, None]
    m = np.arange(n_rows)[None, :]
    r = SLC_BLOCK // CMP_STRIDE
    return jnp.asarray(((m >= r * j) & (m <= r * j + r)).astype(np.float32))


def _heads_major(a, b, t):
    a5 = a.reshape(b, t, 2, N_KV, HEAD_DIM)
    return a5[:, :, 0].transpose(0, 2, 1, 3), a5[:, :, 1].transpose(0, 2, 1, 3)


def _rows_to_state(a_t):
    b, _, t = a_t.shape
    return a_t.reshape(b, 2, N_KV, HEAD_DIM, t).transpose(0, 4, 1, 2, 3)


def _prompt_mixer(hp, lw, tabs):
    b, t, d = hp.shape
    g_pre, g_post = lw['norms'][0], lw['norms'][1]
    xr, gate, qc, qr, gl, kvct, kvst, kvstb, kvwt, kvwtb = _proj_t(
        hp.reshape(b * t, d), g_pre, lw['wrow'], lw['wt'], *tabs, b, _div_tile(t, 512))
    c = xr.shape[1]
    y, htail = _rglru(xr.reshape(b, t, c), gate.reshape(b, t, c), jnp.zeros((b, 8, c), F32),
                      jnp.zeros((b, 1, c), F32), *lw['rg'], _div_tile(t, 512))
    npg = t // 128
    ident = jnp.arange(b * npg, dtype=jnp.int32).reshape(b, npg)
    kcs = _compress(kvct, ident, lw['cmp'], _div_tile(npg, 32), False)
    kck, kcv = _heads_major(kcs, b, t // CMP_STRIDE)
    kcv = jnp.concatenate([kcv, jnp.ones_like(kcv)], axis=-1)
    nqb = t // Q_BLOCK
    n_slc = -(-t // SLC_BLOCK)
    onehot_t = (jnp.arange(HEAD_DIM)[:, None] == jnp.arange(t)[None, :] // SLC_BLOCK).astype(BF16)
    regroup = lambda a, w: a.reshape(b, nqb, Q_BLOCK, N_KV, GROUP, w).transpose(0, 3, 1, 4, 2, 5).reshape(
        b, N_KV, nqb, GROUP * Q_BLOCK, w)
    qc5 = regroup(qc, HEAD_DIM)
    qr5 = regroup(qr, HEAD_DIM)
    gt5 = jnp.pad(regroup(gl[:, :3 * N_HEADS], 3), ((0, 0),) * 4 + ((0, 5),))
    o5 = _nsa_prompt(qc5, qr5, gt5, kck, kcv, kvstb, kvwtb, onehot_t, _pool_map(n_slc, t // CMP_STRIDE), t)
    o = o5.reshape(b, N_KV, nqb, GROUP, Q_BLOCK, HEAD_DIM).transpose(0, 2, 4, 1, 3, 5).reshape(b * t, N_HEADS * HEAD_DIM)
    hp = _merge(y.reshape(b * t, c), o, lw['w_out'], g_post, hp.reshape(b * t, d), _div_tile(b * t, 512))
    win_buf = min(WINDOW, t)
    st = (_rows_to_state(kvct), _rows_to_state(kvst), _rows_to_state(kvwt[:, :, t - win_buf:]), htail[:, 7],
          xr.reshape(b, t, c)[:, t - 3:])
    return hp.reshape(b, t, d), st


def _sample_mixer(hs, lw, s_cos, s_sin, cmp_pool, slc_pool, page_table, win_t, layer, past_len, rg_h0, rg_conv0):
    b, t, d = hs.shape
    g_pre, g_post = lw['norms'][0], lw['norms'][1]
    xr, gate, qc, qr, kvc, kvs, kvw, gl = _proj(hs.reshape(b * t, d), g_pre, lw['wcat'], s_cos, s_sin, b * t)
    c = xr.shape[1]
    padt = lambda a: jnp.pad(a.reshape(b, t, -1), ((0, 0), (0, 8 - t), (0, 0)))
    pre8 = jnp.pad(rg_conv0, ((0, 0), (5, 0), (0, 0)))
    y8, htail = _rglru(padt(xr), padt(gate), pre8, rg_h0.reshape(b, 1, c), *lw['rg'], 8)
    y = y8[:, :t].reshape(b * t, c)
    npg = page_table.shape[1]
    kcs = _compress(cmp_pool, page_table, lw['cmp'], _div_tile(npg, 32), True)
    nj = -(-(past_len + t) // SLC_BLOCK)
    pp = _div_tile(npg, 16)
    njp = -(-nj // LANES) * LANES
    rows = GROUP * N_KV * t

    def qrows(a):
        a5 = a.reshape(b, t, N_KV, GROUP, HEAD_DIM).transpose(0, 3, 2, 1, 4)
        z = jnp.zeros_like(a5[:, :, 0])
        top = jnp.concatenate([a5[:, :, 0], z], axis=-1)
        bot = jnp.concatenate([z, a5[:, :, 1]], axis=-1)
        return jnp.stack([top, bot], axis=2).reshape(b, rows, LANES)

    gts = gl[:, :3 * N_HEADS].reshape(b, t, N_KV, GROUP, 3).transpose(0, 3, 2, 1, 4).reshape(b, rows, 3)
    gts = jnp.pad(gts, ((0, 0), (0, 0), (0, 5)))
    expand = (jnp.arange(LANES)[:, None] == jnp.arange(pp * 128)[None, :] // SLC_BLOCK).astype(BF16)
    new = jnp.concatenate([kvs.reshape(b, t, 256), kvw.reshape(b, t, 256)], axis=-1)
    new = jnp.pad(new, ((0, 0), (0, 8 - t), (0, 0)))
    o32 = _nsa_sample(page_table, qrows(qc), qrows(qr), gts, kcs, slc_pool, expand, _pool_map(njp, kcs.shape[1]).T,
                      new, win_t, layer, pp, past_len, t, nj)
    o5 = o32.reshape(b, GROUP, N_KV, t, N_KV, HEAD_DIM)
    o = jnp.stack([o5[:, :, 0, :, 0], o5[:, :, 1, :, 1]], axis=1)
    o = o.transpose(0, 3, 1, 2, 4).reshape(b * t, N_HEADS * HEAD_DIM).astype(BF16)
    hs = _merge(y, o, lw['w_out'], g_post, hs.reshape(b * t, d), b * t)
    nbuf = win_t.shape[3]
    win_state_t = jnp.concatenate([win_t[layer], kvw.reshape(b, t, 256).transpose(0, 2, 1)], axis=2)[:, :, -nbuf:]
    st = (kvc.reshape(b, t, 2, N_KV, HEAD_DIM), kvs.reshape(b, t, 2, N_KV, HEAD_DIM), _rows_to_state(win_state_t),
          htail[:, (t - 1) % 8], jnp.concatenate([rg_conv0, xr.reshape(b, t, c)], axis=1)[:, -3:])
    return hs.reshape(b, t, d), st


def kernel(x_prompt, x_sample, cache_cmp_kv, cache_slc_kv, cache_win_kv, state_rg_h, state_rg_conv, state_ffn_conv, page_table, norm_mix_pre, norm_mix_post, norm_ffn_pre, norm_ffn_post, w_in, rg_conv_w, rg_conv_b, rg_wa, rg_ba, rg_wx, rg_bx, rg_lambda, cmpk_pe, cmpk_w1, cmpk_b1, cmpk_w2, cmpv_pe, cmpv_w1, cmpv_b1, cmpv_w2, w_out, ffn_w_in, ffn_conv_w, ffn_conv_b, ffn_w_down):
    params = dict(norm_mix_pre=norm_mix_pre, norm_mix_post=norm_mix_post, norm_ffn_pre=norm_ffn_pre,
                  norm_ffn_post=norm_ffn_post, w_in=w_in, rg_conv_w=rg_conv_w, rg_conv_b=rg_conv_b, rg_wa=rg_wa,
                  rg_ba=rg_ba, rg_wx=rg_wx, rg_bx=rg_bx, rg_lambda=rg_lambda, cmpk_pe=cmpk_pe, cmpk_w1=cmpk_w1,
                  cmpk_b1=cmpk_b1, cmpk_w2=cmpk_w2, cmpv_pe=cmpv_pe, cmpv_w1=cmpv_w1, cmpv_b1=cmpv_b1,
                  cmpv_w2=cmpv_w2, w_out=w_out, ffn_w_in=ffn_w_in, ffn_conv_w=ffn_conv_w, ffn_conv_b=ffn_conv_b,
                  ffn_w_down=ffn_w_down)
    depth = w_in.shape[0]
    bp, tp, d = x_prompt.shape
    bs, ts, _ = x_sample.shape
    past_len = page_table.shape[1] * cache_cmp_kv.shape[2]
    dff = ffn_w_in.shape[2] // 2
    p_cos, p_sin = _rope_tables(jnp.arange(tp))
    p_tabs = (p_cos, p_sin, p_cos.T, p_sin.T)
    s_cos, s_sin = _rope_tables(past_len + jnp.arange(ts))
    s_cos, s_sin = jnp.tile(s_cos, (bs, 1)), jnp.tile(s_sin, (bs, 1))
    n_pool, page = cache_cmp_kv.shape[1], cache_cmp_kv.shape[2]
    pages_t = lambda c: c.transpose(0, 1, 3, 4, 5, 2).reshape(depth * n_pool, 2 * KV_DIM, page)
    cmp_pool, slc_pool = pages_t(cache_cmp_kv), pages_t(cache_slc_kv)
    win_t = cache_win_kv.transpose(0, 1, 3, 4, 5, 2).reshape(depth, bs, 2 * KV_DIM, cache_win_kv.shape[2])
    hp, hs = x_prompt, x_sample
    states_p, states_s = [], []
    for l in range(depth):
        lw = _layer_weights(l, params)
        g_fpre, g_fpost = lw['norms'][2], lw['norms'][3]
        hp, st = _prompt_mixer(hp, lw, p_tabs)
        tmf = _div_tile(tp, 512)
        hp, tail = _ffn(hp, jnp.zeros((bp, 8, dff), F32), g_fpre, g_fpost, *lw['ffn'], tmf, _div_tile(dff, 1024), 1)
        states_p.append(st + (tail[:, -1, 6:],))
        hs, st = _sample_mixer(hs, lw, s_cos, s_sin, cmp_pool, slc_pool, page_table + l * n_pool, win_t, l,
                               past_len, state_rg_h[l], state_rg_conv[l])
        x_tm = hs.transpose(1, 0, 2).reshape(1, ts * bs, d)
        pre_tm = state_ffn_conv[l].transpose(1, 0, 2).reshape(1, 2 * bs, dff)
        out_tm, tail = _ffn(x_tm, pre_tm, g_fpre, g_fpost, *lw['ffn'], ts * bs, _div_tile(dff, 1024), bs)
        hs = out_tm.reshape(ts, bs, d).transpose(1, 0, 2)
        states_s.append(st + (tail.reshape(2, bs, dff).transpose(1, 0, 2),))
    p_cmp, p_slc, p_win, p_rgh, p_rgc, p_ffc = [jnp.stack(z) for z in zip(*states_p)]
    s_cmp, s_slc, s_win, s_rgh, s_rgc, s_ffc = [jnp.stack(z) for z in zip(*states_s)]
    return (hp, hs, p_cmp, s_cmp, p_slc, s_slc, p_win, s_win, p_rgh, s_rgh, p_rgc, s_rgc, p_ffc, s_ffc)
```

```python
import functools

import jax
import jax.numpy as jnp
import numpy as np
from jax import lax
from jax.experimental import pallas as pl
from jax.experimental.pallas import tpu as pltpu

F32 = jnp.float32
BF16 = jnp.bfloat16

HEAD_DIM = 64
N_KV = 2
GROUP = 4
N_HEADS = N_KV * GROUP
KV_DIM = N_KV * HEAD_DIM
CMP_STRIDE = 16
SLC_BLOCK = 64
N_SELECT = 16
N_LOCAL = 2
WINDOW = 512
Q_BLOCK = 128
RG_C = 8.0
ROPE_THETA = 10000.0
EPS = 1e-6

V7X_VMEM_BYTES = 64 * 1024 * 1024
VMEM_LIMIT = 48 * 1024 * 1024
SUBLANES = 8
LANES = 128

LOG2E = 1.4426950408889634
NEG = -2.0 ** 100
BIG = 1e30


def _cparams(sem):
    return pltpu.CompilerParams(dimension_semantics=sem, vmem_limit_bytes=VMEM_LIMIT)


def _div_tile(n, pref):
    t = min(n, pref)
    while n % t:
        t -= 1
    return t


def _rms(x, g):
    y = x * lax.rsqrt(jnp.mean(x * x, axis=-1, keepdims=True) + EPS)
    return y * g


def _sigmoid(x):
    return 0.5 * jnp.tanh(0.5 * x) + 0.5


def _nt(a, b, precision=None):
    return lax.dot_general(a, b, (((1,), (1,)), ((), ())), preferred_element_type=F32, precision=precision)


def _mm(a, b):
    return jnp.dot(a, b, preferred_element_type=F32)


C_XR, C_GATE, C_Q, C_QROT, C_KVC, C_KVS, C_KSROT, C_KVW, C_KWROT, C_GL, C_END = (
    0, 512, 1024, 1536, 2048, 2304, 2560, 2688, 2944, 3072, 3200)


def _proj_kernel(x_ref, g_ref, w_ref, cos_ref, sin_ref,
                 xr_ref, gate_ref, qc_ref, qr_ref, kvc_ref, kvs_ref, kvw_ref, gl_ref):
    xn = _rms(x_ref[...], g_ref[...]).astype(BF16)

    def mm(c0, c1):
        return _mm(xn, w_ref[:, c0:c1])

    cos = cos_ref[...]
    sin = sin_ref[...]
    xr_ref[...] = mm(C_XR, C_GATE)
    gate_ref[...] = mm(C_GATE, C_Q)
    q = mm(C_Q, C_QROT)
    qrot = mm(C_QROT, C_KVC)
    cos4 = jnp.concatenate([cos] * 4, axis=1)
    sin4 = jnp.concatenate([sin] * 4, axis=1)
    scale = HEAD_DIM ** -0.5
    qc_ref[...] = (q * scale).astype(BF16)
    qr_ref[...] = ((q * cos4 + qrot * sin4) * scale).astype(BF16)
    kvc_ref[...] = mm(C_KVC, C_KVS)
    kvs = mm(C_KVS, C_KSROT)
    ksrot = mm(C_KSROT, C_KVW)
    kvs_ref[:, 0:KV_DIM] = kvs[:, 0:KV_DIM] * cos + ksrot * sin
    kvs_ref[:, KV_DIM:] = kvs[:, KV_DIM:]
    kvw = mm(C_KVW, C_KWROT)
    kwrot = mm(C_KWROT, C_GL)
    kvw_ref[:, 0:KV_DIM] = kvw[:, 0:KV_DIM] * cos + kwrot * sin
    kvw_ref[:, KV_DIM:] = kvw[:, KV_DIM:]
    gl_ref[...] = jax.nn.sigmoid(mm(C_GL, C_END))


R_KVC, R_KVS, R_KSROT, R_KVW, R_KWROT, R_END = 0, 256, 512, 640, 896, 1024


def _proj_t_kernel(x_ref, g_ref, w_ref, wt_ref, cos_ref, sin_ref, cost_ref, sint_ref,
                   xr_ref, gate_ref, qc_ref, qr_ref, gl_ref, kvct_ref, kvst_ref, kvstb_ref, kvwt_ref, kvwtb_ref):
    xn = _rms(x_ref[...], g_ref[...]).astype(BF16)

    def mm(c0, c1):
        return _mm(xn, w_ref[:, c0:c1])

    def mt(r0, r1):
        return _nt(wt_ref[r0:r1, :], xn)

    xr_ref[...] = mm(0, 512)
    gate_ref[...] = mm(512, 1024)
    q = mm(1024, 1536)
    qrot = mm(1536, 2048)
    cos4 = jnp.concatenate([cos_ref[...]] * 4, axis=1)
    sin4 = jnp.concatenate([sin_ref[...]] * 4, axis=1)
    scale = HEAD_DIM ** -0.5 * LOG2E
    qc_ref[...] = (q * scale).astype(BF16)
    qr_ref[...] = ((q * cos4 + qrot * sin4) * scale).astype(BF16)
    gl_ref[...] = jax.nn.sigmoid(mm(2048, 2176))
    kvct_ref[0] = mt(R_KVC, R_KVS)
    cost, sint = cost_ref[...], sint_ref[...]
    for lo, rot, end, f_ref, b_ref in ((R_KVS, R_KSROT, R_KVW, kvst_ref, kvstb_ref),
                                       (R_KVW, R_KWROT, R_END, kvwt_ref, kvwtb_ref)):
        kv = mt(lo, rot)
        k = kv[0:KV_DIM] * cost + mt(rot, end) * sint
        f_ref[0, 0:KV_DIM, :] = k
        f_ref[0, KV_DIM:, :] = kv[KV_DIM:]
        b_ref[0, 0:KV_DIM, :] = k.astype(BF16)
        b_ref[0, KV_DIM:, :] = kv[KV_DIM:].astype(BF16)


def _proj_t(x2d, g, wrow, wt, cos, sin, cost, sint, b, tm):
    m, d = x2d.shape
    t = m // b
    nper = t // tm
    row = lambda w: pl.BlockSpec((tm, w), lambda i: (i, 0))
    tab = pl.BlockSpec((tm, LANES), lambda i: (i % nper, 0))
    tabt = pl.BlockSpec((KV_DIM, tm), lambda i: (0, i % nper))
    tr = pl.BlockSpec((1, 2 * KV_DIM, tm), lambda i: (i // nper, 0, i % nper))
    full = lambda a: pl.BlockSpec(a.shape, lambda i: (0,) * a.ndim)
    widths = (512, 512, 512, 512, 128)
    dtypes = (F32, F32, BF16, BF16, F32)
    tdt = (F32, F32, BF16, F32, BF16)
    return pl.pallas_call(
        _proj_t_kernel,
        grid=(m // tm,),
        in_specs=[row(d), full(g), full(wrow), full(wt), tab, tab, tabt, tabt],
        out_specs=[row(w) for w in widths] + [tr] * 5,
        out_shape=[jax.ShapeDtypeStruct((m, w), dt) for w, dt in zip(widths, dtypes)]
        + [jax.ShapeDtypeStruct((b, 2 * KV_DIM, t), dt) for dt in tdt],
        compiler_params=_cparams(("arbitrary",)),
        name="proj_t",
    )(x2d, g, wrow, wt, cos, sin, cost, sint)


def _proj(x2d, g, wcat, cos, sin, tm):
    m, d = x2d.shape
    nper = cos.shape[0] // tm
    row = lambda w: pl.BlockSpec((tm, w), lambda i: (i, 0))
    tab = pl.BlockSpec((tm, LANES), lambda i: (i % nper, 0))
    widths = (512, 512, 512, 512, 256, 256, 256, 128)
    dtypes = (F32, F32, BF16, BF16, F32, F32, F32, F32)
    return pl.pallas_call(
        _proj_kernel,
        grid=(m // tm,),
        in_specs=[row(d), pl.BlockSpec((1, d), lambda i: (0, 0)),
                  pl.BlockSpec(wcat.shape, lambda i: (0, 0)), tab, tab],
        out_specs=[row(w) for w in widths],
        out_shape=[jax.ShapeDtypeStruct((m, w), dt) for w, dt in zip(widths, dtypes)],
        compiler_params=_cparams(("arbitrary",)),
        name="proj",
    )(x2d, g, wcat, cos, sin)


def _rglru_kernel(xr_ref, gate_ref, pre_ref, h0_ref, cw_ref, cb_ref, wa_ref, wx_ref, ba_ref, bx_ref, lam_ref,
                  y_ref, htail_ref, xpad, hc, a_s, b_s):
    tt = xr_ref.shape[1]
    j = pl.program_id(1)

    @pl.when(j == 0)
    def _():
        xpad[0:8, :] = pre_ref[0]
        hc[...] = jnp.broadcast_to(h0_ref[0], hc.shape)

    @pl.when(j > 0)
    def _():
        xpad[0:8, :] = xpad[tt:tt + 8, :]

    xpad[8:8 + tt, :] = xr_ref[0]
    xc = cb_ref[...]
    for k in range(4):
        xc = xc + xpad[pl.ds(5 + k, tt), :] * cw_ref[k:k + 1, :]
    xb = xc.astype(BF16)
    r = _sigmoid(_mm(xb, wa_ref[...]) + ba_ref[...])
    i = _sigmoid(_mm(xb, wx_ref[...]) + bx_ref[...])
    log_a = RG_C * r * jax.nn.log_sigmoid(lam_ref[...])
    a_s[...] = jnp.exp(log_a)
    t = jnp.tanh(log_a)
    b_s[...] = jnp.sqrt(-2.0 * t / (1.0 - t)) * (i * xc)

    row = lax.broadcasted_iota(jnp.int32, (8, a_s.shape[1]), 0)

    def body(gi, h):
        r0 = pl.multiple_of(gi * 8, 8)
        av = a_s[pl.ds(r0, 8), :]
        bv = b_s[pl.ds(r0, 8), :]
        for s in (1, 2, 4):
            keep = row >= s
            a_sh = pltpu.roll(av, s, 0)
            b_sh = pltpu.roll(bv, s, 0)
            bv = jnp.where(keep, av * b_sh + bv, bv)
            av = jnp.where(keep, av * a_sh, av)
        hs = av * h + bv
        b_s[pl.ds(r0, 8), :] = hs
        return jnp.broadcast_to(hs[7:8, :], hs.shape)

    h = lax.fori_loop(0, tt // 8, body, hc[...])
    hc[...] = h
    htail_ref[0] = b_s[tt - 8:tt, :]
    y_ref[0] = (b_s[...] * jax.nn.gelu(gate_ref[0])).astype(y_ref.dtype)


def _rglru(xr, gate, pre8, h0, cw, cb, wa, wx, ba, bx, lam, tt):
    b, t, c = xr.shape
    full = lambda a: pl.BlockSpec(a.shape, lambda i, j: (0,) * a.ndim)
    return pl.pallas_call(
        _rglru_kernel,
        grid=(b, t // tt),
        in_specs=[pl.BlockSpec((1, tt, c), lambda i, j: (i, j, 0)),
                  pl.BlockSpec((1, tt, c), lambda i, j: (i, j, 0)),
                  pl.BlockSpec((1, 8, c), lambda i, j: (i, 0, 0)),
                  pl.BlockSpec((1, 1, c), lambda i, j: (i, 0, 0)),
                  full(cw), full(cb), full(wa), full(wx), full(ba), full(bx), full(lam)],
        out_specs=[pl.BlockSpec((1, tt, c), lambda i, j: (i, j, 0)),
                   pl.BlockSpec((1, 8, c), lambda i, j: (i, 0, 0))],
        out_shape=[jax.ShapeDtypeStruct((b, t, c), BF16), jax.ShapeDtypeStruct((b, 8, c), F32)],
        scratch_shapes=[pltpu.VMEM((tt + 8, c), F32), pltpu.VMEM((8, c), F32),
                        pltpu.VMEM((tt, c), F32), pltpu.VMEM((tt, c), F32)],
        compiler_params=_cparams(("arbitrary", "arbitrary")),
        name="rglru",
    )(xr, gate, pre8, h0, cw, cb, wa, wx, ba, bx, lam)


def _page_copy(pt_ref, src_ref, buf, sem, idx, slot, p, seq_pages):
    page = pt_ref[idx]
    if seq_pages is None:
        src = src_ref.at[page]
    else:
        src = src_ref.at[page // seq_pages, :, pl.ds(pl.multiple_of((page % seq_pages) * 128, 128), 128)]
    return pltpu.make_async_copy(src, buf.at[slot, p], sem.at[slot])


def _cmp_kernel(pt_ref, pool_ref, pea_ref, peb_ref, w1a_ref, w1b_ref, b1_ref, w2_ref, out_ref,
                buf, sem, xbuf, ua_s, *, pp, seq_pages):
    nb, nt = pl.num_programs(0), pl.num_programs(1)
    j = pl.program_id(1)
    step = pl.program_id(0) * nt + j
    slot = step % 2
    m = pp * 8

    def fetch(st, sl):
        for p in range(pp):
            _page_copy(pt_ref, pool_ref, buf, sem, st * pp + p, sl, p, seq_pages).start()

    @pl.when(step == 0)
    def _():
        fetch(step, slot)

    @pl.when(step + 1 < nb * nt)
    def _():
        fetch(step + 1, 1 - slot)

    for p in range(pp):
        _page_copy(pt_ref, pool_ref, buf, sem, step * pp + p, slot, p, seq_pages).wait()

    def to_rows(p, carry):
        r0 = pl.multiple_of(p * 128, 128)
        for kv in range(2):
            xbuf[kv, pl.ds(r0, 128), :] = buf[slot, p, kv * KV_DIM:(kv + 1) * KV_DIM, :].T
        return carry

    lax.fori_loop(0, pp, to_rows, 0, unroll=min(pp, 4))

    acc_a = [jnp.zeros((m, 256), F32)] * 2
    acc_b = [jnp.zeros((m, 256), F32)] * 2
    for s in range(0, CMP_STRIDE, 2):
        for kv in range(2):
            x0 = xbuf[kv, pl.ds(s, m, stride=CMP_STRIDE), :]
            x1 = xbuf[kv, pl.ds(s + 1, m, stride=CMP_STRIDE), :]
            lanes = slice(kv * KV_DIM, (kv + 1) * KV_DIM)

            def pair(pe_ref):
                return jnp.concatenate([(x0 + pe_ref[s:s + 1, lanes]).astype(BF16),
                                        (x1 + pe_ref[s + 1:s + 2, lanes]).astype(BF16)], axis=1)

            acc_a[kv] = acc_a[kv] + _mm(pair(pea_ref), w1a_ref[s // 2, kv])
            acc_b[kv] = acc_b[kv] + _mm(pair(peb_ref), w1b_ref[s // 2, kv])
    acc_a = jnp.concatenate(acc_a, axis=1)
    acc_b = jnp.concatenate(acc_b, axis=1)

    @pl.when(j == 0)
    def _():
        ua_s[0:8, :] = jnp.zeros((8, 512), F32)

    @pl.when(j > 0)
    def _():
        ua_s[0:8, :] = ua_s[m:m + 8, :]

    ua_s[8:8 + m, :] = acc_a
    hid = jax.nn.gelu(ua_s[pl.ds(7, m), :] + acc_b + b1_ref[...])
    out_ref[0] = _mm(hid.astype(BF16), w2_ref[...]).astype(out_ref.dtype)


def _compress(pool, page_table, cw, pp, paged):
    b, npg = page_table.shape
    nt = npg // pp
    m = pp * 8
    full = lambda a: pl.BlockSpec(a.shape, lambda i, j, pt: (0,) * a.ndim)
    pea, peb, w1a, w1b, b1, w2 = cw
    gs = pltpu.PrefetchScalarGridSpec(
        num_scalar_prefetch=1,
        grid=(b, nt),
        in_specs=[pl.BlockSpec(memory_space=pl.ANY), full(pea), full(peb), full(w1a), full(w1b), full(b1), full(w2)],
        out_specs=pl.BlockSpec((1, m, 256), lambda i, j, pt: (i, j, 0)),
        scratch_shapes=[pltpu.VMEM((2, pp, 2 * KV_DIM, 128), F32), pltpu.SemaphoreType.DMA((2,)),
                        pltpu.VMEM((2, pp * 128, KV_DIM), F32), pltpu.VMEM((m + 8, 512), F32)],
    )
    return pl.pallas_call(
        functools.partial(_cmp_kernel, pp=pp, seq_pages=None if paged else npg),
        grid_spec=gs,
        out_shape=jax.ShapeDtypeStruct((b, nt * m, 256), BF16),
        compiler_params=_cparams(("arbitrary", "arbitrary")),
        name="compress",
    )(page_table.reshape(-1), pool, pea, peb, w1a, w1b, b1, w2)


def _softmax_rows(s, mask):
    sm = jnp.where(mask, s, -BIG)
    mx = jnp.max(sm, axis=-1, keepdims=True)
    e = jnp.where(mask, jnp.exp(sm - mx), 0.0)
    return e / jnp.maximum(jnp.sum(e, axis=-1, keepdims=True), 1e-30)


def _nsa_prompt_kernel(qc_ref, qr_ref, g_ref, kck_ref, kcv_ref, kvs_ref, kvw_ref, oh_ref, pool_ref,
                       o_ref, sc_s, rank_s, *, t_len, kc_len, wl):
    qb = pl.program_id(1)
    start = qb * Q_BLOCK
    nc = kck_ref.shape[2]
    nj = pool_ref.shape[0]
    rows = GROUP * Q_BLOCK
    qpos = start + (lax.broadcasted_iota(jnp.int32, (rows, 1), 0) & (Q_BLOCK - 1))

    heads = range(N_KV)
    k_rows = [slice(h * HEAD_DIM, (h + 1) * HEAD_DIM) for h in heads]
    v_rows = [slice(KV_DIM + h * HEAD_DIM, KV_DIM + (h + 1) * HEAD_DIM) for h in heads]
    n_chunks = (start + Q_BLOCK + kc_len - 1) // kc_len
    n_full = n_chunks - 1
    k_diag = pl.multiple_of(n_full * kc_len, kc_len)
    base = pl.multiple_of(jnp.maximum(start + Q_BLOCK - wl, 0), Q_BLOCK)

    qpos1 = start + lax.broadcasted_iota(jnp.int32, (Q_BLOCK, 1), 0)
    mrow = lax.broadcasted_iota(jnp.int32, (Q_BLOCK, nc), 1)
    cbias = jnp.where((mrow >= 1) & (mrow * CMP_STRIDE + (CMP_STRIDE - 1) <= qpos1), 0.0, NEG)
    dbias = jnp.where(k_diag + lax.broadcasted_iota(jnp.int32, (Q_BLOCK, kc_len), 1) <= qpos1, 0.0, NEG)
    dist = qpos1 - (base + lax.broadcasted_iota(jnp.int32, (Q_BLOCK, wl), 1))
    wbias = jnp.where((dist >= 0) & (dist < WINDOW), 0.0, NEG)
    any_cmp = jnp.where(qpos >= 2 * CMP_STRIDE - 1, 1.0, 0.0)

    def add_bias(s, bias):
        return (s.reshape(GROUP, Q_BLOCK, s.shape[1]) + bias[None]).reshape(s.shape)

    def with_ones(vt):
        return jnp.concatenate([vt, jnp.ones(vt.shape, vt.dtype)], axis=0)

    def head_rows(ref, h):
        return jnp.concatenate([ref[:, (h * GROUP + g) * HEAD_DIM:(h * GROUP + g + 1) * HEAD_DIM]
                                for g in range(GROUP)], axis=0)

    def gate_rows(h, branch):
        gl = g_ref[...]
        cols = [(h * GROUP + g) * 3 + branch for g in range(GROUP)]
        return jnp.concatenate([gl[:, c:c + 1] for c in cols], axis=0)

    r_cmp, inv_cmp = [], []
    jidx = lax.broadcasted_iota(jnp.int32, (nj, Q_BLOCK), 0)
    lag = ((start + lax.broadcasted_iota(jnp.int32, (nj, Q_BLOCK), 1)) >> 6) - jidx
    forced = (jidx == 0) | ((lag >= 0) & (lag < N_LOCAL))
    for h in heads:
        s = add_bias(_nt(head_rows(qc_ref, h), kck_ref[0, h]), cbias)
        e = jnp.exp2(s - jnp.max(s, axis=-1, keepdims=True))
        r = _mm(e.astype(BF16), kcv_ref[0, h])
        inv = any_cmp / jnp.maximum(r[:, HEAD_DIM:HEAD_DIM + 1], 1e-30)
        p = e * inv
        psum = p[0:Q_BLOCK] + p[Q_BLOCK:2 * Q_BLOCK] + p[2 * Q_BLOCK:3 * Q_BLOCK] + p[3 * Q_BLOCK:]
        score_t = _nt(pool_ref[...], psum, precision=lax.Precision.HIGHEST)
        sc_s[h] = jnp.where(lag >= 0, jnp.where(forced, BIG, score_t), -BIG)
        rank_s[h] = jnp.zeros((nj, Q_BLOCK), F32)
        r_cmp.append(r)
        inv_cmp.append(inv)

    j_last = (start + Q_BLOCK - 1) >> 6
    sub = lax.broadcasted_iota(jnp.int32, (SUBLANES, Q_BLOCK), 0)
    n_grp = nj // SUBLANES
    for gi in range(n_grp):
        @pl.when(gi * SUBLANES <= j_last)
        def _():
            for h in heads:
                scv = sc_s[h]
                cnt = [jnp.zeros((SUBLANES, Q_BLOCK), F32)] * n_grp
                for ii in range(SUBLANES):
                    si = sc_s[h, gi * SUBLANES + ii:gi * SUBLANES + ii + 1, :]
                    for r in range(n_grp):
                        blk = scv[r * SUBLANES:(r + 1) * SUBLANES]
                        if r == gi:
                            beat = jnp.where(sub > ii, jnp.where(si >= blk, 1.0, 0.0), jnp.where(si > blk, 1.0, 0.0))
                        else:
                            beat = jnp.where(si >= blk, 1.0, 0.0) if r > gi else jnp.where(si > blk, 1.0, 0.0)
                        cnt[r] = cnt[r] + beat
                rank_s[h] = rank_s[h] + jnp.concatenate(cnt, axis=0)

    q4r = [head_rows(qr_ref, h) for h in heads]
    qaug = []
    for h in heads:
        bias_t = jnp.where(rank_s[h] < min(N_SELECT, nj), 0.0, NEG)
        pad_t = jnp.concatenate([jnp.zeros((HEAD_DIM, Q_BLOCK), F32), bias_t]
                                + ([jnp.zeros((HEAD_DIM - nj, Q_BLOCK), F32)] if nj < HEAD_DIM else []), axis=0)
        bias4 = jnp.concatenate([pad_t.T[:, HEAD_DIM:]] * GROUP, axis=0)
        qaug.append(jnp.concatenate([q4r[h].astype(F32), bias4], axis=1).astype(BF16))

    def chunk(k0, state, diagonal):
        out = []
        for h in heads:
            m_i, acc = state[h]
            kaug = jnp.concatenate([kvs_ref[0, k_rows[h], pl.ds(k0, kc_len)], oh_ref[:, pl.ds(k0, kc_len)]], axis=0)
            sc = _mm(qaug[h], kaug)
            if diagonal:
                sc = add_bias(sc, dbias)
            m_n = jnp.maximum(m_i, jnp.max(sc, axis=-1, keepdims=True))
            pe = jnp.exp2(sc - m_n).astype(BF16)
            acc = jnp.exp2(m_i - m_n) * acc + _nt(pe, with_ones(kvs_ref[0, v_rows[h], pl.ds(k0, kc_len)]))
            out.append((m_n, acc))
        return tuple(out)

    init = tuple((jnp.full((rows, 1), -3e38, F32), jnp.zeros((rows, LANES), F32)) for _ in heads)
    state = lax.fori_loop(0, n_full, lambda c, st: chunk(pl.multiple_of(c * kc_len, kc_len), st, False), init)
    state = chunk(k_diag, state, True)

    for h in heads:
        sw = add_bias(_mm(q4r[h], kvw_ref[0, k_rows[h], pl.ds(base, wl)]), wbias)
        ew = jnp.exp2(sw - jnp.max(sw, axis=-1, keepdims=True)).astype(BF16)
        r_win = _nt(ew, with_ones(kvw_ref[0, v_rows[h], pl.ds(base, wl)]))
        r_slc = state[h][1]
        den = lambda r: jnp.maximum(r[:, HEAD_DIM:HEAD_DIM + 1], 1e-30)
        o = (r_cmp[h] * (gate_rows(h, 0) * inv_cmp[h]) + r_slc * (gate_rows(h, 1) / den(r_slc))
             + r_win * (gate_rows(h, 2) / den(r_win)))
        for g in range(GROUP):
            lanes = slice((h * GROUP + g) * HEAD_DIM, (h * GROUP + g + 1) * HEAD_DIM)
            o_ref[:, lanes] = o[g * Q_BLOCK:(g + 1) * Q_BLOCK, 0:HEAD_DIM].astype(o_ref.dtype)


def _nsa_prompt(qc, qr, gt, kck, kcv, kvst, kvwt, onehot_t, pool_t, t_len):
    b = kck.shape[0]
    nqb = t_len // Q_BLOCK
    kc_len = _div_tile(t_len, 512)
    wl = min(WINDOW + Q_BLOCK, t_len)
    qspec = lambda w: pl.BlockSpec((Q_BLOCK, w), lambda i, j: (i * nqb + j, 0))
    seq = lambda a: pl.BlockSpec((1,) + a.shape[1:], lambda i, j: (i,) + (0,) * (a.ndim - 1))
    return pl.pallas_call(
        functools.partial(_nsa_prompt_kernel, t_len=t_len, kc_len=kc_len, wl=wl),
        grid=(b, nqb),
        in_specs=[qspec(qc.shape[1]), qspec(qr.shape[1]), qspec(gt.shape[1]), seq(kck), seq(kcv), seq(kvst), seq(kvwt),
                  pl.BlockSpec(onehot_t.shape, lambda i, j: (0, 0)), pl.BlockSpec(pool_t.shape, lambda i, j: (0, 0))],
        out_specs=qspec(qc.shape[1]),
        out_shape=jax.ShapeDtypeStruct(qc.shape, BF16),
        scratch_shapes=[pltpu.VMEM((N_KV,) + pool_t.shape[:1] + (Q_BLOCK,), F32)] * 2,
        compiler_params=_cparams(("arbitrary", "arbitrary")),
        name="nsa_prompt",
    )(qc, qr, gt, kck, kcv, kvst, kvwt, onehot_t, pool_t)


def _nsa_sample_kernel(pt_ref, qc_ref, qr_ref, g_ref, kc_ref, pool_ref, exp_ref, pmap_ref, new_ref, win_ref,
                       o_ref, buf, sem, bias_s, m_s, l_s, acc_s, ocmp_s, *, pp, past_len, n_tok, nj):
    nb, nt = pl.num_programs(0), pl.num_programs(1)
    j = pl.program_id(1)
    step = pl.program_id(0) * nt + j
    slot = step % 2
    rows = GROUP * N_KV * n_tok
    tk = pp * 128
    bpt = tk // SLC_BLOCK

    def fetch(st, sl):
        for p in range(pp):
            _page_copy(pt_ref, pool_ref, buf, sem, st * pp + p, sl, p, None).start()

    @pl.when(step == 0)
    def _():
        fetch(step, slot)

    @pl.when(step + 1 < nb * nt)
    def _():
        fetch(step + 1, 1 - slot)

    rid = lax.broadcasted_iota(jnp.int32, (rows, 1), 0)
    tok = rid % n_tok
    qpos = past_len + tok

    @pl.when(j == 0)
    def _():
        kc = kc_ref[0]
        nc = kc.shape[0]
        s = _nt(qc_ref[0], kc[:, 0:KV_DIM])
        mrow = lax.broadcasted_iota(jnp.int32, (rows, nc), 1)
        valid = (mrow >= 1) & (mrow * CMP_STRIDE + (CMP_STRIDE - 1) <= qpos)
        p = _softmax_rows(s, valid)
        ocmp_s[...] = _mm(p.astype(BF16), kc[:, KV_DIM:])
        r8 = N_KV * n_tok
        psum = p[0:r8] + p[r8:2 * r8] + p[2 * r8:3 * r8] + p[3 * r8:]
        score = jnp.dot(psum, pmap_ref[...], precision=lax.Precision.HIGHEST,
                        preferred_element_type=F32)
        njp = score.shape[1]
        jidx = lax.broadcasted_iota(jnp.int32, (r8, njp), 1)
        qp_r = past_len + lax.broadcasted_iota(jnp.int32, (r8, njp), 0) % n_tok
        lag = (qp_r >> 6) - jidx
        forced = (jidx == 0) | ((lag >= 0) & (lag < N_LOCAL))
        sc = jnp.where((lag >= 0) & (jidx < nj), jnp.where(forced, BIG, score), -BIG)
        lane = lax.broadcasted_iota(jnp.int32, (r8, LANES), 1)
        cnt = [jnp.zeros((r8, LANES), F32)] * (njp // LANES)
        for i in range(nj):
            si = sc[:, i:i + 1]
            for v in range(njp // LANES):
                blk = sc[:, v * LANES:(v + 1) * LANES]
                if v * LANES > i:
                    beat = jnp.where(si >= blk, 1.0, 0.0)
                elif (v + 1) * LANES <= i:
                    beat = jnp.where(si > blk, 1.0, 0.0)
                else:
                    beat = jnp.where(lane + v * LANES > i, jnp.where(si >= blk, 1.0, 0.0), jnp.where(si > blk, 1.0, 0.0))
                cnt[v] = cnt[v] + beat
        bias = jnp.where(jnp.concatenate(cnt, axis=1) < min(N_SELECT, nj), 0.0, NEG)
        for tj in range(njp // bpt):
            b8 = pltpu.roll(bias, (njp - bpt * tj) % njp, 1)[:, 0:LANES]
            bias_s[tj] = jnp.concatenate([b8] * GROUP, axis=0)
        m_s[...] = jnp.full(m_s.shape, -3e38, F32)
        l_s[...] = jnp.zeros(l_s.shape, F32)
        acc_s[...] = jnp.zeros(acc_s.shape, F32)

    for p in range(pp):
        _page_copy(pt_ref, pool_ref, buf, sem, step * pp + p, slot, p, None).wait()

    def online(sc, pv):
        m_i = m_s[...]
        m_n = jnp.maximum(m_i, jnp.max(sc, axis=-1, keepdims=True))
        alpha = jnp.exp(m_i - m_n)
        pe = jnp.exp(sc - m_n)
        l_s[...] = alpha * l_s[...] + jnp.sum(pe, axis=-1, keepdims=True)
        acc_s[...] = alpha * acc_s[...] + pv(pe.astype(BF16))
        m_s[...] = m_n

    qr = qr_ref[0]
    sc = jnp.concatenate([_mm(qr, buf[slot, p, 0:KV_DIM, :].astype(BF16)) for p in range(pp)], axis=1)
    sc = sc + _mm(bias_s[j].astype(BF16), exp_ref[...])

    def pv_pages(pe):
        o = _nt(pe[:, 0:128], buf[slot, 0, KV_DIM:, :].astype(BF16))
        for p in range(1, pp):
            o = o + _nt(pe[:, p * 128:(p + 1) * 128], buf[slot, p, KV_DIM:, :].astype(BF16))
        return o

    online(sc, pv_pages)

    @pl.when(j == nt - 1)
    def _():
        new = new_ref[0]
        nk = new.shape[0]
        kidx = lax.broadcasted_iota(jnp.int32, (rows, nk), 1)
        last_bias = bias_s[(past_len // SLC_BLOCK) // bpt][:, (past_len // SLC_BLOCK) % bpt:(past_len // SLC_BLOCK) % bpt + 1]
        scn = _nt(qr, new[:, 0:KV_DIM].astype(BF16)) + last_bias
        scn = jnp.where((kidx <= tok) & (kidx < n_tok), scn, NEG)
        online(scn, lambda pe: _mm(pe, new[:, KV_DIM:2 * KV_DIM].astype(BF16)))
        o_slc = acc_s[...] / jnp.maximum(l_s[...], 1e-30)

        nbuf = win_ref.shape[3]
        s1 = _mm(qr, win_ref[0, 0, 0:KV_DIM, :].astype(BF16))
        d1 = tok + nbuf - lax.broadcasted_iota(jnp.int32, (rows, nbuf), 1)
        ok1 = (d1 >= 0) & (d1 < WINDOW)
        s2 = _nt(qr, new[:, 2 * KV_DIM:3 * KV_DIM].astype(BF16))
        d2 = tok - kidx
        ok2 = (d2 >= 0) & (d2 < WINDOW) & (kidx < n_tok)
        s1 = jnp.where(ok1, s1, -BIG)
        s2 = jnp.where(ok2, s2, -BIG)
        mx = jnp.maximum(jnp.max(s1, axis=-1, keepdims=True), jnp.max(s2, axis=-1, keepdims=True))
        e1 = jnp.where(ok1, jnp.exp(s1 - mx), 0.0)
        e2 = jnp.where(ok2, jnp.exp(s2 - mx), 0.0)
        den = jnp.maximum(jnp.sum(e1, axis=-1, keepdims=True) + jnp.sum(e2, axis=-1, keepdims=True), 1e-30)
        o_win = (_nt(e1.astype(BF16), win_ref[0, 0, KV_DIM:, :].astype(BF16))
                 + _mm(e2.astype(BF16), new[:, 3 * KV_DIM:].astype(BF16))) / den
        gt = g_ref[0]
        o_ref[0] = gt[:, 0:1] * ocmp_s[...] + gt[:, 1:2] * o_slc + gt[:, 2:3] * o_win


def _nsa_sample(page_table, qc, qr, gt, kc, pool, expand, pmap, new, win, layer, pp, past_len, n_tok, nj):
    b, npg = page_table.shape
    nt = npg // pp
    rows = GROUP * N_KV * n_tok
    tk = pp * 128
    bpt = tk // SLC_BLOCK
    njp = pmap.shape[1]
    seq = lambda a: pl.BlockSpec((1,) + a.shape[1:], lambda i, j, pt: (i,) + (0,) * (a.ndim - 1))
    full = lambda a: pl.BlockSpec(a.shape, lambda i, j, pt: (0,) * a.ndim)
    gs = pltpu.PrefetchScalarGridSpec(
        num_scalar_prefetch=1,
        grid=(b, nt),
        in_specs=[seq(qc), seq(qr), seq(gt), seq(kc), pl.BlockSpec(memory_space=pl.ANY), full(expand), full(pmap),
                  seq(new), pl.BlockSpec((1, 1) + win.shape[2:], lambda i, j, pt: (layer, i, 0, 0))],
        out_specs=pl.BlockSpec((1, rows, LANES), lambda i, j, pt: (i, 0, 0)),
        scratch_shapes=[pltpu.VMEM((2, pp, 2 * KV_DIM, 128), F32), pltpu.SemaphoreType.DMA((2,)),
                        pltpu.VMEM((njp // bpt, rows, LANES), F32),
                        pltpu.VMEM((rows, 1), F32), pltpu.VMEM((rows, 1), F32),
                        pltpu.VMEM((rows, LANES), F32), pltpu.VMEM((rows, LANES), F32)],
    )
    return pl.pallas_call(
        functools.partial(_nsa_sample_kernel, pp=pp, past_len=past_len, n_tok=n_tok, nj=nj),
        grid_spec=gs,
        out_shape=jax.ShapeDtypeStruct((b, rows, LANES), F32),
        compiler_params=_cparams(("arbitrary", "arbitrary")),
        name="nsa_sample",
    )(page_table.reshape(-1), qc, qr, gt, kc, pool, expand, pmap, new, win)


def _merge_kernel(y_ref, o_ref, w_ref, g_ref, x_ref, out_ref):
    half = y_ref.shape[1]
    m = _mm(y_ref[...], w_ref[0:half, :]) + _mm(o_ref[...], w_ref[half:, :])
    out_ref[...] = x_ref[...] + _rms(m, g_ref[...])


def _merge(y, o, w, g, x, tm):
    m, d = x.shape
    half = y.shape[1]
    return pl.pallas_call(
        _merge_kernel,
        grid=(m // tm,),
        in_specs=[pl.BlockSpec((tm, half), lambda i: (i, 0)), pl.BlockSpec((tm, half), lambda i: (i, 0)),
                  pl.BlockSpec(w.shape, lambda i: (0, 0)), pl.BlockSpec((1, d), lambda i: (0, 0)),
                  pl.BlockSpec((tm, d), lambda i: (i, 0))],
        out_specs=pl.BlockSpec((tm, d), lambda i: (i, 0)),
        out_shape=jax.ShapeDtypeStruct((m, d), F32),
        compiler_params=_cparams(("arbitrary",)),
        name="merge",
    )(y, o, w, g, x)


def _ffn_kernel(x_ref, pre_ref, gpre_ref, gpost_ref, wu_ref, wg_ref, cw_ref, cb_ref, wd_ref,
                out_ref, tail_ref, xn_s, gpad, carry, acc, *, shift, padr):
    j, f = pl.program_id(1), pl.program_id(2)
    nf = pl.num_programs(2)
    tm = x_ref.shape[1]

    @pl.when(f == 0)
    def _():
        xn_s[...] = _rms(x_ref[0], gpre_ref[...]).astype(BF16)
        acc[...] = jnp.zeros(acc.shape, F32)

    @pl.when(j == 0)
    def _():
        gpad[0:padr, :] = pre_ref[0]

    @pl.when(j > 0)
    def _():
        gpad[0:padr, :] = carry[f]

    xn = xn_s[...]
    u = _mm(xn, wu_ref[...])
    gpad[padr:padr + tm, :] = _mm(xn, wg_ref[...])
    gc = cb_ref[...]
    for k in range(3):
        gc = gc + gpad[pl.ds(padr - (2 - k) * shift, tm), :] * cw_ref[k:k + 1, :]
    tail = gpad[tm:tm + padr, :]
    carry[f] = tail
    tail_ref[0, 0] = tail
    acc[...] += _mm((jax.nn.gelu(gc) * u).astype(BF16), wd_ref[...])

    @pl.when(f == nf - 1)
    def _():
        out_ref[0] = x_ref[0] + _rms(acc[...], gpost_ref[...])


def _ffn(x, pre, gpre, gpost, wu, wg, cw, cb, wd, tm, tf, shift):
    b, t, d = x.shape
    dff = wu.shape[1]
    padr = pre.shape[1]
    nt, nf = t // tm, dff // tf
    return pl.pallas_call(
        functools.partial(_ffn_kernel, shift=shift, padr=padr),
        grid=(b, nt, nf),
        in_specs=[pl.BlockSpec((1, tm, d), lambda i, j, f: (i, j, 0)),
                  pl.BlockSpec((1, padr, tf), lambda i, j, f: (i, 0, f)),
                  pl.BlockSpec((1, d), lambda i, j, f: (0, 0)),
                  pl.BlockSpec((1, d), lambda i, j, f: (0, 0)),
                  pl.BlockSpec((d, tf), lambda i, j, f: (0, f)),
                  pl.BlockSpec((d, tf), lambda i, j, f: (0, f)),
                  pl.BlockSpec((3, tf), lambda i, j, f: (0, f)),
                  pl.BlockSpec((1, tf), lambda i, j, f: (0, f)),
                  pl.BlockSpec((tf, d), lambda i, j, f: (f, 0))],
        out_specs=[pl.BlockSpec((1, tm, d), lambda i, j, f: (i, j, 0)),
                   pl.BlockSpec((1, 1, padr, tf), lambda i, j, f: (i, j, 0, f))],
        out_shape=[jax.ShapeDtypeStruct((b, t, d), F32), jax.ShapeDtypeStruct((b, nt, padr, dff), F32)],
        scratch_shapes=[pltpu.VMEM((tm, d), BF16), pltpu.VMEM((padr + tm, tf), F32),
                        pltpu.VMEM((nf, padr, tf), F32), pltpu.VMEM((tm, d), F32)],
        compiler_params=_cparams(("arbitrary", "arbitrary", "arbitrary")),
        name="ffn",
    )(x, pre, gpre, gpost, wu, wg, cw, cb, wd)


def _rot_cols(w):
    d, n = w.shape
    w4 = w.reshape(d, n // HEAD_DIM, 2, HEAD_DIM // 2)
    return jnp.concatenate([-w4[:, :, 1], w4[:, :, 0]], axis=2).reshape(d, n)


def _rope_tables(pos):
    half = HEAD_DIM // 2
    freq = ROPE_THETA ** (-jnp.arange(half, dtype=F32) / half)
    ang = pos.astype(F32)[:, None] * freq[None, :]
    cos, sin = jnp.cos(ang), jnp.sin(ang)
    return jnp.concatenate([cos] * 4, axis=1), jnp.concatenate([sin] * 4, axis=1)


def _block_diag(w):
    n, c, d = w.shape
    return jnp.einsum('ncd,nm->ncmd', w, jnp.eye(n, dtype=w.dtype)).reshape(n * c, n * d)


def _layer_weights(l, p):
    d_model = p['w_in'].shape[1]
    w = p['w_in'][l]
    sizes = [512, 512, 512, 256, 256, 256, 3 * N_HEADS]
    cuts = np.cumsum([0] + sizes)
    xr, gate, q, kvc, kvs, kvw, gl = [w[:, cuts[i]:cuts[i + 1]] for i in range(7)]
    gl_pad = jnp.pad(gl, ((0, 0), (0, LANES - gl.shape[1])))
    wcat = jnp.concatenate([xr, gate, q, _rot_cols(q), kvc, kvs, _rot_cols(kvs[:, :KV_DIM]),
                            kvw, _rot_cols(kvw[:, :KV_DIM]), gl_pad], axis=1).astype(BF16)
    wrow = jnp.concatenate([xr, gate, q, _rot_cols(q), gl_pad], axis=1).astype(BF16)
    wt = jnp.concatenate([kvc, kvs, _rot_cols(kvs[:, :KV_DIM]), kvw, _rot_cols(kvw[:, :KV_DIM])], axis=1).T.astype(BF16)
    row = lambda v: v.reshape(1, -1)
    rg = (p['rg_conv_w'][l], row(p['rg_conv_b'][l]), _block_diag(p['rg_wa'][l]).astype(BF16),
          _block_diag(p['rg_wx'][l]).astype(BF16), row(p['rg_ba'][l]), row(p['rg_bx'][l]), row(p['rg_lambda'][l]))
    sel = lambda a, b: jnp.stack([a, a, b, b])
    eye2 = jnp.eye(N_KV, dtype=F32)
    w1 = jnp.stack([p['cmpk_w1'][l], p['cmpv_w1'][l]])
    w1bd = jnp.einsum('ksdf,he->skhdef', w1, eye2).reshape(2 * CMP_STRIDE, 2, KV_DIM, 2 * KV_DIM)
    pe = sel(p['cmpk_pe'][l], p['cmpv_pe'][l])
    pe = pe.transpose(1, 0, 2).reshape(2 * CMP_STRIDE, 4 * HEAD_DIM)
    b1 = sel(p['cmpk_b1'][l], p['cmpv_b1'][l]).reshape(1, -1)
    w2 = _block_diag(sel(p['cmpk_w2'][l], p['cmpv_w2'][l]))
    pairs = lambda w: w.reshape(CMP_STRIDE // 2, 2, 2, KV_DIM, 2 * KV_DIM).transpose(0, 2, 1, 3, 4).reshape(
        CMP_STRIDE // 2, 2, 2 * KV_DIM, 2 * KV_DIM).astype(BF16)
    cmpw = (pe[:CMP_STRIDE], pe[CMP_STRIDE:], pairs(w1bd[:CMP_STRIDE]), pairs(w1bd[CMP_STRIDE:]), b1, w2.astype(BF16))
    dff = p['ffn_w_in'].shape[2] // 2
    ffn = (p['ffn_w_in'][l][:, :dff].astype(BF16), p['ffn_w_in'][l][:, dff:].astype(BF16),
           p['ffn_conv_w'][l], row(p['ffn_conv_b'][l]), p['ffn_w_down'][l].astype(BF16))
    norms = tuple(row(p[k][l]) for k in ('norm_mix_pre', 'norm_mix_post', 'norm_ffn_pre', 'norm_ffn_post'))
    return dict(wcat=wcat, wrow=wrow, wt=wt, rg=rg, cmp=cmpw, w_out=p['w_out'][l].astype(BF16), ffn=ffn, norms=norms)


def _pool_map(n_slc_pad, n_rows):
    j = np.arange(n_slc_pad)[:, None]
    m = np.arange(n_rows)[None, :]
    r = SLC_BLOCK // CMP_STRIDE
    return jnp.asarray(((m >= r * j) & (m <= r * j + r)).astype(np.float32))


def _heads_major(a, b, t):
    a5 = a.reshape(b, t, 2, N_KV, HEAD_DIM)
    return a5[:, :, 0].transpose(0, 2, 1, 3), a5[:, :, 1].transpose(0, 2, 1, 3)


def _rows_to_state(a_t):
    b, _, t = a_t.shape
    return a_t.reshape(b, 2, N_KV, HEAD_DIM, t).transpose(0, 4, 1, 2, 3)


def _prompt_mixer(hp, lw, tabs):
    b, t, d = hp.shape
    g_pre, g_post = lw['norms'][0], lw['norms'][1]
    xr, gate, qc, qr, gl, kvct, kvst, kvstb, kvwt, kvwtb = _proj_t(
        hp.reshape(b * t, d), g_pre, lw['wrow'], lw['wt'], *tabs, b, _div_tile(t, 512))
    c = xr.shape[1]
    y, htail = _rglru(xr.reshape(b, t, c), gate.reshape(b, t, c), jnp.zeros((b, 8, c), F32),
                      jnp.zeros((b, 1, c), F32), *lw['rg'], _div_tile(t, 512))
    npg = t // 128
    ident = jnp.arange(b * npg, dtype=jnp.int32).reshape(b, npg)
    kcs = _compress(kvct, ident, lw['cmp'], _div_tile(npg, 32), False)
    kck, kcv = _heads_major(kcs, b, t // CMP_STRIDE)
    kcv = jnp.concatenate([kcv, jnp.ones_like(kcv)], axis=-1)
    nqb = t // Q_BLOCK
    n_slc = -(-t // SLC_BLOCK)
    onehot_t = (jnp.arange(HEAD_DIM)[:, None] == jnp.arange(t)[None, :] // SLC_BLOCK).astype(BF16)
    o = _nsa_prompt(qc, qr, gl, kck, kcv, kvstb, kvwtb, onehot_t, _pool_map(n_slc, t // CMP_STRIDE), t)
    hp = _merge(y.reshape(b * t, c), o, lw['w_out'], g_post, hp.reshape(b * t, d), _div_tile(b * t, 512))
    win_buf = min(WINDOW, t)
    st = (_rows_to_state(kvct), _rows_to_state(kvst), _rows_to_state(kvwt[:, :, t - win_buf:]), htail[:, 7],
          xr.reshape(b, t, c)[:, t - 3:])
    return hp.reshape(b, t, d), st


def _sample_mixer(hs, lw, s_cos, s_sin, cmp_pool, slc_pool, page_table, win_t, layer, past_len, rg_h0, rg_conv0):
    b, t, d = hs.shape
    g_pre, g_post = lw['norms'][0], lw['norms'][1]
    xr, gate, qc, qr, kvc, kvs, kvw, gl = _proj(hs.reshape(b * t, d), g_pre, lw['wcat'], s_cos, s_sin, b * t)
    c = xr.shape[1]
    padt = lambda a: jnp.pad(a.reshape(b, t, -1), ((0, 0), (0, 8 - t), (0, 0)))
    pre8 = jnp.pad(rg_conv0, ((0, 0), (5, 0), (0, 0)))
    y8, htail = _rglru(padt(xr), padt(gate), pre8, rg_h0.reshape(b, 1, c), *lw['rg'], 8)
    y = y8[:, :t].reshape(b * t, c)
    npg = page_table.shape[1]
    kcs = _compress(cmp_pool, page_table, lw['cmp'], _div_tile(npg, 32), True)
    nj = -(-(past_len + t) // SLC_BLOCK)
    pp = _div_tile(npg, 16)
    njp = -(-nj // LANES) * LANES
    rows = GROUP * N_KV * t

    def qrows(a):
        a5 = a.reshape(b, t, N_KV, GROUP, HEAD_DIM).transpose(0, 3, 2, 1, 4)
        z = jnp.zeros_like(a5[:, :, 0])
        top = jnp.concatenate([a5[:, :, 0], z], axis=-1)
        bot = jnp.concatenate([z, a5[:, :, 1]], axis=-1)
        return jnp.stack([top, bot], axis=2).reshape(b, rows, LANES)

    gts = gl[:, :3 * N_HEADS].reshape(b, t, N_KV, GROUP, 3).transpose(0, 3, 2, 1, 4).reshape(b, rows, 3)
    gts = jnp.pad(gts, ((0, 0), (0, 0), (0, 5)))
    expand = (jnp.arange(LANES)[:, None] == jnp.arange(pp * 128)[None, :] // SLC_BLOCK).astype(BF16)
    new = jnp.concatenate([kvs.reshape(b, t, 256), kvw.reshape(b, t, 256)], axis=-1)
    new = jnp.pad(new, ((0, 0), (0, 8 - t), (0, 0)))
    o32 = _nsa_sample(page_table, qrows(qc), qrows(qr), gts, kcs, slc_pool, expand, _pool_map(njp, kcs.shape[1]).T,
                      new, win_t, layer, pp, past_len, t, nj)
    o5 = o32.reshape(b, GROUP, N_KV, t, N_KV, HEAD_DIM)
    o = jnp.stack([o5[:, :, 0, :, 0], o5[:, :, 1, :, 1]], axis=1)
    o = o.transpose(0, 3, 1, 2, 4).reshape(b * t, N_HEADS * HEAD_DIM).astype(BF16)
    hs = _merge(y, o, lw['w_out'], g_post, hs.reshape(b * t, d), b * t)
    nbuf = win_t.shape[3]
    win_state_t = jnp.concatenate([win_t[layer], kvw.reshape(b, t, 256).transpose(0, 2, 1)], axis=2)[:, :, -nbuf:]
    st = (kvc.reshape(b, t, 2, N_KV, HEAD_DIM), kvs.reshape(b, t, 2, N_KV, HEAD_DIM), _rows_to_state(win_state_t),
          htail[:, (t - 1) % 8], jnp.concatenate([rg_conv0, xr.reshape(b, t, c)], axis=1)[:, -3:])
    return hs.reshape(b, t, d), st


def kernel(x_prompt, x_sample, cache_cmp_kv, cache_slc_kv, cache_win_kv, state_rg_h, state_rg_conv, state_ffn_conv, page_table, norm_mix_pre, norm_mix_post, norm_ffn_pre, norm_ffn_post, w_in, rg_conv_w, rg_conv_b, rg_wa, rg_ba, rg_wx, rg_bx, rg_lambda, cmpk_pe, cmpk_w1, cmpk_b1, cmpk_w2, cmpv_pe, cmpv_w1, cmpv_b1, cmpv_w2, w_out, ffn_w_in, ffn_conv_w, ffn_conv_b, ffn_w_down):
    params = dict(norm_mix_pre=norm_mix_pre, norm_mix_post=norm_mix_post, norm_ffn_pre=norm_ffn_pre,
                  norm_ffn_post=norm_ffn_post, w_in=w_in, rg_conv_w=rg_conv_w, rg_conv_b=rg_conv_b, rg_wa=rg_wa,
                  rg_ba=rg_ba, rg_wx=rg_wx, rg_bx=rg_bx, rg_lambda=rg_lambda, cmpk_pe=cmpk_pe, cmpk_w1=cmpk_w1,
                  cmpk_b1=cmpk_b1, cmpk_w2=cmpk_w2, cmpv_pe=cmpv_pe, cmpv_w1=cmpv_w1, cmpv_b1=cmpv_b1,
                  cmpv_w2=cmpv_w2, w_out=w_out, ffn_w_in=ffn_w_in, ffn_conv_w=ffn_conv_w, ffn_conv_b=ffn_conv_b,
                  ffn_w_down=ffn_w_down)
    depth = w_in.shape[0]
    bp, tp, d = x_prompt.shape
    bs, ts, _ = x_sample.shape
    past_len = page_table.shape[1] * cache_cmp_kv.shape[2]
    dff = ffn_w_in.shape[2] // 2
    p_cos, p_sin = _rope_tables(jnp.arange(tp))
    p_tabs = (p_cos, p_sin, p_cos.T, p_sin.T)
    s_cos, s_sin = _rope_tables(past_len + jnp.arange(ts))
    s_cos, s_sin = jnp.tile(s_cos, (bs, 1)), jnp.tile(s_sin, (bs, 1))
    n_pool, page = cache_cmp_kv.shape[1], cache_cmp_kv.shape[2]
    pages_t = lambda c: c.transpose(0, 1, 3, 4, 5, 2).reshape(depth * n_pool, 2 * KV_DIM, page)
    cmp_pool, slc_pool = pages_t(cache_cmp_kv), pages_t(cache_slc_kv)
    win_t = cache_win_kv.transpose(0, 1, 3, 4, 5, 2).reshape(depth, bs, 2 * KV_DIM, cache_win_kv.shape[2])
    hp, hs = x_prompt, x_sample
    states_p, states_s = [], []
    for l in range(depth):
        lw = _layer_weights(l, params)
        g_fpre, g_fpost = lw['norms'][2], lw['norms'][3]
        hp, st = _prompt_mixer(hp, lw, p_tabs)
        tmf = _div_tile(tp, 512)
        hp, tail = _ffn(hp, jnp.zeros((bp, 8, dff), F32), g_fpre, g_fpost, *lw['ffn'], tmf, _div_tile(dff, 1024), 1)
        states_p.append(st + (tail[:, -1, 6:],))
        hs, st = _sample_mixer(hs, lw, s_cos, s_sin, cmp_pool, slc_pool, page_table + l * n_pool, win_t, l,
                               past_len, state_rg_h[l], state_rg_conv[l])
        x_tm = hs.transpose(1, 0, 2).reshape(1, ts * bs, d)
        pre_tm = state_ffn_conv[l].transpose(1, 0, 2).reshape(1, 2 * bs, dff)
        out_tm, tail = _ffn(x_tm, pre_tm, g_fpre, g_fpost, *lw['ffn'], ts * bs, _div_tile(dff, 1024), bs)
        hs = out_tm.reshape(ts, bs, d).transpose(1, 0, 2)
        states_s.append(st + (tail.reshape(2, bs, dff).transpose(1, 0, 2),))
    p_cmp, p_slc, p_win, p_rgh, p_rgc, p_ffc = [jnp.stack(z) for z in zip(*states_p)]
    s_cmp, s_slc, s_win, s_rgh, s_rgc, s_ffc = [jnp.stack(z) for z in zip(*states_s)]
    return (hp, hs, p_cmp, s_cmp, p_slc, s_slc, p_win, s_win, p_rgh, s_rgh, p_rgc, s_rgc, p_ffc, s_ffc)
```

```python
import functools

import jax
import jax.numpy as jnp
import numpy as np
from jax import lax
from jax.experimental import pallas as pl
from jax.experimental.pallas import tpu as pltpu

F32 = jnp.float32
BF16 = jnp.bfloat16

HEAD_DIM = 64
N_KV = 2
GROUP = 4
N_HEADS = N_KV * GROUP
KV_DIM = N_KV * HEAD_DIM
CMP_STRIDE = 16
SLC_BLOCK = 64
N_SELECT = 16
N_LOCAL = 2
WINDOW = 512
Q_BLOCK = 128
RG_C = 8.0
ROPE_THETA = 10000.0
EPS = 1e-6

V7X_VMEM_BYTES = 64 * 1024 * 1024
VMEM_LIMIT = 48 * 1024 * 1024
SUBLANES = 8
LANES = 128

LOG2E = 1.4426950408889634
NEG = -2.0 ** 100
BIG = 1e30


def _cparams(sem):
    return pltpu.CompilerParams(dimension_semantics=sem, vmem_limit_bytes=VMEM_LIMIT)


def _div_tile(n, pref):
    t = min(n, pref)
    while n % t:
        t -= 1
    return t


def _rms(x, g):
    y = x * lax.rsqrt(jnp.mean(x * x, axis=-1, keepdims=True) + EPS)
    return y * g


def _sigmoid(x):
    return 0.5 * jnp.tanh(0.5 * x) + 0.5


def _nt(a, b, precision=None):
    return lax.dot_general(a, b, (((1,), (1,)), ((), ())), preferred_element_type=F32, precision=precision)


def _mm(a, b):
    return jnp.dot(a, b, preferred_element_type=F32)


C_XR, C_GATE, C_Q, C_QROT, C_KVC, C_KVS, C_KSROT, C_KVW, C_KWROT, C_GL, C_END = (
    0, 512, 1024, 1536, 2048, 2304, 2560, 2688, 2944, 3072, 3200)


def _proj_kernel(x_ref, g_ref, w_ref, cos_ref, sin_ref,
                 xr_ref, gate_ref, qc_ref, qr_ref, kvc_ref, kvs_ref, kvw_ref, gl_ref):
    xn = _rms(x_ref[...], g_ref[...]).astype(BF16)

    def mm(c0, c1):
        return _mm(xn, w_ref[:, c0:c1])

    cos = cos_ref[...]
    sin = sin_ref[...]
    xr_ref[...] = mm(C_XR, C_GATE)
    gate_ref[...] = mm(C_GATE, C_Q)
    q = mm(C_Q, C_QROT)
    qrot = mm(C_QROT, C_KVC)
    cos4 = jnp.concatenate([cos] * 4, axis=1)
    sin4 = jnp.concatenate([sin] * 4, axis=1)
    scale = HEAD_DIM ** -0.5
    qc_ref[...] = (q * scale).astype(BF16)
    qr_ref[...] = ((q * cos4 + qrot * sin4) * scale).astype(BF16)
    kvc_ref[...] = mm(C_KVC, C_KVS)
    kvs = mm(C_KVS, C_KSROT)
    ksrot = mm(C_KSROT, C_KVW)
    kvs_ref[:, 0:KV_DIM] = kvs[:, 0:KV_DIM] * cos + ksrot * sin
    kvs_ref[:, KV_DIM:] = kvs[:, KV_DIM:]
    kvw = mm(C_KVW, C_KWROT)
    kwrot = mm(C_KWROT, C_GL)
    kvw_ref[:, 0:KV_DIM] = kvw[:, 0:KV_DIM] * cos + kwrot * sin
    kvw_ref[:, KV_DIM:] = kvw[:, KV_DIM:]
    gl_ref[...] = jax.nn.sigmoid(mm(C_GL, C_END))


R_KVC, R_KVS, R_KSROT, R_KVW, R_KWROT, R_END = 0, 256, 512, 640, 896, 1024


def _proj_t_kernel(x_ref, g_ref, w_ref, wt_ref, cos_ref, sin_ref, cost_ref, sint_ref,
                   xr_ref, gate_ref, qc_ref, qr_ref, gl_ref, kvct_ref, kvst_ref, kvstb_ref, kvwt_ref, kvwtb_ref):
    xn = _rms(x_ref[...], g_ref[...]).astype(BF16)

    def mm(c0, c1):
        return _mm(xn, w_ref[:, c0:c1])

    def mt(r0, r1):
        return _nt(wt_ref[r0:r1, :], xn)

    xr_ref[...] = mm(0, 512)
    gate_ref[...] = mm(512, 1024)
    q = mm(1024, 1536)
    qrot = mm(1536, 2048)
    cos4 = jnp.concatenate([cos_ref[...]] * 4, axis=1)
    sin4 = jnp.concatenate([sin_ref[...]] * 4, axis=1)
    scale = HEAD_DIM ** -0.5 * LOG2E
    qc_ref[...] = (q * scale).astype(BF16)
    qr_ref[...] = ((q * cos4 + qrot * sin4) * scale).astype(BF16)
    gl_ref[...] = jax.nn.sigmoid(mm(2048, 2176))
    kvct_ref[0] = mt(R_KVC, R_KVS)
    cost, sint = cost_ref[...], sint_ref[...]
    for lo, rot, end, f_ref, b_ref in ((R_KVS, R_KSROT, R_KVW, kvst_ref, kvstb_ref),
                                       (R_KVW, R_KWROT, R_END, kvwt_ref, kvwtb_ref)):
        kv = mt(lo, rot)
        k = kv[0:KV_DIM] * cost + mt(rot, end) * sint
        f_ref[0, 0:KV_DIM, :] = k
        f_ref[0, KV_DIM:, :] = kv[KV_DIM:]
        b_ref[0, 0:KV_DIM, :] = k.astype(BF16)
        b_ref[0, KV_DIM:, :] = kv[KV_DIM:].astype(BF16)


def _proj_t(x2d, g, wrow, wt, cos, sin, cost, sint, b, tm):
    m, d = x2d.shape
    t = m // b
    nper = t // tm
    row = lambda w: pl.BlockSpec((tm, w), lambda i: (i, 0))
    tab = pl.BlockSpec((tm, LANES), lambda i: (i % nper, 0))
    tabt = pl.BlockSpec((KV_DIM, tm), lambda i: (0, i % nper))
    tr = pl.BlockSpec((1, 2 * KV_DIM, tm), lambda i: (i // nper, 0, i % nper))
    full = lambda a: pl.BlockSpec(a.shape, lambda i: (0,) * a.ndim)
    widths = (512, 512, 512, 512, 128)
    dtypes = (F32, F32, BF16, BF16, F32)
    tdt = (F32, F32, BF16, F32, BF16)
    return pl.pallas_call(
        _proj_t_kernel,
        grid=(m // tm,),
        in_specs=[row(d), full(g), full(wrow), full(wt), tab, tab, tabt, tabt],
        out_specs=[row(w) for w in widths] + [tr] * 5,
        out_shape=[jax.ShapeDtypeStruct((m, w), dt) for w, dt in zip(widths, dtypes)]
        + [jax.ShapeDtypeStruct((b, 2 * KV_DIM, t), dt) for dt in tdt],
        compiler_params=_cparams(("arbitrary",)),
        name="proj_t",
    )(x2d, g, wrow, wt, cos, sin, cost, sint)


def _proj(x2d, g, wcat, cos, sin, tm):
    m, d = x2d.shape
    nper = cos.shape[0] // tm
    row = lambda w: pl.BlockSpec((tm, w), lambda i: (i, 0))
    tab = pl.BlockSpec((tm, LANES), lambda i: (i % nper, 0))
    widths = (512, 512, 512, 512, 256, 256, 256, 128)
    dtypes = (F32, F32, BF16, BF16, F32, F32, F32, F32)
    return pl.pallas_call(
        _proj_kernel,
        grid=(m // tm,),
        in_specs=[row(d), pl.BlockSpec((1, d), lambda i: (0, 0)),
                  pl.BlockSpec(wcat.shape, lambda i: (0, 0)), tab, tab],
        out_specs=[row(w) for w in widths],
        out_shape=[jax.ShapeDtypeStruct((m, w), dt) for w, dt in zip(widths, dtypes)],
        compiler_params=_cparams(("arbitrary",)),
        name="proj",
    )(x2d, g, wcat, cos, sin)


def _rglru_kernel(xr_ref, gate_ref, pre_ref, h0_ref, cw_ref, cb_ref, wa_ref, wx_ref, ba_ref, bx_ref, lam_ref,
                  y_ref, htail_ref, xpad, hc, a_s, b_s):
    tt = xr_ref.shape[1]
    j = pl.program_id(1)

    @pl.when(j == 0)
    def _():
        xpad[0:8, :] = pre_ref[0]
        hc[...] = jnp.broadcast_to(h0_ref[0], hc.shape)

    @pl.when(j > 0)
    def _():
        xpad[0:8, :] = xpad[tt:tt + 8, :]

    xpad[8:8 + tt, :] = xr_ref[0]
    xc = cb_ref[...]
    for k in range(4):
        xc = xc + xpad[pl.ds(5 + k, tt), :] * cw_ref[k:k + 1, :]
    xb = xc.astype(BF16)
    r = _sigmoid(_mm(xb, wa_ref[...]) + ba_ref[...])
    i = _sigmoid(_mm(xb, wx_ref[...]) + bx_ref[...])
    log_a = RG_C * r * jax.nn.log_sigmoid(lam_ref[...])
    a_s[...] = jnp.exp(log_a)
    t = jnp.tanh(log_a)
    b_s[...] = jnp.sqrt(-2.0 * t / (1.0 - t)) * (i * xc)

    row = lax.broadcasted_iota(jnp.int32, (8, a_s.shape[1]), 0)

    def body(gi, h):
        r0 = pl.multiple_of(gi * 8, 8)
        av = a_s[pl.ds(r0, 8), :]
        bv = b_s[pl.ds(r0, 8), :]
        for s in (1, 2, 4):
            keep = row >= s
            a_sh = pltpu.roll(av, s, 0)
            b_sh = pltpu.roll(bv, s, 0)
            bv = jnp.where(keep, av * b_sh + bv, bv)
            av = jnp.where(keep, av * a_sh, av)
        hs = av * h + bv
        b_s[pl.ds(r0, 8), :] = hs
        return jnp.broadcast_to(hs[7:8, :], hs.shape)

    h = lax.fori_loop(0, tt // 8, body, hc[...])
    hc[...] = h
    htail_ref[0] = b_s[tt - 8:tt, :]
    y_ref[0] = (b_s[...] * jax.nn.gelu(gate_ref[0])).astype(y_ref.dtype)


def _rglru(xr, gate, pre8, h0, cw, cb, wa, wx, ba, bx, lam, tt):
    b, t, c = xr.shape
    full = lambda a: pl.BlockSpec(a.shape, lambda i, j: (0,) * a.ndim)
    return pl.pallas_call(
        _rglru_kernel,
        grid=(b, t // tt),
        in_specs=[pl.BlockSpec((1, tt, c), lambda i, j: (i, j, 0)),
                  pl.BlockSpec((1, tt, c), lambda i, j: (i, j, 0)),
                  pl.BlockSpec((1, 8, c), lambda i, j: (i, 0, 0)),
                  pl.BlockSpec((1, 1, c), lambda i, j: (i, 0, 0)),
                  full(cw), full(cb), full(wa), full(wx), full(ba), full(bx), full(lam)],
        out_specs=[pl.BlockSpec((1, tt, c), lambda i, j: (i, j, 0)),
                   pl.BlockSpec((1, 8, c), lambda i, j: (i, 0, 0))],
        out_shape=[jax.ShapeDtypeStruct((b, t, c), BF16), jax.ShapeDtypeStruct((b, 8, c), F32)],
        scratch_shapes=[pltpu.VMEM((tt + 8, c), F32), pltpu.VMEM((8, c), F32),
                        pltpu.VMEM((tt, c), F32), pltpu.VMEM((tt, c), F32)],
        compiler_params=_cparams(("arbitrary", "arbitrary")),
        name="rglru",
    )(xr, gate, pre8, h0, cw, cb, wa, wx, ba, bx, lam)


def _page_copy(pt_ref, src_ref, buf, sem, idx, slot, p, seq_pages):
    page = pt_ref[idx]
    if seq_pages is None:
        src = src_ref.at[page]
    else:
        src = src_ref.at[page // seq_pages, :, pl.ds(pl.multiple_of((page % seq_pages) * 128, 128), 128)]
    return pltpu.make_async_copy(src, buf.at[slot, p], sem.at[slot])


def _cmp_kernel(pt_ref, pool_ref, pea_ref, peb_ref, w1a_ref, w1b_ref, b1_ref, w2_ref, out_ref,
                buf, sem, xbuf, ua_s, *, pp, seq_pages):
    nb, nt = pl.num_programs(0), pl.num_programs(1)
    j = pl.program_id(1)
    step = pl.program_id(0) * nt + j
    slot = step % 2
    m = pp * 8

    def fetch(st, sl):
        for p in range(pp):
            _page_copy(pt_ref, pool_ref, buf, sem, st * pp + p, sl, p, seq_pages).start()

    @pl.when(step == 0)
    def _():
        fetch(step, slot)

    @pl.when(step + 1 < nb * nt)
    def _():
        fetch(step + 1, 1 - slot)

    for p in range(pp):
        _page_copy(pt_ref, pool_ref, buf, sem, step * pp + p, slot, p, seq_pages).wait()

    n_part = xbuf.shape[0]
    ph = pp // n_part
    mh = ph * 8
    part_a, part_b = [], []
    for part in range(n_part):
        for p in range(ph):
            for kv in range(2):
                xbuf[part, kv, p * 128:(p + 1) * 128, :] = buf[slot, part * ph + p, kv * KV_DIM:(kv + 1) * KV_DIM, :].T
        acc_a = [jnp.zeros((mh, 256), F32)] * 2
        acc_b = [jnp.zeros((mh, 256), F32)] * 2
        for s in range(0, CMP_STRIDE, 2):
            for kv in range(2):
                x0 = xbuf[part, kv, pl.ds(s, mh, stride=CMP_STRIDE), :]
                x1 = xbuf[part, kv, pl.ds(s + 1, mh, stride=CMP_STRIDE), :]
                lanes = slice(kv * KV_DIM, (kv + 1) * KV_DIM)

                def pair(pe_ref):
                    return jnp.concatenate([(x0 + pe_ref[s:s + 1, lanes]).astype(BF16),
                                            (x1 + pe_ref[s + 1:s + 2, lanes]).astype(BF16)], axis=1)

                acc_a[kv] = acc_a[kv] + _mm(pair(pea_ref), w1a_ref[s // 2, kv])
                acc_b[kv] = acc_b[kv] + _mm(pair(peb_ref), w1b_ref[s // 2, kv])
        part_a.append(jnp.concatenate(acc_a, axis=1))
        part_b.append(jnp.concatenate(acc_b, axis=1))
    acc_a = jnp.concatenate(part_a, axis=0)
    acc_b = jnp.concatenate(part_b, axis=0)

    @pl.when(j == 0)
    def _():
        ua_s[0:8, :] = jnp.zeros((8, 512), F32)

    @pl.when(j > 0)
    def _():
        ua_s[0:8, :] = ua_s[m:m + 8, :]

    ua_s[8:8 + m, :] = acc_a
    hid = jax.nn.gelu(ua_s[pl.ds(7, m), :] + acc_b + b1_ref[...])
    out_ref[0] = _mm(hid.astype(BF16), w2_ref[...]).astype(out_ref.dtype)


def _compress(pool, page_table, cw, pp, paged):
    b, npg = page_table.shape
    nt = npg // pp
    m = pp * 8
    n_part = 2 if pp % 2 == 0 else 1
    full = lambda a: pl.BlockSpec(a.shape, lambda i, j, pt: (0,) * a.ndim)
    pea, peb, w1a, w1b, b1, w2 = cw
    gs = pltpu.PrefetchScalarGridSpec(
        num_scalar_prefetch=1,
        grid=(b, nt),
        in_specs=[pl.BlockSpec(memory_space=pl.ANY), full(pea), full(peb), full(w1a), full(w1b), full(b1), full(w2)],
        out_specs=pl.BlockSpec((1, m, 256), lambda i, j, pt: (i, j, 0)),
        scratch_shapes=[pltpu.VMEM((2, pp, 2 * KV_DIM, 128), F32), pltpu.SemaphoreType.DMA((2,)),
                        pltpu.VMEM((n_part, 2, pp // n_part * 128, KV_DIM), F32), pltpu.VMEM((m + 8, 512), F32)],
    )
    return pl.pallas_call(
        functools.partial(_cmp_kernel, pp=pp, seq_pages=None if paged else npg),
        grid_spec=gs,
        out_shape=jax.ShapeDtypeStruct((b, nt * m, 256), BF16),
        compiler_params=_cparams(("arbitrary", "arbitrary")),
        name="compress",
    )(page_table.reshape(-1), pool, pea, peb, w1a, w1b, b1, w2)


def _softmax_rows(s, mask):
    sm = jnp.where(mask, s, -BIG)
    mx = jnp.max(sm, axis=-1, keepdims=True)
    e = jnp.where(mask, jnp.exp(sm - mx), 0.0)
    return e / jnp.maximum(jnp.sum(e, axis=-1, keepdims=True), 1e-30)


def _nsa_prompt_kernel(qc_ref, qr_ref, g_ref, kck_ref, kcv_ref, kvs_ref, kvw_ref, oh_ref, pool_ref,
                       o_ref, sc_s, rank_s, *, t_len, kc_len, wl):
    qb = pl.program_id(1)
    start = qb * Q_BLOCK
    nc = kck_ref.shape[2]
    nj = pool_ref.shape[0]
    rows = GROUP * Q_BLOCK
    qpos = start + (lax.broadcasted_iota(jnp.int32, (rows, 1), 0) & (Q_BLOCK - 1))

    heads = range(N_KV)
    k_rows = [slice(h * HEAD_DIM, (h + 1) * HEAD_DIM) for h in heads]
    v_rows = [slice(KV_DIM + h * HEAD_DIM, KV_DIM + (h + 1) * HEAD_DIM) for h in heads]
    n_chunks = (start + Q_BLOCK + kc_len - 1) // kc_len
    n_full = n_chunks - 1
    k_diag = pl.multiple_of(n_full * kc_len, kc_len)
    base = pl.multiple_of(jnp.maximum(start + Q_BLOCK - wl, 0), Q_BLOCK)

    qpos1 = start + lax.broadcasted_iota(jnp.int32, (Q_BLOCK, 1), 0)
    mrow = lax.broadcasted_iota(jnp.int32, (Q_BLOCK, nc), 1)
    cbias = jnp.where((mrow >= 1) & (mrow * CMP_STRIDE + (CMP_STRIDE - 1) <= qpos1), 0.0, NEG)
    dbias = jnp.where(k_diag + lax.broadcasted_iota(jnp.int32, (Q_BLOCK, kc_len), 1) <= qpos1, 0.0, NEG)
    dist = qpos1 - (base + lax.broadcasted_iota(jnp.int32, (Q_BLOCK, wl), 1))
    wbias = jnp.where((dist >= 0) & (dist < WINDOW), 0.0, NEG)
    any_cmp = jnp.where(qpos >= 2 * CMP_STRIDE - 1, 1.0, 0.0)

    def add_bias(s, bias):
        return (s.reshape(GROUP, Q_BLOCK, s.shape[1]) + bias[None]).reshape(s.shape)

    def with_ones(vt):
        return jnp.concatenate([vt, jnp.ones(vt.shape, vt.dtype)], axis=0)

    def head_rows(ref, h):
        return jnp.concatenate([ref[:, (h * GROUP + g) * HEAD_DIM:(h * GROUP + g + 1) * HEAD_DIM]
                                for g in range(GROUP)], axis=0)

    def gate_rows(h, branch):
        gl = g_ref[...]
        cols = [(h * GROUP + g) * 3 + branch for g in range(GROUP)]
        return jnp.concatenate([gl[:, c:c + 1] for c in cols], axis=0)

    r_cmp, inv_cmp = [], []
    jidx = lax.broadcasted_iota(jnp.int32, (nj, Q_BLOCK), 0)
    lag = ((start + lax.broadcasted_iota(jnp.int32, (nj, Q_BLOCK), 1)) >> 6) - jidx
    forced = (jidx == 0) | ((lag >= 0) & (lag < N_LOCAL))
    for h in heads:
        s = add_bias(_nt(head_rows(qc_ref, h), kck_ref[0, h]), cbias)
        e = jnp.exp2(s - jnp.max(s, axis=-1, keepdims=True))
        r = _mm(e.astype(BF16), kcv_ref[0, h])
        inv = any_cmp / jnp.maximum(r[:, HEAD_DIM:HEAD_DIM + 1], 1e-30)
        p = e * inv
        psum = p[0:Q_BLOCK] + p[Q_BLOCK:2 * Q_BLOCK] + p[2 * Q_BLOCK:3 * Q_BLOCK] + p[3 * Q_BLOCK:]
        score_t = _nt(pool_ref[...], psum, precision=lax.Precision.HIGHEST)
        sc_s[h] = jnp.where(lag >= 0, jnp.where(forced, BIG, score_t), -BIG)
        rank_s[h] = jnp.zeros((nj, Q_BLOCK), F32)
        r_cmp.append(r)
        inv_cmp.append(inv)

    sub = lax.broadcasted_iota(jnp.int32, (SUBLANES, Q_BLOCK), 0)
    n_grp = nj // SUBLANES
    for h in heads:
        scv = sc_s[h]
        cnt = [jnp.zeros((SUBLANES, Q_BLOCK), F32)] * n_grp
        for i in range(nj):
            gi, ii = divmod(i, SUBLANES)
            si = scv[i:i + 1, :]
            for r in range(n_grp):
                blk = scv[r * SUBLANES:(r + 1) * SUBLANES]
                if r == gi:
                    beat = jnp.where(sub > ii, jnp.where(si >= blk, 1.0, 0.0), jnp.where(si > blk, 1.0, 0.0))
                else:
                    beat = jnp.where(si >= blk, 1.0, 0.0) if r > gi else jnp.where(si > blk, 1.0, 0.0)
                cnt[r] = cnt[r] + beat
        rank_s[h] = jnp.concatenate(cnt, axis=0)

    q4r = [head_rows(qr_ref, h) for h in heads]
    qaug = []
    for h in heads:
        bias_t = jnp.where(rank_s[h] < min(N_SELECT, nj), 0.0, NEG)
        pad_t = jnp.concatenate([jnp.zeros((HEAD_DIM, Q_BLOCK), F32), bias_t]
                                + ([jnp.zeros((HEAD_DIM - nj, Q_BLOCK), F32)] if nj < HEAD_DIM else []), axis=0)
        bias4 = jnp.concatenate([pad_t.T[:, HEAD_DIM:]] * GROUP, axis=0)
        qaug.append(jnp.concatenate([q4r[h].astype(F32), bias4], axis=1).astype(BF16))

    den = lambda r: jnp.maximum(r[:, HEAD_DIM:HEAD_DIM + 1], 1e-30)
    o_cw = []
    for h in heads:
        sw = add_bias(_mm(q4r[h], kvw_ref[0, k_rows[h], pl.ds(base, wl)]), wbias)
        ew = jnp.exp2(sw - jnp.max(sw, axis=-1, keepdims=True)).astype(BF16)
        r_win = _nt(ew, with_ones(kvw_ref[0, v_rows[h], pl.ds(base, wl)]))
        o_cw.append(r_cmp[h] * (gate_rows(h, 0) * inv_cmp[h]) + r_win * (gate_rows(h, 2) / den(r_win)))

    def chunk(k0, state, diagonal):
        out = []
        for h in heads:
            m_i, acc = state[h]
            kaug = jnp.concatenate([kvs_ref[0, k_rows[h], pl.ds(k0, kc_len)], oh_ref[:, pl.ds(k0, kc_len)]], axis=0)
            sc = _mm(qaug[h], kaug)
            if diagonal:
                sc = add_bias(sc, dbias)
            m_n = jnp.maximum(m_i, jnp.max(sc, axis=-1, keepdims=True))
            pe = jnp.exp2(sc - m_n).astype(BF16)
            acc = jnp.exp2(m_i - m_n) * acc + _nt(pe, with_ones(kvs_ref[0, v_rows[h], pl.ds(k0, kc_len)]))
            out.append((m_n, acc))
        return tuple(out)

    init = tuple((jnp.full((rows, 1), -3e38, F32), jnp.zeros((rows, LANES), F32)) for _ in heads)
    state = chunk(k_diag, init, True)
    state = lax.fori_loop(0, n_full, lambda c, st: chunk(pl.multiple_of(c * kc_len, kc_len), st, False), state)

    for h in heads:
        r_slc = state[h][1]
        o = o_cw[h] + r_slc * (gate_rows(h, 1) / den(r_slc))
        for g in range(GROUP):
            lanes = slice((h * GROUP + g) * HEAD_DIM, (h * GROUP + g + 1) * HEAD_DIM)
            o_ref[:, lanes] = o[g * Q_BLOCK:(g + 1) * Q_BLOCK, 0:HEAD_DIM].astype(o_ref.dtype)


def _nsa_prompt(qc, qr, gt, kck, kcv, kvst, kvwt, onehot_t, pool_t, t_len):
    b = kck.shape[0]
    nqb = t_len // Q_BLOCK
    kc_len = _div_tile(t_len, 512)
    wl = min(WINDOW + Q_BLOCK, t_len)
    qspec = lambda w: pl.BlockSpec((Q_BLOCK, w), lambda i, j: (i * nqb + j, 0))
    seq = lambda a: pl.BlockSpec((1,) + a.shape[1:], lambda i, j: (i,) + (0,) * (a.ndim - 1))
    return pl.pallas_call(
        functools.partial(_nsa_prompt_kernel, t_len=t_len, kc_len=kc_len, wl=wl),
        grid=(b, nqb),
        in_specs=[qspec(qc.shape[1]), qspec(qr.shape[1]), qspec(gt.shape[1]), seq(kck), seq(kcv), seq(kvst), seq(kvwt),
                  pl.BlockSpec(onehot_t.shape, lambda i, j: (0, 0)), pl.BlockSpec(pool_t.shape, lambda i, j: (0, 0))],
        out_specs=qspec(qc.shape[1]),
        out_shape=jax.ShapeDtypeStruct(qc.shape, BF16),
        scratch_shapes=[pltpu.VMEM((N_KV,) + pool_t.shape[:1] + (Q_BLOCK,), F32)] * 2,
        compiler_params=_cparams(("arbitrary", "arbitrary")),
        name="nsa_prompt",
    )(qc, qr, gt, kck, kcv, kvst, kvwt, onehot_t, pool_t)


def _nsa_sample_kernel(pt_ref, qc_ref, qr_ref, g_ref, kc_ref, pool_ref, exp_ref, pmap_ref, new_ref, win_ref,
                       o_ref, buf, sem, bias_s, m_s, l_s, acc_s, ocmp_s, *, pp, past_len, n_tok, nj):
    nb, nt = pl.num_programs(0), pl.num_programs(1)
    j = pl.program_id(1)
    step = pl.program_id(0) * nt + j
    slot = step % 2
    rows = GROUP * N_KV * n_tok
    tk = pp * 128
    bpt = tk // SLC_BLOCK

    def fetch(st, sl):
        for p in range(pp):
            _page_copy(pt_ref, pool_ref, buf, sem, st * pp + p, sl, p, None).start()

    @pl.when(step == 0)
    def _():
        fetch(step, slot)

    @pl.when(step + 1 < nb * nt)
    def _():
        fetch(step + 1, 1 - slot)

    rid = lax.broadcasted_iota(jnp.int32, (rows, 1), 0)
    tok = rid % n_tok
    qpos = past_len + tok

    @pl.when(j == 0)
    def _():
        kc = kc_ref[0]
        nc = kc.shape[0]
        s = _nt(qc_ref[0], kc[:, 0:KV_DIM])
        mrow = lax.broadcasted_iota(jnp.int32, (rows, nc), 1)
        valid = (mrow >= 1) & (mrow * CMP_STRIDE + (CMP_STRIDE - 1) <= qpos)
        p = _softmax_rows(s, valid)
        ocmp_s[...] = _mm(p.astype(BF16), kc[:, KV_DIM:])
        r8 = N_KV * n_tok
        psum = p[0:r8] + p[r8:2 * r8] + p[2 * r8:3 * r8] + p[3 * r8:]
        score = jnp.dot(psum, pmap_ref[...], precision=lax.Precision.HIGHEST,
                        preferred_element_type=F32)
        njp = score.shape[1]
        jidx = lax.broadcasted_iota(jnp.int32, (r8, njp), 1)
        qp_r = past_len + lax.broadcasted_iota(jnp.int32, (r8, njp), 0) % n_tok
        lag = (qp_r >> 6) - jidx
        forced = (jidx == 0) | ((lag >= 0) & (lag < N_LOCAL))
        sc = jnp.where((lag >= 0) & (jidx < nj), jnp.where(forced, BIG, score), -BIG)
        lane = lax.broadcasted_iota(jnp.int32, (r8, LANES), 1)
        cnt = [jnp.zeros((r8, LANES), F32)] * (njp // LANES)
        for i in range(nj):
            si = sc[:, i:i + 1]
            for v in range(njp // LANES):
                blk = sc[:, v * LANES:(v + 1) * LANES]
                if v * LANES > i:
                    beat = jnp.where(si >= blk, 1.0, 0.0)
                elif (v + 1) * LANES <= i:
                    beat = jnp.where(si > blk, 1.0, 0.0)
                else:
                    beat = jnp.where(lane + v * LANES > i, jnp.where(si >= blk, 1.0, 0.0), jnp.where(si > blk, 1.0, 0.0))
                cnt[v] = cnt[v] + beat
        bias = jnp.where(jnp.concatenate(cnt, axis=1) < min(N_SELECT, nj), 0.0, NEG)
        for tj in range(njp // bpt):
            b8 = pltpu.roll(bias, (njp - bpt * tj) % njp, 1)[:, 0:LANES]
            bias_s[tj] = jnp.concatenate([b8] * GROUP, axis=0)
        m_s[...] = jnp.full(m_s.shape, -3e38, F32)
        l_s[...] = jnp.zeros(l_s.shape, F32)
        acc_s[...] = jnp.zeros(acc_s.shape, F32)

    for p in range(pp):
        _page_copy(pt_ref, pool_ref, buf, sem, step * pp + p, slot, p, None).wait()

    def online(sc, pv):
        m_i = m_s[...]
        m_n = jnp.maximum(m_i, jnp.max(sc, axis=-1, keepdims=True))
        alpha = jnp.exp(m_i - m_n)
        pe = jnp.exp(sc - m_n)
        l_s[...] = alpha * l_s[...] + jnp.sum(pe, axis=-1, keepdims=True)
        acc_s[...] = alpha * acc_s[...] + pv(pe.astype(BF16))
        m_s[...] = m_n

    qr = qr_ref[0]
    sc = jnp.concatenate([_mm(qr, buf[slot, p, 0:KV_DIM, :].astype(BF16)) for p in range(pp)], axis=1)
    sc = sc + _mm(bias_s[j].astype(BF16), exp_ref[...])

    def pv_pages(pe):
        o = _nt(pe[:, 0:128], buf[slot, 0, KV_DIM:, :].astype(BF16))
        for p in range(1, pp):
            o = o + _nt(pe[:, p * 128:(p + 1) * 128], buf[slot, p, KV_DIM:, :].astype(BF16))
        return o

    online(sc, pv_pages)

    @pl.when(j == nt - 1)
    def _():
        new = new_ref[0]
        nk = new.shape[0]
        kidx = lax.broadcasted_iota(jnp.int32, (rows, nk), 1)
        last_bias = bias_s[(past_len // SLC_BLOCK) // bpt][:, (past_len // SLC_BLOCK) % bpt:(past_len // SLC_BLOCK) % bpt + 1]
        scn = _nt(qr, new[:, 0:KV_DIM].astype(BF16)) + last_bias
        scn = jnp.where((kidx <= tok) & (kidx < n_tok), scn, NEG)
        online(scn, lambda pe: _mm(pe, new[:, KV_DIM:2 * KV_DIM].astype(BF16)))
        o_slc = acc_s[...] / jnp.maximum(l_s[...], 1e-30)

        nbuf = win_ref.shape[3]
        s1 = _mm(qr, win_ref[0, 0, 0:KV_DIM, :].astype(BF16))
        d1 = tok + nbuf - lax.broadcasted_iota(jnp.int32, (rows, nbuf), 1)
        ok1 = (d1 >= 0) & (d1 < WINDOW)
        s2 = _nt(qr, new[:, 2 * KV_DIM:3 * KV_DIM].astype(BF16))
        d2 = tok - kidx
        ok2 = (d2 >= 0) & (d2 < WINDOW) & (kidx < n_tok)
        s1 = jnp.where(ok1, s1, -BIG)
        s2 = jnp.where(ok2, s2, -BIG)
        mx = jnp.maximum(jnp.max(s1, axis=-1, keepdims=True), jnp.max(s2, axis=-1, keepdims=True))
        e1 = jnp.where(ok1, jnp.exp(s1 - mx), 0.0)
        e2 = jnp.where(ok2, jnp.exp(s2 - mx), 0.0)
        den = jnp.maximum(jnp.sum(e1, axis=-1, keepdims=True) + jnp.sum(e2, axis=-1, keepdims=True), 1e-30)
        o_win = (_nt(e1.astype(BF16), win_ref[0, 0, KV_DIM:, :].astype(BF16))
                 + _mm(e2.astype(BF16), new[:, 3 * KV_DIM:].astype(BF16))) / den
        gt = g_ref[0]
        o_ref[0] = gt[:, 0:1] * ocmp_s[...] + gt[:, 1:2] * o_slc + gt[:, 2:3] * o_win


def _nsa_sample(page_table, qc, qr, gt, kc, pool, expand, pmap, new, win, layer, pp, past_len, n_tok, nj):
    b, npg = page_table.shape
    nt = npg // pp
    rows = GROUP * N_KV * n_tok
    tk = pp * 128
    bpt = tk // SLC_BLOCK
    njp = pmap.shape[1]
    seq = lambda a: pl.BlockSpec((1,) + a.shape[1:], lambda i, j, pt: (i,) + (0,) * (a.ndim - 1))
    full = lambda a: pl.BlockSpec(a.shape, lambda i, j, pt: (0,) * a.ndim)
    gs = pltpu.PrefetchScalarGridSpec(
        num_scalar_prefetch=1,
        grid=(b, nt),
        in_specs=[seq(qc), seq(qr), seq(gt), seq(kc), pl.BlockSpec(memory_space=pl.ANY), full(expand), full(pmap),
                  seq(new), pl.BlockSpec((1, 1) + win.shape[2:], lambda i, j, pt: (layer, i, 0, 0))],
        out_specs=pl.BlockSpec((1, rows, LANES), lambda i, j, pt: (i, 0, 0)),
        scratch_shapes=[pltpu.VMEM((2, pp, 2 * KV_DIM, 128), F32), pltpu.SemaphoreType.DMA((2,)),
                        pltpu.VMEM((njp // bpt, rows, LANES), F32),
                        pltpu.VMEM((rows, 1), F32), pltpu.VMEM((rows, 1), F32),
                        pltpu.VMEM((rows, LANES), F32), pltpu.VMEM((rows, LANES), F32)],
    )
    return pl.pallas_call(
        functools.partial(_nsa_sample_kernel, pp=pp, past_len=past_len, n_tok=n_tok, nj=nj),
        grid_spec=gs,
        out_shape=jax.ShapeDtypeStruct((b, rows, LANES), F32),
        compiler_params=_cparams(("arbitrary", "arbitrary")),
        name="nsa_sample",
    )(page_table.reshape(-1), qc, qr, gt, kc, pool, expand, pmap, new, win)


def _merge_kernel(y_ref, o_ref, w_ref, g_ref, x_ref, out_ref):
    half = y_ref.shape[1]
    m = _mm(y_ref[...], w_ref[0:half, :]) + _mm(o_ref[...], w_ref[half:, :])
    out_ref[...] = x_ref[...] + _rms(m, g_ref[...])


def _merge(y, o, w, g, x, tm):
    m, d = x.shape
    half = y.shape[1]
    return pl.pallas_call(
        _merge_kernel,
        grid=(m // tm,),
        in_specs=[pl.BlockSpec((tm, half), lambda i: (i, 0)), pl.BlockSpec((tm, half), lambda i: (i, 0)),
                  pl.BlockSpec(w.shape, lambda i: (0, 0)), pl.BlockSpec((1, d), lambda i: (0, 0)),
                  pl.BlockSpec((tm, d), lambda i: (i, 0))],
        out_specs=pl.BlockSpec((tm, d), lambda i: (i, 0)),
        out_shape=jax.ShapeDtypeStruct((m, d), F32),
        compiler_params=_cparams(("arbitrary",)),
        name="merge",
    )(y, o, w, g, x)


def _ffn_kernel(x_ref, pre_ref, gpre_ref, gpost_ref, wu_ref, wg_ref, cw_ref, cb_ref, wd_ref,
                out_ref, tail_ref, xn_s, gpad, carry, acc, *, shift, padr):
    j, f = pl.program_id(1), pl.program_id(2)
    nf = pl.num_programs(2)
    tm = x_ref.shape[1]

    @pl.when(f == 0)
    def _():
        xn_s[...] = _rms(x_ref[0], gpre_ref[...]).astype(BF16)
        acc[...] = jnp.zeros(acc.shape, F32)

    @pl.when(j == 0)
    def _():
        gpad[0:padr, :] = pre_ref[0]

    @pl.when(j > 0)
    def _():
        gpad[0:padr, :] = carry[f]

    xn = xn_s[...]
    u = _mm(xn, wu_ref[...])
    gpad[padr:padr + tm, :] = _mm(xn, wg_ref[...])
    gc = cb_ref[...]
    for k in range(3):
        gc = gc + gpad[pl.ds(padr - (2 - k) * shift, tm), :] * cw_ref[k:k + 1, :]
    tail = gpad[tm:tm + padr, :]
    carry[f] = tail
    tail_ref[0, 0] = tail
    acc[...] += _mm((jax.nn.gelu(gc) * u).astype(BF16), wd_ref[...])

    @pl.when(f == nf - 1)
    def _():
        out_ref[0] = x_ref[0] + _rms(acc[...], gpost_ref[...])


def _ffn(x, pre, gpre, gpost, wu, wg, cw, cb, wd, tm, tf, shift):
    b, t, d = x.shape
    dff = wu.shape[1]
    padr = pre.shape[1]
    nt, nf = t // tm, dff // tf
    return pl.pallas_call(
        functools.partial(_ffn_kernel, shift=shift, padr=padr),
        grid=(b, nt, nf),
        in_specs=[pl.BlockSpec((1, tm, d), lambda i, j, f: (i, j, 0)),
                  pl.BlockSpec((1, padr, tf), lambda i, j, f: (i, 0, f)),
                  pl.BlockSpec((1, d), lambda i, j, f: (0, 0)),
                  pl.BlockSpec((1, d), lambda i, j, f: (0, 0)),
                  pl.BlockSpec((d, tf), lambda i, j, f: (0, f)),
                  pl.BlockSpec((d, tf), lambda i, j, f: (0, f)),
                  pl.BlockSpec((3, tf), lambda i, j, f: (0, f)),
                  pl.BlockSpec((1, tf), lambda i, j, f: (0, f)),
                  pl.BlockSpec((tf, d), lambda i, j, f: (f, 0))],
        out_specs=[pl.BlockSpec((1, tm, d), lambda i, j, f: (i, j, 0)),
                   pl.BlockSpec((1, 1, padr, tf), lambda i, j, f: (i, j, 0, f))],
        out_shape=[jax.ShapeDtypeStruct((b, t, d), F32), jax.ShapeDtypeStruct((b, nt, padr, dff), F32)],
        scratch_shapes=[pltpu.VMEM((tm, d), BF16), pltpu.VMEM((padr + tm, tf), F32),
                        pltpu.VMEM((nf, padr, tf), F32), pltpu.VMEM((tm, d), F32)],
        compiler_params=_cparams(("arbitrary", "arbitrary", "arbitrary")),
        name="ffn",
    )(x, pre, gpre, gpost, wu, wg, cw, cb, wd)


def _rot_cols(w):
    d, n = w.shape
    w4 = w.reshape(d, n // HEAD_DIM, 2, HEAD_DIM // 2)
    return jnp.concatenate([-w4[:, :, 1], w4[:, :, 0]], axis=2).reshape(d, n)


def _rope_tables(pos):
    half = HEAD_DIM // 2
    freq = ROPE_THETA ** (-jnp.arange(half, dtype=F32) / half)
    ang = pos.astype(F32)[:, None] * freq[None, :]
    cos, sin = jnp.cos(ang), jnp.sin(ang)
    return jnp.concatenate([cos] * 4, axis=1), jnp.concatenate([sin] * 4, axis=1)


def _block_diag(w):
    n, c, d = w.shape
    return jnp.einsum('ncd,nm->ncmd', w, jnp.eye(n, dtype=w.dtype)).reshape(n * c, n * d)


def _layer_weights(l, p):
    d_model = p['w_in'].shape[1]
    w = p['w_in'][l]
    sizes = [512, 512, 512, 256, 256, 256, 3 * N_HEADS]
    cuts = np.cumsum([0] + sizes)
    xr, gate, q, kvc, kvs, kvw, gl = [w[:, cuts[i]:cuts[i + 1]] for i in range(7)]
    gl_pad = jnp.pad(gl, ((0, 0), (0, LANES - gl.shape[1])))
    wcat = jnp.concatenate([xr, gate, q, _rot_cols(q), kvc, kvs, _rot_cols(kvs[:, :KV_DIM]),
                            kvw, _rot_cols(kvw[:, :KV_DIM]), gl_pad], axis=1).astype(BF16)
    wrow = jnp.concatenate([xr, gate, q, _rot_cols(q), gl_pad], axis=1).astype(BF16)
    wt = jnp.concatenate([kvc, kvs, _rot_cols(kvs[:, :KV_DIM]), kvw, _rot_cols(kvw[:, :KV_DIM])], axis=1).T.astype(BF16)
    row = lambda v: v.reshape(1, -1)
    rg = (p['rg_conv_w'][l], row(p['rg_conv_b'][l]), _block_diag(p['rg_wa'][l]).astype(BF16),
          _block_diag(p['rg_wx'][l]).astype(BF16), row(p['rg_ba'][l]), row(p['rg_bx'][l]), row(p['rg_lambda'][l]))
    sel = lambda a, b: jnp.stack([a, a, b, b])
    eye2 = jnp.eye(N_KV, dtype=F32)
    w1 = jnp.stack([p['cmpk_w1'][l], p['cmpv_w1'][l]])
    w1bd = jnp.einsum('ksdf,he->skhdef', w1, eye2).reshape(2 * CMP_STRIDE, 2, KV_DIM, 2 * KV_DIM)
    pe = sel(p['cmpk_pe'][l], p['cmpv_pe'][l])
    pe = pe.transpose(1, 0, 2).reshape(2 * CMP_STRIDE, 4 * HEAD_DIM)
    b1 = sel(p['cmpk_b1'][l], p['cmpv_b1'][l]).reshape(1, -1)
    w2 = _block_diag(sel(p['cmpk_w2'][l], p['cmpv_w2'][l]))
    pairs = lambda w: w.reshape(CMP_STRIDE // 2, 2, 2, KV_DIM, 2 * KV_DIM).transpose(0, 2, 1, 3, 4).reshape(
        CMP_STRIDE // 2, 2, 2 * KV_DIM, 2 * KV_DIM).astype(BF16)
    cmpw = (pe[:CMP_STRIDE], pe[CMP_STRIDE:], pairs(w1bd[:CMP_STRIDE]), pairs(w1bd[CMP_STRIDE:]), b1, w2.astype(BF16))
    dff = p['ffn_w_in'].shape[2] // 2
    ffn = (p['ffn_w_in'][l][:, :dff].astype(BF16), p['ffn_w_in'][l][:, dff:].astype(BF16),
           p['ffn_conv_w'][l], row(p['ffn_conv_b'][l]), p['ffn_w_down'][l].astype(BF16))
    norms = tuple(row(p[k][l]) for k in ('norm_mix_pre', 'norm_mix_post', 'norm_ffn_pre', 'norm_ffn_post'))
    return dict(wcat=wcat, wrow=wrow, wt=wt, rg=rg, cmp=cmpw, w_out=p['w_out'][l].astype(BF16), ffn=ffn, norms=norms)


def _pool_map(n_slc_pad, n_rows):
    j = np.arange(n_slc_pad)[:, None]
    m = np.arange(n_rows)[None, :]
    r = SLC_BLOCK // CMP_STRIDE
    return jnp.asarray(((m >= r * j) & (m <= r * j + r)).astype(np.float32))


def _heads_major(a, b, t):
    a5 = a.reshape(b, t, 2, N_KV, HEAD_DIM)
    return a5[:, :, 0].transpose(0, 2, 1, 3), a5[:, :, 1].transpose(0, 2, 1, 3)


def _rows_to_state(a_t):
    b, _, t = a_t.shape
    return a_t.reshape(b, 2, N_KV, HEAD_DIM, t).transpose(0, 4, 1, 2, 3)


def _prompt_mixer(hp, lw, tabs):
    b, t, d = hp.shape
    g_pre, g_post = lw['norms'][0], lw['norms'][1]
    xr, gate, qc, qr, gl, kvct, kvst, kvstb, kvwt, kvwtb = _proj_t(
        hp.reshape(b * t, d), g_pre, lw['wrow'], lw['wt'], *tabs, b, _div_tile(t, 512))
    c = xr.shape[1]
    y, htail = _rglru(xr.reshape(b, t, c), gate.reshape(b, t, c), jnp.zeros((b, 8, c), F32),
                      jnp.zeros((b, 1, c), F32), *lw['rg'], _div_tile(t, 512))
    npg = t // 128
    ident = jnp.arange(b * npg, dtype=jnp.int32).reshape(b, npg)
    kcs = _compress(kvct, ident, lw['cmp'], _div_tile(npg, 32), False)
    kck, kcv = _heads_major(kcs, b, t // CMP_STRIDE)
    kcv = jnp.concatenate([kcv, jnp.ones_like(kcv)], axis=-1)
    nqb = t // Q_BLOCK
    n_slc = -(-t // SLC_BLOCK)
    onehot_t = (jnp.arange(HEAD_DIM)[:, None] == jnp.arange(t)[None, :] // SLC_BLOCK).astype(BF16)
    o = _nsa_prompt(qc, qr, gl, kck, kcv, kvstb, kvwtb, onehot_t, _pool_map(n_slc, t // CMP_STRIDE), t)
    hp = _merge(y.reshape(b * t, c), o, lw['w_out'], g_post, hp.reshape(b * t, d), _div_tile(b * t, 512))
    win_buf = min(WINDOW, t)
    st = (_rows_to_state(kvct), _rows_to_state(kvst), _rows_to_state(kvwt[:, :, t - win_buf:]), htail[:, 7],
          xr.reshape(b, t, c)[:, t - 3:])
    return hp.reshape(b, t, d), st


def _sample_mixer(hs, lw, s_cos, s_sin, cmp_pool, slc_pool, page_table, win_t, layer, past_len, rg_h0, rg_conv0):
    b, t, d = hs.shape
    g_pre, g_post = lw['norms'][0], lw['norms'][1]
    xr, gate, qc, qr, kvc, kvs, kvw, gl = _proj(hs.reshape(b * t, d), g_pre, lw['wcat'], s_cos, s_sin, b * t)
    c = xr.shape[1]
    padt = lambda a: jnp.pad(a.reshape(b, t, -1), ((0, 0), (0, 8 - t), (0, 0)))
    pre8 = jnp.pad(rg_conv0, ((0, 0), (5, 0), (0, 0)))
    y8, htail = _rglru(padt(xr), padt(gate), pre8, rg_h0.reshape(b, 1, c), *lw['rg'], 8)
    y = y8[:, :t].reshape(b * t, c)
    npg = page_table.shape[1]
    kcs = _compress(cmp_pool, page_table, lw['cmp'], _div_tile(npg, 32), True)
    nj = -(-(past_len + t) // SLC_BLOCK)
    pp = _div_tile(npg, 16)
    njp = -(-nj // LANES) * LANES
    rows = GROUP * N_KV * t

    def qrows(a):
        a5 = a.reshape(b, t, N_KV, GROUP, HEAD_DIM).transpose(0, 3, 2, 1, 4)
        z = jnp.zeros_like(a5[:, :, 0])
        top = jnp.concatenate([a5[:, :, 0], z], axis=-1)
        bot = jnp.concatenate([z, a5[:, :, 1]], axis=-1)
        return jnp.stack([top, bot], axis=2).reshape(b, rows, LANES)

    gts = gl[:, :3 * N_HEADS].reshape(b, t, N_KV, GROUP, 3).transpose(0, 3, 2, 1, 4).reshape(b, rows, 3)
    gts = jnp.pad(gts, ((0, 0), (0, 0), (0, 5)))
    expand = (jnp.arange(LANES)[:, None] == jnp.arange(pp * 128)[None, :] // SLC_BLOCK).astype(BF16)
    new = jnp.concatenate([kvs.reshape(b, t, 256), kvw.reshape(b, t, 256)], axis=-1)
    new = jnp.pad(new, ((0, 0), (0, 8 - t), (0, 0)))
    o32 = _nsa_sample(page_table, qrows(qc), qrows(qr), gts, kcs, slc_pool, expand, _pool_map(njp, kcs.shape[1]).T,
                      new, win_t, layer, pp, past_len, t, nj)
    o5 = o32.reshape(b, GROUP, N_KV, t, N_KV, HEAD_DIM)
    o = jnp.stack([o5[:, :, 0, :, 0], o5[:, :, 1, :, 1]], axis=1)
    o = o.transpose(0, 3, 1, 2, 4).reshape(b * t, N_HEADS * HEAD_DIM).astype(BF16)
    hs = _merge(y, o, lw['w_out'], g_post, hs.reshape(b * t, d), b * t)
    nbuf = win_t.shape[3]
    win_state_t = jnp.concatenate([win_t[layer], kvw.reshape(b, t, 256).transpose(0, 2, 1)], axis=2)[:, :, -nbuf:]
    st = (kvc.reshape(b, t, 2, N_KV, HEAD_DIM), kvs.reshape(b, t, 2, N_KV, HEAD_DIM), _rows_to_state(win_state_t),
          htail[:, (t - 1) % 8], jnp.concatenate([rg_conv0, xr.reshape(b, t, c)], axis=1)[:, -3:])
    return hs.reshape(b, t, d), st


def kernel(x_prompt, x_sample, cache_cmp_kv, cache_slc_kv, cache_win_kv, state_rg_h, state_rg_conv, state_ffn_conv, page_table, norm_mix_pre, norm_mix_post, norm_ffn_pre, norm_ffn_post, w_in, rg_conv_w, rg_conv_b, rg_wa, rg_ba, rg_wx, rg_bx, rg_lambda, cmpk_pe, cmpk_w1, cmpk_b1, cmpk_w2, cmpv_pe, cmpv_w1, cmpv_b1, cmpv_w2, w_out, ffn_w_in, ffn_conv_w, ffn_conv_b, ffn_w_down):
    params = dict(norm_mix_pre=norm_mix_pre, norm_mix_post=norm_mix_post, norm_ffn_pre=norm_ffn_pre,
                  norm_ffn_post=norm_ffn_post, w_in=w_in, rg_conv_w=rg_conv_w, rg_conv_b=rg_conv_b, rg_wa=rg_wa,
                  rg_ba=rg_ba, rg_wx=rg_wx, rg_bx=rg_bx, rg_lambda=rg_lambda, cmpk_pe=cmpk_pe, cmpk_w1=cmpk_w1,
                  cmpk_b1=cmpk_b1, cmpk_w2=cmpk_w2, cmpv_pe=cmpv_pe, cmpv_w1=cmpv_w1, cmpv_b1=cmpv_b1,
                  cmpv_w2=cmpv_w2, w_out=w_out, ffn_w_in=ffn_w_in, ffn_conv_w=ffn_conv_w, ffn_conv_b=ffn_conv_b,
                  ffn_w_down=ffn_w_down)
    depth = w_in.shape[0]
    bp, tp, d = x_prompt.shape
    bs, ts, _ = x_sample.shape
    past_len = page_table.shape[1] * cache_cmp_kv.shape[2]
    dff = ffn_w_in.shape[2] // 2
    p_cos, p_sin = _rope_tables(jnp.arange(tp))
    p_tabs = (p_cos, p_sin, p_cos.T, p_sin.T)
    s_cos, s_sin = _rope_tables(past_len + jnp.arange(ts))
    s_cos, s_sin = jnp.tile(s_cos, (bs, 1)), jnp.tile(s_sin, (bs, 1))
    n_pool, page = cache_cmp_kv.shape[1], cache_cmp_kv.shape[2]
    pages_t = lambda c: c.transpose(0, 1, 3, 4, 5, 2).reshape(depth * n_pool, 2 * KV_DIM, page)
    cmp_pool, slc_pool = pages_t(cache_cmp_kv), pages_t(cache_slc_kv)
    win_t = cache_win_kv.transpose(0, 1, 3, 4, 5, 2).reshape(depth, bs, 2 * KV_DIM, cache_win_kv.shape[2])
    hp, hs = x_prompt, x_sample
    states_p, states_s = [], []
    for l in range(depth):
        lw = _layer_weights(l, params)
        g_fpre, g_fpost = lw['norms'][2], lw['norms'][3]
        hp, st = _prompt_mixer(hp, lw, p_tabs)
        tmf = _div_tile(tp, 512)
        hp, tail = _ffn(hp, jnp.zeros((bp, 8, dff), F32), g_fpre, g_fpost, *lw['ffn'], tmf, _div_tile(dff, 1024), 1)
        states_p.append(st + (tail[:, -1, 6:],))
        hs, st = _sample_mixer(hs, lw, s_cos, s_sin, cmp_pool, slc_pool, page_table + l * n_pool, win_t, l,
                               past_len, state_rg_h[l], state_rg_conv[l])
        x_tm = hs.transpose(1, 0, 2).reshape(1, ts * bs, d)
        pre_tm = state_ffn_conv[l].transpose(1, 0, 2).reshape(1, 2 * bs, dff)
        out_tm, tail = _ffn(x_tm, pre_tm, g_fpre, g_fpost, *lw['ffn'], ts * bs, _div_tile(dff, 1024), bs)
        hs = out_tm.reshape(ts, bs, d).transpose(1, 0, 2)
        states_s.append(st + (tail.reshape(2, bs, dff).transpose(1, 0, 2),))
    p_cmp, p_slc, p_win, p_rgh, p_rgc, p_ffc = [jnp.stack(z) for z in zip(*states_p)]
    s_cmp, s_slc, s_win, s_rgh, s_rgc, s_ffc = [jnp.stack(z) for z in zip(*states_s)]
    return (hp, hs, p_cmp, s_cmp, p_slc, s_slc, p_win, s_win, p_rgh, s_rgh, p_rgc, s_rgc, p_ffc, s_ffc)
```

```python
import functools

import jax
import jax.numpy as jnp
import numpy as np
from jax import lax
from jax.experimental import pallas as pl
from jax.experimental.pallas import tpu as pltpu

F32 = jnp.float32
BF16 = jnp.bfloat16

HEAD_DIM = 64
N_KV = 2
GROUP = 4
N_HEADS = N_KV * GROUP
KV_DIM = N_KV * HEAD_DIM
CMP_STRIDE = 16
SLC_BLOCK = 64
N_SELECT = 16
N_LOCAL = 2
WINDOW = 512
Q_BLOCK = 128
RG_C = 8.0
ROPE_THETA = 10000.0
EPS = 1e-6

V7X_VMEM_BYTES = 64 * 1024 * 1024
VMEM_LIMIT = 48 * 1024 * 1024
SUBLANES = 8
LANES = 128

LOG2E = 1.4426950408889634
NEG = -2.0 ** 100
BIG = 1e30


def _cparams(sem):
    return pltpu.CompilerParams(dimension_semantics=sem, vmem_limit_bytes=VMEM_LIMIT)


def _div_tile(n, pref):
    t = min(n, pref)
    while n % t:
        t -= 1
    return t


def _rms(x, g):
    y = x * lax.rsqrt(jnp.mean(x * x, axis=-1, keepdims=True) + EPS)
    return y * g


def _sigmoid(x):
    return 0.5 * jnp.tanh(0.5 * x) + 0.5


def _nt(a, b, precision=None):
    return lax.dot_general(a, b, (((1,), (1,)), ((), ())), preferred_element_type=F32, precision=precision)


def _mm(a, b):
    return jnp.dot(a, b, preferred_element_type=F32)


C_XR, C_GATE, C_Q, C_QROT, C_KVC, C_KVS, C_KSROT, C_KVW, C_KWROT, C_GL, C_END = (
    0, 512, 1024, 1536, 2048, 2304, 2560, 2688, 2944, 3072, 3200)


def _proj_kernel(x_ref, g_ref, w_ref, cos_ref, sin_ref,
                 xr_ref, gate_ref, qc_ref, qr_ref, kvc_ref, kvs_ref, kvw_ref, gl_ref):
    xn = _rms(x_ref[...], g_ref[...]).astype(BF16)

    def mm(c0, c1):
        return _mm(xn, w_ref[:, c0:c1])

    cos = cos_ref[...]
    sin = sin_ref[...]
    xr_ref[...] = mm(C_XR, C_GATE)
    gate_ref[...] = mm(C_GATE, C_Q)
    q = mm(C_Q, C_QROT)
    qrot = mm(C_QROT, C_KVC)
    cos4 = jnp.concatenate([cos] * 4, axis=1)
    sin4 = jnp.concatenate([sin] * 4, axis=1)
    scale = HEAD_DIM ** -0.5
    qc_ref[...] = (q * scale).astype(BF16)
    qr_ref[...] = ((q * cos4 + qrot * sin4) * scale).astype(BF16)
    kvc_ref[...] = mm(C_KVC, C_KVS)
    kvs = mm(C_KVS, C_KSROT)
    ksrot = mm(C_KSROT, C_KVW)
    kvs_ref[:, 0:KV_DIM] = kvs[:, 0:KV_DIM] * cos + ksrot * sin
    kvs_ref[:, KV_DIM:] = kvs[:, KV_DIM:]
    kvw = mm(C_KVW, C_KWROT)
    kwrot = mm(C_KWROT, C_GL)
    kvw_ref[:, 0:KV_DIM] = kvw[:, 0:KV_DIM] * cos + kwrot * sin
    kvw_ref[:, KV_DIM:] = kvw[:, KV_DIM:]
    gl_ref[...] = jax.nn.sigmoid(mm(C_GL, C_END))


R_KVC, R_KVS, R_KVW = 0, 256, 512


def _proj_t_kernel(x_ref, g_ref, w_ref, wt_ref, cos_ref, sin_ref, cost_ref, sint_ref,
                   xr_ref, gate_ref, qc_ref, qr_ref, gl_ref, kvct_ref, kvst_ref, kvstb_ref, kvwt_ref, kvwtb_ref):
    xn = _rms(x_ref[...], g_ref[...]).astype(BF16)

    def mm(c0, c1):
        return _mm(xn, w_ref[:, c0:c1])

    def mt(r0, r1):
        return _nt(wt_ref[r0:r1, :], xn)

    xr_ref[...] = mm(0, 512)
    gate_ref[...] = mm(512, 1024)
    q = mm(1024, 1536)
    half = HEAD_DIM // 2
    lane = lax.broadcasted_iota(jnp.int32, q.shape, 1)
    qrot = jnp.where((lane & (HEAD_DIM - 1)) < half, -pltpu.roll(q, q.shape[1] - half, 1), pltpu.roll(q, half, 1))
    cos4 = jnp.concatenate([cos_ref[...]] * 4, axis=1)
    sin4 = jnp.concatenate([sin_ref[...]] * 4, axis=1)
    scale = HEAD_DIM ** -0.5 * LOG2E
    qc_ref[...] = (q * scale).astype(BF16)
    qr_ref[...] = ((q * cos4 + qrot * sin4) * scale).astype(BF16)
    gl_ref[...] = jax.nn.sigmoid(mm(1536, 1664))
    kvct_ref[0] = mt(R_KVC, R_KVS)
    cost, sint = cost_ref[...], sint_ref[...]
    for lo, f_ref, b_ref in ((R_KVS, kvst_ref, kvstb_ref), (R_KVW, kvwt_ref, kvwtb_ref)):
        kv = mt(lo, lo + 2 * KV_DIM)
        rot = jnp.concatenate([piece for h in range(N_KV) for piece in
                               (-kv[h * HEAD_DIM + half:(h + 1) * HEAD_DIM], kv[h * HEAD_DIM:h * HEAD_DIM + half])],
                              axis=0)
        k = kv[0:KV_DIM] * cost + rot * sint
        f_ref[0, 0:KV_DIM, :] = k
        f_ref[0, KV_DIM:, :] = kv[KV_DIM:]
        b_ref[0, 0:KV_DIM, :] = k.astype(BF16)
        b_ref[0, KV_DIM:, :] = kv[KV_DIM:].astype(BF16)


def _proj_t(x2d, g, wrow, wt, cos, sin, cost, sint, b, tm):
    m, d = x2d.shape
    t = m // b
    nper = t // tm
    row = lambda w: pl.BlockSpec((tm, w), lambda i: (i, 0))
    tab = pl.BlockSpec((tm, LANES), lambda i: (i % nper, 0))
    tabt = pl.BlockSpec((KV_DIM, tm), lambda i: (0, i % nper))
    tr = pl.BlockSpec((1, 2 * KV_DIM, tm), lambda i: (i // nper, 0, i % nper))
    full = lambda a: pl.BlockSpec(a.shape, lambda i: (0,) * a.ndim)
    widths = (512, 512, 512, 512, 128)
    dtypes = (F32, F32, BF16, BF16, F32)
    tdt = (F32, F32, BF16, F32, BF16)
    return pl.pallas_call(
        _proj_t_kernel,
        grid=(m // tm,),
        in_specs=[row(d), full(g), full(wrow), full(wt), tab, tab, tabt, tabt],
        out_specs=[row(w) for w in widths] + [tr] * 5,
        out_shape=[jax.ShapeDtypeStruct((m, w), dt) for w, dt in zip(widths, dtypes)]
        + [jax.ShapeDtypeStruct((b, 2 * KV_DIM, t), dt) for dt in tdt],
        compiler_params=_cparams(("arbitrary",)),
        name="proj_t",
    )(x2d, g, wrow, wt, cos, sin, cost, sint)


def _proj(x2d, g, wcat, cos, sin, tm):
    m, d = x2d.shape
    nper = cos.shape[0] // tm
    row = lambda w: pl.BlockSpec((tm, w), lambda i: (i, 0))
    tab = pl.BlockSpec((tm, LANES), lambda i: (i % nper, 0))
    widths = (512, 512, 512, 512, 256, 256, 256, 128)
    dtypes = (F32, F32, BF16, BF16, F32, F32, F32, F32)
    return pl.pallas_call(
        _proj_kernel,
        grid=(m // tm,),
        in_specs=[row(d), pl.BlockSpec((1, d), lambda i: (0, 0)),
                  pl.BlockSpec(wcat.shape, lambda i: (0, 0)), tab, tab],
        out_specs=[row(w) for w in widths],
        out_shape=[jax.ShapeDtypeStruct((m, w), dt) for w, dt in zip(widths, dtypes)],
        compiler_params=_cparams(("arbitrary",)),
        name="proj",
    )(x2d, g, wcat, cos, sin)


def _rglru_kernel(xr_ref, gate_ref, pre_ref, h0_ref, cw_ref, cb_ref, wa_ref, wx_ref, ba_ref, bx_ref, lam_ref,
                  y_ref, htail_ref, xpad, hc, a_s, b_s):
    tt = xr_ref.shape[1]
    j = pl.program_id(1)

    @pl.when(j == 0)
    def _():
        xpad[0:8, :] = pre_ref[0]
        hc[...] = jnp.broadcast_to(h0_ref[0], hc.shape)

    @pl.when(j > 0)
    def _():
        xpad[0:8, :] = xpad[tt:tt + 8, :]

    xpad[8:8 + tt, :] = xr_ref[0]
    xc = cb_ref[...]
    for k in range(4):
        xc = xc + xpad[pl.ds(5 + k, tt), :] * cw_ref[k:k + 1, :]
    xb = xc.astype(BF16)
    r = _sigmoid(_mm(xb, wa_ref[...]) + ba_ref[...])
    i = _sigmoid(_mm(xb, wx_ref[...]) + bx_ref[...])
    log_a = RG_C * r * jax.nn.log_sigmoid(lam_ref[...])
    a_s[...] = jnp.exp(log_a)
    t = jnp.tanh(log_a)
    b_s[...] = jnp.sqrt(-2.0 * t / (1.0 - t)) * (i * xc)

    row = lax.broadcasted_iota(jnp.int32, (8, a_s.shape[1]), 0)

    def body(gi, h):
        r0 = pl.multiple_of(gi * 8, 8)
        av = a_s[pl.ds(r0, 8), :]
        bv = b_s[pl.ds(r0, 8), :]
        for s in (1, 2, 4):
            keep = row >= s
            a_sh = pltpu.roll(av, s, 0)
            b_sh = pltpu.roll(bv, s, 0)
            bv = jnp.where(keep, av * b_sh + bv, bv)
            av = jnp.where(keep, av * a_sh, av)
        hs = av * h + bv
        b_s[pl.ds(r0, 8), :] = hs
        return jnp.broadcast_to(hs[7:8, :], hs.shape)

    h = lax.fori_loop(0, tt // 8, body, hc[...])
    hc[...] = h
    htail_ref[0] = b_s[tt - 8:tt, :]
    y_ref[0] = (b_s[...] * jax.nn.gelu(gate_ref[0])).astype(y_ref.dtype)


def _rglru(xr, gate, pre8, h0, cw, cb, wa, wx, ba, bx, lam, tt):
    b, t, c = xr.shape
    full = lambda a: pl.BlockSpec(a.shape, lambda i, j: (0,) * a.ndim)
    return pl.pallas_call(
        _rglru_kernel,
        grid=(b, t // tt),
        in_specs=[pl.BlockSpec((1, tt, c), lambda i, j: (i, j, 0)),
                  pl.BlockSpec((1, tt, c), lambda i, j: (i, j, 0)),
                  pl.BlockSpec((1, 8, c), lambda i, j: (i, 0, 0)),
                  pl.BlockSpec((1, 1, c), lambda i, j: (i, 0, 0)),
                  full(cw), full(cb), full(wa), full(wx), full(ba), full(bx), full(lam)],
        out_specs=[pl.BlockSpec((1, tt, c), lambda i, j: (i, j, 0)),
                   pl.BlockSpec((1, 8, c), lambda i, j: (i, 0, 0))],
        out_shape=[jax.ShapeDtypeStruct((b, t, c), BF16), jax.ShapeDtypeStruct((b, 8, c), F32)],
        scratch_shapes=[pltpu.VMEM((tt + 8, c), F32), pltpu.VMEM((8, c), F32),
                        pltpu.VMEM((tt, c), F32), pltpu.VMEM((tt, c), F32)],
        compiler_params=_cparams(("arbitrary", "arbitrary")),
        name="rglru",
    )(xr, gate, pre8, h0, cw, cb, wa, wx, ba, bx, lam)


def _page_copy(pt_ref, src_ref, buf, sem, idx, slot, p, seq_pages):
    page = pt_ref[idx]
    if seq_pages is None:
        src = src_ref.at[page]
    else:
        src = src_ref.at[page // seq_pages, :, pl.ds(pl.multiple_of((page % seq_pages) * 128, 128), 128)]
    return pltpu.make_async_copy(src, buf.at[slot, p], sem.at[slot])


def _cmp_kernel(pt_ref, pool_ref, pea_ref, peb_ref, w1a_ref, w1b_ref, b1_ref, w2_ref, out_ref,
                buf, sem, xbuf, ua_s, *, pp, seq_pages):
    nb, nt = pl.num_programs(0), pl.num_programs(1)
    j = pl.program_id(1)
    step = pl.program_id(0) * nt + j
    slot = step % 2
    m = pp * 8

    def fetch(st, sl):
        for p in range(pp):
            _page_copy(pt_ref, pool_ref, buf, sem, st * pp + p, sl, p, seq_pages).start()

    @pl.when(step == 0)
    def _():
        fetch(step, slot)

    @pl.when(step + 1 < nb * nt)
    def _():
        fetch(step + 1, 1 - slot)

    for p in range(pp):
        _page_copy(pt_ref, pool_ref, buf, sem, step * pp + p, slot, p, seq_pages).wait()

    n_part = xbuf.shape[0]
    ph = pp // n_part
    mh = ph * 8
    part_a, part_b = [], []
    for part in range(n_part):
        for p in range(ph):
            for kv in range(2):
                xbuf[part, kv, p * 128:(p + 1) * 128, :] = buf[slot, part * ph + p, kv * KV_DIM:(kv + 1) * KV_DIM, :].T
        acc_a = [jnp.zeros((mh, 256), F32)] * 2
        acc_b = [jnp.zeros((mh, 256), F32)] * 2
        for s in range(0, CMP_STRIDE, 2):
            for kv in range(2):
                x0 = xbuf[part, kv, pl.ds(s, mh, stride=CMP_STRIDE), :]
                x1 = xbuf[part, kv, pl.ds(s + 1, mh, stride=CMP_STRIDE), :]
                lanes = slice(kv * KV_DIM, (kv + 1) * KV_DIM)

                def pair(pe_ref):
                    return jnp.concatenate([(x0 + pe_ref[s:s + 1, lanes]).astype(BF16),
                                            (x1 + pe_ref[s + 1:s + 2, lanes]).astype(BF16)], axis=1)

                acc_a[kv] = acc_a[kv] + _mm(pair(pea_ref), w1a_ref[s // 2, kv])
                acc_b[kv] = acc_b[kv] + _mm(pair(peb_ref), w1b_ref[s // 2, kv])
        part_a.append(jnp.concatenate(acc_a, axis=1))
        part_b.append(jnp.concatenate(acc_b, axis=1))
    acc_a = jnp.concatenate(part_a, axis=0)
    acc_b = jnp.concatenate(part_b, axis=0)

    @pl.when(j == 0)
    def _():
        ua_s[0:8, :] = jnp.zeros((8, 512), F32)

    @pl.when(j > 0)
    def _():
        ua_s[0:8, :] = ua_s[m:m + 8, :]

    ua_s[8:8 + m, :] = acc_a
    hid = jax.nn.gelu(ua_s[pl.ds(7, m), :] + acc_b + b1_ref[...])
    out_ref[0] = _mm(hid.astype(BF16), w2_ref[...]).astype(out_ref.dtype)


def _compress(pool, page_table, cw, pp, paged):
    b, npg = page_table.shape
    nt = npg // pp
    m = pp * 8
    n_part = 2 if pp % 2 == 0 else 1
    full = lambda a: pl.BlockSpec(a.shape, lambda i, j, pt: (0,) * a.ndim)
    pea, peb, w1a, w1b, b1, w2 = cw
    gs = pltpu.PrefetchScalarGridSpec(
        num_scalar_prefetch=1,
        grid=(b, nt),
        in_specs=[pl.BlockSpec(memory_space=pl.ANY), full(pea), full(peb), full(w1a), full(w1b), full(b1), full(w2)],
        out_specs=pl.BlockSpec((1, m, 256), lambda i, j, pt: (i, j, 0)),
        scratch_shapes=[pltpu.VMEM((2, pp, 2 * KV_DIM, 128), F32), pltpu.SemaphoreType.DMA((2,)),
                        pltpu.VMEM((n_part, 2, pp // n_part * 128, KV_DIM), F32), pltpu.VMEM((m + 8, 512), F32)],
    )
    return pl.pallas_call(
        functools.partial(_cmp_kernel, pp=pp, seq_pages=None if paged else npg),
        grid_spec=gs,
        out_shape=jax.ShapeDtypeStruct((b, nt * m, 256), BF16),
        compiler_params=_cparams(("arbitrary", "arbitrary")),
        name="compress",
    )(page_table.reshape(-1), pool, pea, peb, w1a, w1b, b1, w2)


def _softmax_rows(s, mask):
    sm = jnp.where(mask, s, -BIG)
    mx = jnp.max(sm, axis=-1, keepdims=True)
    e = jnp.where(mask, jnp.exp(sm - mx), 0.0)
    return e / jnp.maximum(jnp.sum(e, axis=-1, keepdims=True), 1e-30)


def _nsa_prompt_kernel(qc_ref, qr_ref, g_ref, kck_ref, kcv_ref, kvs_ref, kvw_ref, oh_ref, pool_ref,
                       o_ref, sc_s, rank_s, *, t_len, kc_len, wl):
    qb = pl.program_id(1)
    start = qb * Q_BLOCK
    nc = kck_ref.shape[2]
    nj = pool_ref.shape[0]
    rows = GROUP * Q_BLOCK
    qpos = start + (lax.broadcasted_iota(jnp.int32, (rows, 1), 0) & (Q_BLOCK - 1))

    heads = range(N_KV)
    k_rows = [slice(h * HEAD_DIM, (h + 1) * HEAD_DIM) for h in heads]
    v_rows = [slice(KV_DIM + h * HEAD_DIM, KV_DIM + (h + 1) * HEAD_DIM) for h in heads]
    n_chunks = (start + Q_BLOCK + kc_len - 1) // kc_len
    n_full = n_chunks - 1
    k_diag = pl.multiple_of(n_full * kc_len, kc_len)
    base = pl.multiple_of(jnp.maximum(start + Q_BLOCK - wl, 0), Q_BLOCK)

    qpos1 = start + lax.broadcasted_iota(jnp.int32, (Q_BLOCK, 1), 0)
    mrow = lax.broadcasted_iota(jnp.int32, (Q_BLOCK, nc), 1)
    cbias = jnp.where((mrow >= 1) & (mrow * CMP_STRIDE + (CMP_STRIDE - 1) <= qpos1), 0.0, NEG)
    dbias = jnp.where(k_diag + lax.broadcasted_iota(jnp.int32, (Q_BLOCK, kc_len), 1) <= qpos1, 0.0, NEG)
    dist = qpos1 - (base + lax.broadcasted_iota(jnp.int32, (Q_BLOCK, wl), 1))
    wbias = jnp.where((dist >= 0) & (dist < WINDOW), 0.0, NEG)
    any_cmp = jnp.where(qpos >= 2 * CMP_STRIDE - 1, 1.0, 0.0)

    def add_bias(s, bias):
        return (s.reshape(GROUP, Q_BLOCK, s.shape[1]) + bias[None]).reshape(s.shape)

    def with_ones(vt):
        return jnp.concatenate([vt, jnp.ones(vt.shape, vt.dtype)], axis=0)

    def head_rows(ref, h):
        return jnp.concatenate([ref[:, (h * GROUP + g) * HEAD_DIM:(h * GROUP + g + 1) * HEAD_DIM]
                                for g in range(GROUP)], axis=0)

    def gate_rows(h, branch):
        gl = g_ref[...]
        cols = [(h * GROUP + g) * 3 + branch for g in range(GROUP)]
        return jnp.concatenate([gl[:, c:c + 1] for c in cols], axis=0)

    r_cmp, inv_cmp = [], []
    jidx = lax.broadcasted_iota(jnp.int32, (nj, Q_BLOCK), 0)
    lag = ((start + lax.broadcasted_iota(jnp.int32, (nj, Q_BLOCK), 1)) >> 6) - jidx
    forced = (jidx == 0) | ((lag >= 0) & (lag < N_LOCAL))
    for h in heads:
        s = add_bias(_nt(head_rows(qc_ref, h), kck_ref[0, h]), cbias)
        e = jnp.exp2(s - jnp.max(s, axis=-1, keepdims=True))
        r = _mm(e.astype(BF16), kcv_ref[0, h])
        inv = any_cmp / jnp.maximum(r[:, HEAD_DIM:HEAD_DIM + 1], 1e-30)
        p = e * inv
        psum = p[0:Q_BLOCK] + p[Q_BLOCK:2 * Q_BLOCK] + p[2 * Q_BLOCK:3 * Q_BLOCK] + p[3 * Q_BLOCK:]
        score_t = _nt(pool_ref[...], psum, precision=lax.Precision.HIGHEST)
        sc_s[h] = jnp.where(lag >= 0, jnp.where(forced, BIG, score_t), -BIG)
        rank_s[h] = jnp.zeros((nj, Q_BLOCK), F32)
        r_cmp.append(r)
        inv_cmp.append(inv)

    sub = lax.broadcasted_iota(jnp.int32, (SUBLANES, Q_BLOCK), 0)
    n_grp = nj // SUBLANES
    for h in heads:
        scv = sc_s[h]
        cnt = [jnp.zeros((SUBLANES, Q_BLOCK), F32)] * n_grp
        for i in range(nj):
            gi, ii = divmod(i, SUBLANES)
            si = scv[i:i + 1, :]
            for r in range(n_grp):
                blk = scv[r * SUBLANES:(r + 1) * SUBLANES]
                if r == gi:
                    beat = jnp.where(sub > ii, jnp.where(si >= blk, 1.0, 0.0), jnp.where(si > blk, 1.0, 0.0))
                else:
                    beat = jnp.where(si >= blk, 1.0, 0.0) if r > gi else jnp.where(si > blk, 1.0, 0.0)
                cnt[r] = cnt[r] + beat
        rank_s[h] = jnp.concatenate(cnt, axis=0)

    q4r = [head_rows(qr_ref, h) for h in heads]
    qaug = []
    for h in heads:
        bias_t = jnp.where(rank_s[h] < min(N_SELECT, nj), 0.0, NEG)
        pad_t = jnp.concatenate([jnp.zeros((HEAD_DIM, Q_BLOCK), F32), bias_t]
                                + ([jnp.zeros((HEAD_DIM - nj, Q_BLOCK), F32)] if nj < HEAD_DIM else []), axis=0)
        bias4 = jnp.concatenate([pad_t.T[:, HEAD_DIM:]] * GROUP, axis=0)
        qaug.append(jnp.concatenate([q4r[h].astype(F32), bias4], axis=1).astype(BF16))

    den = lambda r: jnp.maximum(r[:, HEAD_DIM:HEAD_DIM + 1], 1e-30)
    o_cw = []
    for h in heads:
        sw = add_bias(_mm(q4r[h], kvw_ref[0, k_rows[h], pl.ds(base, wl)]), wbias)
        ew = jnp.exp2(sw - jnp.max(sw, axis=-1, keepdims=True)).astype(BF16)
        r_win = _nt(ew, with_ones(kvw_ref[0, v_rows[h], pl.ds(base, wl)]))
        o_cw.append(r_cmp[h] * (gate_rows(h, 0) * inv_cmp[h]) + r_win * (gate_rows(h, 2) / den(r_win)))

    def chunk(k0, state, diagonal):
        out = []
        for h in heads:
            m_i, acc = state[h]
            kaug = jnp.concatenate([kvs_ref[0, k_rows[h], pl.ds(k0, kc_len)], oh_ref[:, pl.ds(k0, kc_len)]], axis=0)
            sc = _mm(qaug[h], kaug)
            if diagonal:
                sc = add_bias(sc, dbias)
            m_n = jnp.maximum(m_i, jnp.max(sc, axis=-1, keepdims=True))
            pe = jnp.exp2(sc - m_n).astype(BF16)
            acc = jnp.exp2(m_i - m_n) * acc + _nt(pe, with_ones(kvs_ref[0, v_rows[h], pl.ds(k0, kc_len)]))
            out.append((m_n, acc))
        return tuple(out)

    init = tuple((jnp.full((rows, 1), -3e38, F32), jnp.zeros((rows, LANES), F32)) for _ in heads)
    state = chunk(k_diag, init, True)
    state = lax.fori_loop(0, n_full, lambda c, st: chunk(pl.multiple_of(c * kc_len, kc_len), st, False), state)

    for h in heads:
        r_slc = state[h][1]
        o = o_cw[h] + r_slc * (gate_rows(h, 1) / den(r_slc))
        for g in range(GROUP):
            lanes = slice((h * GROUP + g) * HEAD_DIM, (h * GROUP + g + 1) * HEAD_DIM)
            o_ref[:, lanes] = o[g * Q_BLOCK:(g + 1) * Q_BLOCK, 0:HEAD_DIM].astype(o_ref.dtype)


def _nsa_prompt(qc, qr, gt, kck, kcv, kvst, kvwt, onehot_t, pool_t, t_len):
    b = kck.shape[0]
    nqb = t_len // Q_BLOCK
    kc_len = _div_tile(t_len, 512)
    wl = min(WINDOW + Q_BLOCK, t_len)
    qspec = lambda w: pl.BlockSpec((Q_BLOCK, w), lambda i, j: (i * nqb + j, 0))
    seq = lambda a: pl.BlockSpec((1,) + a.shape[1:], lambda i, j: (i,) + (0,) * (a.ndim - 1))
    return pl.pallas_call(
        functools.partial(_nsa_prompt_kernel, t_len=t_len, kc_len=kc_len, wl=wl),
        grid=(b, nqb),
        in_specs=[qspec(qc.shape[1]), qspec(qr.shape[1]), qspec(gt.shape[1]), seq(kck), seq(kcv), seq(kvst), seq(kvwt),
                  pl.BlockSpec(onehot_t.shape, lambda i, j: (0, 0)), pl.BlockSpec(pool_t.shape, lambda i, j: (0, 0))],
        out_specs=qspec(qc.shape[1]),
        out_shape=jax.ShapeDtypeStruct(qc.shape, BF16),
        scratch_shapes=[pltpu.VMEM((N_KV,) + pool_t.shape[:1] + (Q_BLOCK,), F32)] * 2,
        compiler_params=_cparams(("arbitrary", "arbitrary")),
        name="nsa_prompt",
    )(qc, qr, gt, kck, kcv, kvst, kvwt, onehot_t, pool_t)


def _nsa_sample_kernel(pt_ref, qc_ref, qr_ref, g_ref, kc_ref, pool_ref, exp_ref, pmap_ref, new_ref, win_ref,
                       o_ref, buf, sem, bias_s, m_s, l_s, acc_s, ocmp_s, *, pp, past_len, n_tok, nj):
    nb, nt = pl.num_programs(0), pl.num_programs(1)
    j = pl.program_id(1)
    step = pl.program_id(0) * nt + j
    slot = step % 2
    rows = GROUP * N_KV * n_tok
    tk = pp * 128
    bpt = tk // SLC_BLOCK

    def fetch(st, sl):
        for p in range(pp):
            _page_copy(pt_ref, pool_ref, buf, sem, st * pp + p, sl, p, None).start()

    @pl.when(step == 0)
    def _():
        fetch(step, slot)

    @pl.when(step + 1 < nb * nt)
    def _():
        fetch(step + 1, 1 - slot)

    rid = lax.broadcasted_iota(jnp.int32, (rows, 1), 0)
    tok = rid % n_tok
    qpos = past_len + tok

    @pl.when(j == 0)
    def _():
        kc = kc_ref[0]
        nc = kc.shape[0]
        s = _nt(qc_ref[0], kc[:, 0:KV_DIM])
        mrow = lax.broadcasted_iota(jnp.int32, (rows, nc), 1)
        valid = (mrow >= 1) & (mrow * CMP_STRIDE + (CMP_STRIDE - 1) <= qpos)
        p = _softmax_rows(s, valid)
        ocmp_s[...] = _mm(p.astype(BF16), kc[:, KV_DIM:])
        r8 = N_KV * n_tok
        psum = p[0:r8] + p[r8:2 * r8] + p[2 * r8:3 * r8] + p[3 * r8:]
        score = jnp.dot(psum, pmap_ref[...], precision=lax.Precision.HIGHEST,
                        preferred_element_type=F32)
        njp = score.shape[1]
        jidx = lax.broadcasted_iota(jnp.int32, (r8, njp), 1)
        qp_r = past_len + lax.broadcasted_iota(jnp.int32, (r8, njp), 0) % n_tok
        lag = (qp_r >> 6) - jidx
        forced = (jidx == 0) | ((lag >= 0) & (lag < N_LOCAL))
        sc = jnp.where((lag >= 0) & (jidx < nj), jnp.where(forced, BIG, score), -BIG)
        lane = lax.broadcasted_iota(jnp.int32, (r8, LANES), 1)
        cnt = [jnp.zeros((r8, LANES), F32)] * (njp // LANES)
        for i in range(nj):
            si = sc[:, i:i + 1]
            for v in range(njp // LANES):
                blk = sc[:, v * LANES:(v + 1) * LANES]
                if v * LANES > i:
                    beat = jnp.where(si >= blk, 1.0, 0.0)
                elif (v + 1) * LANES <= i:
                    beat = jnp.where(si > blk, 1.0, 0.0)
                else:
                    beat = jnp.where(lane + v * LANES > i, jnp.where(si >= blk, 1.0, 0.0), jnp.where(si > blk, 1.0, 0.0))
                cnt[v] = cnt[v] + beat
        bias = jnp.where(jnp.concatenate(cnt, axis=1) < min(N_SELECT, nj), 0.0, NEG)
        for tj in range(njp // bpt):
            b8 = pltpu.roll(bias, (njp - bpt * tj) % njp, 1)[:, 0:LANES]
            bias_s[tj] = jnp.concatenate([b8] * GROUP, axis=0)
        m_s[...] = jnp.full(m_s.shape, -3e38, F32)
        l_s[...] = jnp.zeros(l_s.shape, F32)
        acc_s[...] = jnp.zeros(acc_s.shape, F32)

    for p in range(pp):
        _page_copy(pt_ref, pool_ref, buf, sem, step * pp + p, slot, p, None).wait()

    def online(sc, pv):
        m_i = m_s[...]
        m_n = jnp.maximum(m_i, jnp.max(sc, axis=-1, keepdims=True))
        alpha = jnp.exp(m_i - m_n)
        pe = jnp.exp(sc - m_n)
        l_s[...] = alpha * l_s[...] + jnp.sum(pe, axis=-1, keepdims=True)
        acc_s[...] = alpha * acc_s[...] + pv(pe.astype(BF16))
        m_s[...] = m_n

    qr = qr_ref[0]
    sc = jnp.concatenate([_mm(qr, buf[slot, p, 0:KV_DIM, :].astype(BF16)) for p in range(pp)], axis=1)
    sc = sc + _mm(bias_s[j].astype(BF16), exp_ref[...])

    def pv_pages(pe):
        o = _nt(pe[:, 0:128], buf[slot, 0, KV_DIM:, :].astype(BF16))
        for p in range(1, pp):
            o = o + _nt(pe[:, p * 128:(p + 1) * 128], buf[slot, p, KV_DIM:, :].astype(BF16))
        return o

    online(sc, pv_pages)

    @pl.when(j == nt - 1)
    def _():
        new = new_ref[0]
        nk = new.shape[0]
        kidx = lax.broadcasted_iota(jnp.int32, (rows, nk), 1)
        last_bias = bias_s[(past_len // SLC_BLOCK) // bpt][:, (past_len // SLC_BLOCK) % bpt:(past_len // SLC_BLOCK) % bpt + 1]
        scn = _nt(qr, new[:, 0:KV_DIM].astype(BF16)) + last_bias
        scn = jnp.where((kidx <= tok) & (kidx < n_tok), scn, NEG)
        online(scn, lambda pe: _mm(pe, new[:, KV_DIM:2 * KV_DIM].astype(BF16)))
        o_slc = acc_s[...] / jnp.maximum(l_s[...], 1e-30)

        nbuf = win_ref.shape[3]
        s1 = _mm(qr, win_ref[0, 0, 0:KV_DIM, :].astype(BF16))
        d1 = tok + nbuf - lax.broadcasted_iota(jnp.int32, (rows, nbuf), 1)
        ok1 = (d1 >= 0) & (d1 < WINDOW)
        s2 = _nt(qr, new[:, 2 * KV_DIM:3 * KV_DIM].astype(BF16))
        d2 = tok - kidx
        ok2 = (d2 >= 0) & (d2 < WINDOW) & (kidx < n_tok)
        s1 = jnp.where(ok1, s1, -BIG)
        s2 = jnp.where(ok2, s2, -BIG)
        mx = jnp.maximum(jnp.max(s1, axis=-1, keepdims=True), jnp.max(s2, axis=-1, keepdims=True))
        e1 = jnp.where(ok1, jnp.exp(s1 - mx), 0.0)
        e2 = jnp.where(ok2, jnp.exp(s2 - mx), 0.0)
        den = jnp.maximum(jnp.sum(e1, axis=-1, keepdims=True) + jnp.sum(e2, axis=-1, keepdims=True), 1e-30)
        o_win = (_nt(e1.astype(BF16), win_ref[0, 0, KV_DIM:, :].astype(BF16))
                 + _mm(e2.astype(BF16), new[:, 3 * KV_DIM:].astype(BF16))) / den
        gt = g_ref[0]
        o_ref[0] = gt[:, 0:1] * ocmp_s[...] + gt[:, 1:2] * o_slc + gt[:, 2:3] * o_win


def _nsa_sample(page_table, qc, qr, gt, kc, pool, expand, pmap, new, win, layer, pp, past_len, n_tok, nj):
    b, npg = page_table.shape
    nt = npg // pp
    rows = GROUP * N_KV * n_tok
    tk = pp * 128
    bpt = tk // SLC_BLOCK
    njp = pmap.shape[1]
    seq = lambda a: pl.BlockSpec((1,) + a.shape[1:], lambda i, j, pt: (i,) + (0,) * (a.ndim - 1))
    full = lambda a: pl.BlockSpec(a.shape, lambda i, j, pt: (0,) * a.ndim)
    gs = pltpu.PrefetchScalarGridSpec(
        num_scalar_prefetch=1,
        grid=(b, nt),
        in_specs=[seq(qc), seq(qr), seq(gt), seq(kc), pl.BlockSpec(memory_space=pl.ANY), full(expand), full(pmap),
                  seq(new), pl.BlockSpec((1, 1) + win.shape[2:], lambda i, j, pt: (layer, i, 0, 0))],
        out_specs=pl.BlockSpec((1, rows, LANES), lambda i, j, pt: (i, 0, 0)),
        scratch_shapes=[pltpu.VMEM((2, pp, 2 * KV_DIM, 128), F32), pltpu.SemaphoreType.DMA((2,)),
                        pltpu.VMEM((njp // bpt, rows, LANES), F32),
                        pltpu.VMEM((rows, 1), F32), pltpu.VMEM((rows, 1), F32),
                        pltpu.VMEM((rows, LANES), F32), pltpu.VMEM((rows, LANES), F32)],
    )
    return pl.pallas_call(
        functools.partial(_nsa_sample_kernel, pp=pp, past_len=past_len, n_tok=n_tok, nj=nj),
        grid_spec=gs,
        out_shape=jax.ShapeDtypeStruct((b, rows, LANES), F32),
        compiler_params=_cparams(("arbitrary", "arbitrary")),
        name="nsa_sample",
    )(page_table.reshape(-1), qc, qr, gt, kc, pool, expand, pmap, new, win)


def _merge_kernel(y_ref, o_ref, w_ref, g_ref, x_ref, out_ref):
    half = y_ref.shape[1]
    m = _mm(y_ref[...], w_ref[0:half, :]) + _mm(o_ref[...], w_ref[half:, :])
    out_ref[...] = x_ref[...] + _rms(m, g_ref[...])


def _merge(y, o, w, g, x, tm):
    m, d = x.shape
    half = y.shape[1]
    return pl.pallas_call(
        _merge_kernel,
        grid=(m // tm,),
        in_specs=[pl.BlockSpec((tm, half), lambda i: (i, 0)), pl.BlockSpec((tm, half), lambda i: (i, 0)),
                  pl.BlockSpec(w.shape, lambda i: (0, 0)), pl.BlockSpec((1, d), lambda i: (0, 0)),
                  pl.BlockSpec((tm, d), lambda i: (i, 0))],
        out_specs=pl.BlockSpec((tm, d), lambda i: (i, 0)),
        out_shape=jax.ShapeDtypeStruct((m, d), F32),
        compiler_params=_cparams(("arbitrary",)),
        name="merge",
    )(y, o, w, g, x)


def _ffn_kernel(x_ref, pre_ref, gpre_ref, gpost_ref, wu_ref, wg_ref, cw_ref, cb_ref, wd_ref,
                out_ref, tail_ref, xn_s, gpad, carry, acc, *, shift, padr):
    j, f = pl.program_id(1), pl.program_id(2)
    nf = pl.num_programs(2)
    tm = x_ref.shape[1]

    @pl.when(f == 0)
    def _():
        xn_s[...] = _rms(x_ref[0], gpre_ref[...]).astype(BF16)
        acc[...] = jnp.zeros(acc.shape, F32)

    @pl.when(j == 0)
    def _():
        gpad[0:padr, :] = pre_ref[0]

    @pl.when(j > 0)
    def _():
        gpad[0:padr, :] = carry[f]

    xn = xn_s[...]
    u = _mm(xn, wu_ref[...])
    gpad[padr:padr + tm, :] = _mm(xn, wg_ref[...])
    gc = cb_ref[...]
    for k in range(3):
        gc = gc + gpad[pl.ds(padr - (2 - k) * shift, tm), :] * cw_ref[k:k + 1, :]
    tail = gpad[tm:tm + padr, :]
    carry[f] = tail
    tail_ref[0, 0] = tail
    acc[...] += _mm((jax.nn.gelu(gc) * u).astype(BF16), wd_ref[...])

    @pl.when(f == nf - 1)
    def _():
        out_ref[0] = x_ref[0] + _rms(acc[...], gpost_ref[...])


def _ffn(x, pre, gpre, gpost, wu, wg, cw, cb, wd, tm, tf, shift):
    b, t, d = x.shape
    dff = wu.shape[1]
    padr = pre.shape[1]
    nt, nf = t // tm, dff // tf
    return pl.pallas_call(
        functools.partial(_ffn_kernel, shift=shift, padr=padr),
        grid=(b, nt, nf),
        in_specs=[pl.BlockSpec((1, tm, d), lambda i, j, f: (i, j, 0)),
                  pl.BlockSpec((1, padr, tf), lambda i, j, f: (i, 0, f)),
                  pl.BlockSpec((1, d), lambda i, j, f: (0, 0)),
                  pl.BlockSpec((1, d), lambda i, j, f: (0, 0)),
                  pl.BlockSpec((d, tf), lambda i, j, f: (0, f)),
                  pl.BlockSpec((d, tf), lambda i, j, f: (0, f)),
                  pl.BlockSpec((3, tf), lambda i, j, f: (0, f)),
                  pl.BlockSpec((1, tf), lambda i, j, f: (0, f)),
                  pl.BlockSpec((tf, d), lambda i, j, f: (f, 0))],
        out_specs=[pl.BlockSpec((1, tm, d), lambda i, j, f: (i, j, 0)),
                   pl.BlockSpec((1, 1, padr, tf), lambda i, j, f: (i, j, 0, f))],
        out_shape=[jax.ShapeDtypeStruct((b, t, d), F32), jax.ShapeDtypeStruct((b, nt, padr, dff), F32)],
        scratch_shapes=[pltpu.VMEM((tm, d), BF16), pltpu.VMEM((padr + tm, tf), F32),
                        pltpu.VMEM((nf, padr, tf), F32), pltpu.VMEM((tm, d), F32)],
        compiler_params=_cparams(("arbitrary", "arbitrary", "arbitrary")),
        name="ffn",
    )(x, pre, gpre, gpost, wu, wg, cw, cb, wd)


def _rot_cols(w):
    d, n = w.shape
    w4 = w.reshape(d, n // HEAD_DIM, 2, HEAD_DIM // 2)
    return jnp.concatenate([-w4[:, :, 1], w4[:, :, 0]], axis=2).reshape(d, n)


def _rope_tables(pos):
    half = HEAD_DIM // 2
    freq = ROPE_THETA ** (-jnp.arange(half, dtype=F32) / half)
    ang = pos.astype(F32)[:, None] * freq[None, :]
    cos, sin = jnp.cos(ang), jnp.sin(ang)
    return jnp.concatenate([cos] * 4, axis=1), jnp.concatenate([sin] * 4, axis=1)


def _block_diag(w):
    n, c, d = w.shape
    return jnp.einsum('ncd,nm->ncmd', w, jnp.eye(n, dtype=w.dtype)).reshape(n * c, n * d)


def _layer_weights(l, p):
    d_model = p['w_in'].shape[1]
    w = p['w_in'][l]
    sizes = [512, 512, 512, 256, 256, 256, 3 * N_HEADS]
    cuts = np.cumsum([0] + sizes)
    xr, gate, q, kvc, kvs, kvw, gl = [w[:, cuts[i]:cuts[i + 1]] for i in range(7)]
    gl_pad = jnp.pad(gl, ((0, 0), (0, LANES - gl.shape[1])))
    wcat = jnp.concatenate([xr, gate, q, _rot_cols(q), kvc, kvs, _rot_cols(kvs[:, :KV_DIM]),
                            kvw, _rot_cols(kvw[:, :KV_DIM]), gl_pad], axis=1).astype(BF16)
    wrow = jnp.concatenate([xr, gate, q, gl_pad], axis=1).astype(BF16)
    wt = jnp.concatenate([kvc, kvs, kvw], axis=1).T.astype(BF16)
    row = lambda v: v.reshape(1, -1)
    rg = (p['rg_conv_w'][l], row(p['rg_conv_b'][l]), _block_diag(p['rg_wa'][l]).astype(BF16),
          _block_diag(p['rg_wx'][l]).astype(BF16), row(p['rg_ba'][l]), row(p['rg_bx'][l]), row(p['rg_lambda'][l]))
    sel = lambda a, b: jnp.stack([a, a, b, b])
    eye2 = jnp.eye(N_KV, dtype=F32)
    w1 = jnp.stack([p['cmpk_w1'][l], p['cmpv_w1'][l]])
    w1bd = jnp.einsum('ksdf,he->skhdef', w1, eye2).reshape(2 * CMP_STRIDE, 2, KV_DIM, 2 * KV_DIM)
    pe = sel(p['cmpk_pe'][l], p['cmpv_pe'][l])
    pe = pe.transpose(1, 0, 2).reshape(2 * CMP_STRIDE, 4 * HEAD_DIM)
    b1 = sel(p['cmpk_b1'][l], p['cmpv_b1'][l]).reshape(1, -1)
    w2 = _block_diag(sel(p['cmpk_w2'][l], p['cmpv_w2'][l]))
    pairs = lambda w: w.reshape(CMP_STRIDE // 2, 2, 2, KV_DIM, 2 * KV_DIM).transpose(0, 2, 1, 3, 4).reshape(
        CMP_STRIDE // 2, 2, 2 * KV_DIM, 2 * KV_DIM).astype(BF16)
    cmpw = (pe[:CMP_STRIDE], pe[CMP_STRIDE:], pairs(w1bd[:CMP_STRIDE]), pairs(w1bd[CMP_STRIDE:]), b1, w2.astype(BF16))
    dff = p['ffn_w_in'].shape[2] // 2
    ffn = (p['ffn_w_in'][l][:, :dff].astype(BF16), p['ffn_w_in'][l][:, dff:].astype(BF16),
           p['ffn_conv_w'][l], row(p['ffn_conv_b'][l]), p['ffn_w_down'][l].astype(BF16))
    norms = tuple(row(p[k][l]) for k in ('norm_mix_pre', 'norm_mix_post', 'norm_ffn_pre', 'norm_ffn_post'))
    return dict(wcat=wcat, wrow=wrow, wt=wt, rg=rg, cmp=cmpw, w_out=p['w_out'][l].astype(BF16), ffn=ffn, norms=norms)


def _pool_map(n_slc_pad, n_rows):
    j = np.arange(n_slc_pad)[:, None]
    m = np.arange(n_rows)[None, :]
    r = SLC_BLOCK // CMP_STRIDE
    return jnp.asarray(((m >= r * j) & (m <= r * j + r)).astype(np.float32))


def _heads_major(a, b, t):
    a5 = a.reshape(b, t, 2, N_KV, HEAD_DIM)
    return a5[:, :, 0].transpose(0, 2, 1, 3), a5[:, :, 1].transpose(0, 2, 1, 3)


def _rows_to_state(a_t):
    b, _, t = a_t.shape
    return a_t.reshape(b, 2, N_KV, HEAD_DIM, t).transpose(0, 4, 1, 2, 3)


def _prompt_mixer(hp, lw, tabs):
    b, t, d = hp.shape
    g_pre, g_post = lw['norms'][0], lw['norms'][1]
    xr, gate, qc, qr, gl, kvct, kvst, kvstb, kvwt, kvwtb = _proj_t(
        hp.reshape(b * t, d), g_pre, lw['wrow'], lw['wt'], *tabs, b, _div_tile(t, 512))
    c = xr.shape[1]
    y, htail = _rglru(xr.reshape(b, t, c), gate.reshape(b, t, c), jnp.zeros((b, 8, c), F32),
                      jnp.zeros((b, 1, c), F32), *lw['rg'], _div_tile(t, 512))
    npg = t // 128
    ident = jnp.arange(b * npg, dtype=jnp.int32).reshape(b, npg)
    kcs = _compress(kvct, ident, lw['cmp'], _div_tile(npg, 32), False)
    kck, kcv = _heads_major(kcs, b, t // CMP_STRIDE)
    kcv = jnp.concatenate([kcv, jnp.ones_like(kcv)], axis=-1)
    nqb = t // Q_BLOCK
    n_slc = -(-t // SLC_BLOCK)
    onehot_t = (jnp.arange(HEAD_DIM)[:, None] == jnp.arange(t)[None, :] // SLC_BLOCK).astype(BF16)
    o = _nsa_prompt(qc, qr, gl, kck, kcv, kvstb, kvwtb, onehot_t, _pool_map(n_slc, t // CMP_STRIDE), t)
    hp = _merge(y.reshape(b * t, c), o, lw['w_out'], g_post, hp.reshape(b * t, d), _div_tile(b * t, 512))
    win_buf = min(WINDOW, t)
    st = (_rows_to_state(kvct), _rows_to_state(kvst), _rows_to_state(kvwt[:, :, t - win_buf:]), htail[:, 7],
          xr.reshape(b, t, c)[:, t - 3:])
    return hp.reshape(b, t, d), st


def _sample_mixer(hs, lw, s_cos, s_sin, cmp_pool, slc_pool, page_table, win_t, layer, past_len, rg_h0, rg_conv0):
    b, t, d = hs.shape
    g_pre, g_post = lw['norms'][0], lw['norms'][1]
    xr, gate, qc, qr, kvc, kvs, kvw, gl = _proj(hs.reshape(b * t, d), g_pre, lw['wcat'], s_cos, s_sin, b * t)
    c = xr.shape[1]
    padt = lambda a: jnp.pad(a.reshape(b, t, -1), ((0, 0), (0, 8 - t), (0, 0)))
    pre8 = jnp.pad(rg_conv0, ((0, 0), (5, 0), (0, 0)))
    y8, htail = _rglru(padt(xr), padt(gate), pre8, rg_h0.reshape(b, 1, c), *lw['rg'], 8)
    y = y8[:, :t].reshape(b * t, c)
    npg = page_table.shape[1]
    kcs = _compress(cmp_pool, page_table, lw['cmp'], _div_tile(npg, 32), True)
    nj = -(-(past_len + t) // SLC_BLOCK)
    pp = _div_tile(npg, 16)
    njp = -(-nj // LANES) * LANES
    rows = GROUP * N_KV * t

    def qrows(a):
        a5 = a.reshape(b, t, N_KV, GROUP, HEAD_DIM).transpose(0, 3, 2, 1, 4)
        z = jnp.zeros_like(a5[:, :, 0])
        top = jnp.concatenate([a5[:, :, 0], z], axis=-1)
        bot = jnp.concatenate([z, a5[:, :, 1]], axis=-1)
        return jnp.stack([top, bot], axis=2).reshape(b, rows, LANES)

    gts = gl[:, :3 * N_HEADS].reshape(b, t, N_KV, GROUP, 3).transpose(0, 3, 2, 1, 4).reshape(b, rows, 3)
    gts = jnp.pad(gts, ((0, 0), (0, 0), (0, 5)))
    expand = (jnp.arange(LANES)[:, None] == jnp.arange(pp * 128)[None, :] // SLC_BLOCK).astype(BF16)
    new = jnp.concatenate([kvs.reshape(b, t, 256), kvw.reshape(b, t, 256)], axis=-1)
    new = jnp.pad(new, ((0, 0), (0, 8 - t), (0, 0)))
    o32 = _nsa_sample(page_table, qrows(qc), qrows(qr), gts, kcs, slc_pool, expand, _pool_map(njp, kcs.shape[1]).T,
                      new, win_t, layer, pp, past_len, t, nj)
    o5 = o32.reshape(b, GROUP, N_KV, t, N_KV, HEAD_DIM)
    o = jnp.stack([o5[:, :, 0, :, 0], o5[:, :, 1, :, 1]], axis=1)
    o = o.transpose(0, 3, 1, 2, 4).reshape(b * t, N_HEADS * HEAD_DIM).astype(BF16)
    hs = _merge(y, o, lw['w_out'], g_post, hs.reshape(b * t, d), b * t)
    nbuf = win_t.shape[3]
    win_state_t = jnp.concatenate([win_t[layer], kvw.reshape(b, t, 256).transpose(0, 2, 1)], axis=2)[:, :, -nbuf:]
    st = (kvc.reshape(b, t, 2, N_KV, HEAD_DIM), kvs.reshape(b, t, 2, N_KV, HEAD_DIM), _rows_to_state(win_state_t),
          htail[:, (t - 1) % 8], jnp.concatenate([rg_conv0, xr.reshape(b, t, c)], axis=1)[:, -3:])
    return hs.reshape(b, t, d), st


def kernel(x_prompt, x_sample, cache_cmp_kv, cache_slc_kv, cache_win_kv, state_rg_h, state_rg_conv, state_ffn_conv, page_table, norm_mix_pre, norm_mix_post, norm_ffn_pre, norm_ffn_post, w_in, rg_conv_w, rg_conv_b, rg_wa, rg_ba, rg_wx, rg_bx, rg_lambda, cmpk_pe, cmpk_w1, cmpk_b1, cmpk_w2, cmpv_pe, cmpv_w1, cmpv_b1, cmpv_w2, w_out, ffn_w_in, ffn_conv_w, ffn_conv_b, ffn_w_down):
    params = dict(norm_mix_pre=norm_mix_pre, norm_mix_post=norm_mix_post, norm_ffn_pre=norm_ffn_pre,
                  norm_ffn_post=norm_ffn_post, w_in=w_in, rg_conv_w=rg_conv_w, rg_conv_b=rg_conv_b, rg_wa=rg_wa,
                  rg_ba=rg_ba, rg_wx=rg_wx, rg_bx=rg_bx, rg_lambda=rg_lambda, cmpk_pe=cmpk_pe, cmpk_w1=cmpk_w1,
                  cmpk_b1=cmpk_b1, cmpk_w2=cmpk_w2, cmpv_pe=cmpv_pe, cmpv_w1=cmpv_w1, cmpv_b1=cmpv_b1,
                  cmpv_w2=cmpv_w2, w_out=w_out, ffn_w_in=ffn_w_in, ffn_conv_w=ffn_conv_w, ffn_conv_b=ffn_conv_b,
                  ffn_w_down=ffn_w_down)
    depth = w_in.shape[0]
    bp, tp, d = x_prompt.shape
    bs, ts, _ = x_sample.shape
    past_len = page_table.shape[1] * cache_cmp_kv.shape[2]
    dff = ffn_w_in.shape[2] // 2
    p_cos, p_sin = _rope_tables(jnp.arange(tp))
    p_tabs = (p_cos, p_sin, p_cos.T, p_sin.T)
    s_cos, s_sin = _rope_tables(past_len + jnp.arange(ts))
    s_cos, s_sin = jnp.tile(s_cos, (bs, 1)), jnp.tile(s_sin, (bs, 1))
    n_pool, page = cache_cmp_kv.shape[1], cache_cmp_kv.shape[2]
    pages_t = lambda c: c.transpose(0, 1, 3, 4, 5, 2).reshape(depth * n_pool, 2 * KV_DIM, page)
    cmp_pool, slc_pool = pages_t(cache_cmp_kv), pages_t(cache_slc_kv)
    win_t = cache_win_kv.transpose(0, 1, 3, 4, 5, 2).reshape(depth, bs, 2 * KV_DIM, cache_win_kv.shape[2])
    hp, hs = x_prompt, x_sample
    states_p, states_s = [], []
    for l in range(depth):
        lw = _layer_weights(l, params)
        g_fpre, g_fpost = lw['norms'][2], lw['norms'][3]
        hp, st = _prompt_mixer(hp, lw, p_tabs)
        tmf = _div_tile(tp, 512)
        hp, tail = _ffn(hp, jnp.zeros((bp, 8, dff), F32), g_fpre, g_fpost, *lw['ffn'], tmf, _div_tile(dff, 1024), 1)
        states_p.append(st + (tail[:, -1, 6:],))
        hs, st = _sample_mixer(hs, lw, s_cos, s_sin, cmp_pool, slc_pool, page_table + l * n_pool, win_t, l,
                               past_len, state_rg_h[l], state_rg_conv[l])
        x_tm = hs.transpose(1, 0, 2).reshape(1, ts * bs, d)
        pre_tm = state_ffn_conv[l].transpose(1, 0, 2).reshape(1, 2 * bs, dff)
        out_tm, tail = _ffn(x_tm, pre_tm, g_fpre, g_fpost, *lw['ffn'], ts * bs, _div_tile(dff, 1024), bs)
        hs = out_tm.reshape(ts, bs, d).transpose(1, 0, 2)
        states_s.append(st + (tail.reshape(2, bs, dff).transpose(1, 0, 2),))
    p_cmp, p_slc, p_win, p_rgh, p_rgc, p_ffc = [jnp.stack(z) for z in zip(*states_p)]
    s_cmp, s_slc, s_win, s_rgh, s_rgc, s_ffc = [jnp.stack(z) for z in zip(*states_s)]
    return (hp, hs, p_cmp, s_cmp, p_slc, s_slc, p_win, s_win, p_rgh, s_rgh, p_rgc, s_rgc, p_ffc, s_ffc)
```

```python
import functools

import jax
import jax.numpy as jnp
import numpy as np
from jax import lax
from jax.experimental import pallas as pl
from jax.experimental.pallas import tpu as pltpu

F32 = jnp.float32
BF16 = jnp.bfloat16

HEAD_DIM = 64
N_KV = 2
GROUP = 4
N_HEADS = N_KV * GROUP
KV_DIM = N_KV * HEAD_DIM
CMP_STRIDE = 16
SLC_BLOCK = 64
N_SELECT = 16
N_LOCAL = 2
WINDOW = 512
Q_BLOCK = 128
RG_C = 8.0
ROPE_THETA = 10000.0
EPS = 1e-6

V7X_VMEM_BYTES = 64 * 1024 * 1024
VMEM_LIMIT = 48 * 1024 * 1024
SUBLANES = 8
LANES = 128

LOG2E = 1.4426950408889634
NEG = -2.0 ** 100
BIG = 1e30


def _cparams(sem):
    return pltpu.CompilerParams(dimension_semantics=sem, vmem_limit_bytes=VMEM_LIMIT)


def _div_tile(n, pref):
    t = min(n, pref)
    while n % t:
        t -= 1
    return t


def _rms(x, g):
    y = x * lax.rsqrt(jnp.mean(x * x, axis=-1, keepdims=True) + EPS)
    return y * g


def _sigmoid(x):
    return 0.5 * jnp.tanh(0.5 * x) + 0.5


def _nt(a, b, precision=None):
    return lax.dot_general(a, b, (((1,), (1,)), ((), ())), preferred_element_type=F32, precision=precision)


def _mm(a, b):
    return jnp.dot(a, b, preferred_element_type=F32)


C_XR, C_GATE, C_Q, C_QROT, C_KVC, C_KVS, C_KSROT, C_KVW, C_KWROT, C_GL, C_END = (
    0, 512, 1024, 1536, 2048, 2304, 2560, 2688, 2944, 3072, 3200)


def _proj_kernel(x_ref, g_ref, w_ref, cos_ref, sin_ref,
                 xr_ref, gate_ref, qc_ref, qr_ref, kvc_ref, kvs_ref, kvw_ref, gl_ref):
    xn = _rms(x_ref[...], g_ref[...]).astype(BF16)

    def mm(c0, c1):
        return _mm(xn, w_ref[:, c0:c1])

    cos = cos_ref[...]
    sin = sin_ref[...]
    xr_ref[...] = mm(C_XR, C_GATE)
    gate_ref[...] = mm(C_GATE, C_Q)
    q = mm(C_Q, C_QROT)
    qrot = mm(C_QROT, C_KVC)
    cos4 = jnp.concatenate([cos] * 4, axis=1)
    sin4 = jnp.concatenate([sin] * 4, axis=1)
    scale = HEAD_DIM ** -0.5
    qc_ref[...] = (q * scale).astype(BF16)
    qr_ref[...] = ((q * cos4 + qrot * sin4) * scale).astype(BF16)
    kvc_ref[...] = mm(C_KVC, C_KVS)
    kvs = mm(C_KVS, C_KSROT)
    ksrot = mm(C_KSROT, C_KVW)
    kvs_ref[:, 0:KV_DIM] = kvs[:, 0:KV_DIM] * cos + ksrot * sin
    kvs_ref[:, KV_DIM:] = kvs[:, KV_DIM:]
    kvw = mm(C_KVW, C_KWROT)
    kwrot = mm(C_KWROT, C_GL)
    kvw_ref[:, 0:KV_DIM] = kvw[:, 0:KV_DIM] * cos + kwrot * sin
    kvw_ref[:, KV_DIM:] = kvw[:, KV_DIM:]
    gl_ref[...] = jax.nn.sigmoid(mm(C_GL, C_END))


R_KVC, R_KVS, R_KVW = 0, 256, 512


def _proj_t_kernel(x_ref, g_ref, w_ref, wt_ref, cos_ref, sin_ref, cost_ref, sint_ref,
                   xr_ref, gate_ref, qc_ref, qr_ref, gl_ref, kvct_ref, kvst_ref, kvstb_ref, kvwt_ref, kvwtb_ref):
    xn = _rms(x_ref[...], g_ref[...]).astype(BF16)

    def mm(c0, c1):
        return _mm(xn, w_ref[:, c0:c1])

    def mt(r0, r1):
        return _nt(wt_ref[r0:r1, :], xn)

    xr_ref[...] = mm(0, 512)
    gate_ref[...] = mm(512, 1024)
    q = mm(1024, 1536)
    half = HEAD_DIM // 2
    lane = lax.broadcasted_iota(jnp.int32, q.shape, 1)
    qrot = jnp.where((lane & (HEAD_DIM - 1)) < half, -pltpu.roll(q, q.shape[1] - half, 1), pltpu.roll(q, half, 1))
    cos4 = jnp.concatenate([cos_ref[...]] * 4, axis=1)
    sin4 = jnp.concatenate([sin_ref[...]] * 4, axis=1)
    scale = HEAD_DIM ** -0.5 * LOG2E
    qc_ref[...] = (q * scale).astype(BF16)
    qr_ref[...] = ((q * cos4 + qrot * sin4) * scale).astype(BF16)
    gl_ref[...] = jax.nn.sigmoid(mm(1536, 1664))
    kvct_ref[0] = mt(R_KVC, R_KVS)
    cost, sint = cost_ref[...], sint_ref[...]
    for lo, f_ref, b_ref in ((R_KVS, kvst_ref, kvstb_ref), (R_KVW, kvwt_ref, kvwtb_ref)):
        kv = mt(lo, lo + 2 * KV_DIM)
        rot = jnp.concatenate([piece for h in range(N_KV) for piece in
                               (-kv[h * HEAD_DIM + half:(h + 1) * HEAD_DIM], kv[h * HEAD_DIM:h * HEAD_DIM + half])],
                              axis=0)
        k = kv[0:KV_DIM] * cost + rot * sint
        f_ref[0, 0:KV_DIM, :] = k
        f_ref[0, KV_DIM:, :] = kv[KV_DIM:]
        b_ref[0, 0:KV_DIM, :] = k.astype(BF16)
        b_ref[0, KV_DIM:, :] = kv[KV_DIM:].astype(BF16)


def _proj_t(x2d, g, wrow, wt, cos, sin, cost, sint, b, tm):
    m, d = x2d.shape
    t = m // b
    nper = t // tm
    row = lambda w: pl.BlockSpec((tm, w), lambda i: (i, 0))
    tab = pl.BlockSpec((tm, LANES), lambda i: (i % nper, 0))
    tabt = pl.BlockSpec((KV_DIM, tm), lambda i: (0, i % nper))
    tr = pl.BlockSpec((1, 2 * KV_DIM, tm), lambda i: (i // nper, 0, i % nper))
    full = lambda a: pl.BlockSpec(a.shape, lambda i: (0,) * a.ndim)
    widths = (512, 512, 512, 512, 128)
    dtypes = (F32, F32, BF16, BF16, F32)
    tdt = (F32, F32, BF16, F32, BF16)
    return pl.pallas_call(
        _proj_t_kernel,
        grid=(m // tm,),
        in_specs=[row(d), full(g), full(wrow), full(wt), tab, tab, tabt, tabt],
        out_specs=[row(w) for w in widths] + [tr] * 5,
        out_shape=[jax.ShapeDtypeStruct((m, w), dt) for w, dt in zip(widths, dtypes)]
        + [jax.ShapeDtypeStruct((b, 2 * KV_DIM, t), dt) for dt in tdt],
        compiler_params=_cparams(("arbitrary",)),
        name="proj_t",
    )(x2d, g, wrow, wt, cos, sin, cost, sint)


def _proj(x2d, g, wcat, cos, sin, tm):
    m, d = x2d.shape
    nper = cos.shape[0] // tm
    row = lambda w: pl.BlockSpec((tm, w), lambda i: (i, 0))
    tab = pl.BlockSpec((tm, LANES), lambda i: (i % nper, 0))
    widths = (512, 512, 512, 512, 256, 256, 256, 128)
    dtypes = (F32, F32, BF16, BF16, F32, F32, F32, F32)
    return pl.pallas_call(
        _proj_kernel,
        grid=(m // tm,),
        in_specs=[row(d), pl.BlockSpec((1, d), lambda i: (0, 0)),
                  pl.BlockSpec(wcat.shape, lambda i: (0, 0)), tab, tab],
        out_specs=[row(w) for w in widths],
        out_shape=[jax.ShapeDtypeStruct((m, w), dt) for w, dt in zip(widths, dtypes)],
        compiler_params=_cparams(("arbitrary",)),
        name="proj",
    )(x2d, g, wcat, cos, sin)


def _rglru_kernel(xr_ref, gate_ref, pre_ref, h0_ref, cw_ref, cb_ref, wa_ref, wx_ref, ba_ref, bx_ref, lam_ref,
                  y_ref, htail_ref, xpad, hc, a_s, b_s):
    tt = xr_ref.shape[1]
    j = pl.program_id(1)

    @pl.when(j == 0)
    def _():
        xpad[0:8, :] = pre_ref[0]
        hc[...] = jnp.broadcast_to(h0_ref[0], hc.shape)

    @pl.when(j > 0)
    def _():
        xpad[0:8, :] = xpad[tt:tt + 8, :]

    xpad[8:8 + tt, :] = xr_ref[0]
    xc = cb_ref[...]
    for k in range(4):
        xc = xc + xpad[pl.ds(5 + k, tt), :] * cw_ref[k:k + 1, :]
    xb = xc.astype(BF16)
    r = _sigmoid(_mm(xb, wa_ref[...]) + ba_ref[...])
    i = _sigmoid(_mm(xb, wx_ref[...]) + bx_ref[...])
    log_a = RG_C * r * jax.nn.log_sigmoid(lam_ref[...])
    a_s[...] = jnp.exp(log_a)
    t = jnp.tanh(log_a)
    b_s[...] = jnp.sqrt(-2.0 * t / (1.0 - t)) * (i * xc)

    row = lax.broadcasted_iota(jnp.int32, (8, a_s.shape[1]), 0)

    def body(gi, h):
        r0 = pl.multiple_of(gi * 8, 8)
        av = a_s[pl.ds(r0, 8), :]
        bv = b_s[pl.ds(r0, 8), :]
        for s in (1, 2, 4):
            keep = row >= s
            a_sh = pltpu.roll(av, s, 0)
            b_sh = pltpu.roll(bv, s, 0)
            bv = jnp.where(keep, av * b_sh + bv, bv)
            av = jnp.where(keep, av * a_sh, av)
        hs = av * h + bv
        b_s[pl.ds(r0, 8), :] = hs
        return jnp.broadcast_to(hs[7:8, :], hs.shape)

    h = lax.fori_loop(0, tt // 8, body, hc[...])
    hc[...] = h
    htail_ref[0] = b_s[tt - 8:tt, :]
    y_ref[0] = (b_s[...] * jax.nn.gelu(gate_ref[0])).astype(y_ref.dtype)


def _rglru(xr, gate, pre8, h0, cw, cb, wa, wx, ba, bx, lam, tt):
    b, t, c = xr.shape
    full = lambda a: pl.BlockSpec(a.shape, lambda i, j: (0,) * a.ndim)
    return pl.pallas_call(
        _rglru_kernel,
        grid=(b, t // tt),
        in_specs=[pl.BlockSpec((1, tt, c), lambda i, j: (i, j, 0)),
                  pl.BlockSpec((1, tt, c), lambda i, j: (i, j, 0)),
                  pl.BlockSpec((1, 8, c), lambda i, j: (i, 0, 0)),
                  pl.BlockSpec((1, 1, c), lambda i, j: (i, 0, 0)),
                  full(cw), full(cb), full(wa), full(wx), full(ba), full(bx), full(lam)],
        out_specs=[pl.BlockSpec((1, tt, c), lambda i, j: (i, j, 0)),
                   pl.BlockSpec((1, 8, c), lambda i, j: (i, 0, 0))],
        out_shape=[jax.ShapeDtypeStruct((b, t, c), BF16), jax.ShapeDtypeStruct((b, 8, c), F32)],
        scratch_shapes=[pltpu.VMEM((tt + 8, c), F32), pltpu.VMEM((8, c), F32),
                        pltpu.VMEM((tt, c), F32), pltpu.VMEM((tt, c), F32)],
        compiler_params=_cparams(("arbitrary", "arbitrary")),
        name="rglru",
    )(xr, gate, pre8, h0, cw, cb, wa, wx, ba, bx, lam)


def _page_copy(pt_ref, src_ref, buf, sem, idx, slot, p, seq_pages):
    page = pt_ref[idx]
    if seq_pages is None:
        src = src_ref.at[page]
    else:
        src = src_ref.at[page // seq_pages, :, pl.ds(pl.multiple_of((page % seq_pages) * 128, 128), 128)]
    return pltpu.make_async_copy(src, buf.at[slot, p], sem.at[slot])


def _cmp_kernel(pt_ref, pool_ref, pea_ref, peb_ref, w1a_ref, w1b_ref, b1_ref, w2_ref, out_ref,
                buf, sem, xbuf, ua_s, *, pp, seq_pages):
    nb, nt = pl.num_programs(0), pl.num_programs(1)
    j = pl.program_id(1)
    step = pl.program_id(0) * nt + j
    slot = step % 2
    m = pp * 8

    def fetch(st, sl):
        for p in range(pp):
            _page_copy(pt_ref, pool_ref, buf, sem, st * pp + p, sl, p, seq_pages).start()

    @pl.when(step == 0)
    def _():
        fetch(step, slot)

    @pl.when(step + 1 < nb * nt)
    def _():
        fetch(step + 1, 1 - slot)

    for p in range(pp):
        _page_copy(pt_ref, pool_ref, buf, sem, step * pp + p, slot, p, seq_pages).wait()

    n_part = xbuf.shape[0]
    ph = pp // n_part
    mh = ph * 8
    part_a, part_b = [], []
    for part in range(n_part):
        for p in range(ph):
            for kv in range(2):
                xbuf[part, kv, p * 128:(p + 1) * 128, :] = buf[slot, part * ph + p, kv * KV_DIM:(kv + 1) * KV_DIM, :].T
        acc_a = [jnp.zeros((mh, 256), F32)] * 2
        acc_b = [jnp.zeros((mh, 256), F32)] * 2
        for s in range(0, CMP_STRIDE, 2):
            for kv in range(2):
                x0 = xbuf[part, kv, pl.ds(s, mh, stride=CMP_STRIDE), :]
                x1 = xbuf[part, kv, pl.ds(s + 1, mh, stride=CMP_STRIDE), :]
                lanes = slice(kv * KV_DIM, (kv + 1) * KV_DIM)

                def pair(pe_ref):
                    return jnp.concatenate([(x0 + pe_ref[s:s + 1, lanes]).astype(BF16),
                                            (x1 + pe_ref[s + 1:s + 2, lanes]).astype(BF16)], axis=1)

                acc_a[kv] = acc_a[kv] + _mm(pair(pea_ref), w1a_ref[s // 2, kv])
                acc_b[kv] = acc_b[kv] + _mm(pair(peb_ref), w1b_ref[s // 2, kv])
        part_a.append(jnp.concatenate(acc_a, axis=1))
        part_b.append(jnp.concatenate(acc_b, axis=1))
    acc_a = jnp.concatenate(part_a, axis=0)
    acc_b = jnp.concatenate(part_b, axis=0)

    @pl.when(j == 0)
    def _():
        ua_s[0:8, :] = jnp.zeros((8, 512), F32)

    @pl.when(j > 0)
    def _():
        ua_s[0:8, :] = ua_s[m:m + 8, :]

    ua_s[8:8 + m, :] = acc_a
    hid = jax.nn.gelu(ua_s[pl.ds(7, m), :] + acc_b + b1_ref[...])
    out_ref[0] = _mm(hid.astype(BF16), w2_ref[...]).astype(out_ref.dtype)


def _compress(pool, page_table, cw, pp, paged):
    b, npg = page_table.shape
    nt = npg // pp
    m = pp * 8
    n_part = 2 if pp % 2 == 0 else 1
    full = lambda a: pl.BlockSpec(a.shape, lambda i, j, pt: (0,) * a.ndim)
    pea, peb, w1a, w1b, b1, w2 = cw
    gs = pltpu.PrefetchScalarGridSpec(
        num_scalar_prefetch=1,
        grid=(b, nt),
        in_specs=[pl.BlockSpec(memory_space=pl.ANY), full(pea), full(peb), full(w1a), full(w1b), full(b1), full(w2)],
        out_specs=pl.BlockSpec((1, m, 256), lambda i, j, pt: (i, j, 0)),
        scratch_shapes=[pltpu.VMEM((2, pp, 2 * KV_DIM, 128), F32), pltpu.SemaphoreType.DMA((2,)),
                        pltpu.VMEM((n_part, 2, pp // n_part * 128, KV_DIM), F32), pltpu.VMEM((m + 8, 512), F32)],
    )
    return pl.pallas_call(
        functools.partial(_cmp_kernel, pp=pp, seq_pages=None if paged else npg),
        grid_spec=gs,
        out_shape=jax.ShapeDtypeStruct((b, nt * m, 256), BF16),
        compiler_params=_cparams(("arbitrary", "arbitrary")),
        name="compress",
    )(page_table.reshape(-1), pool, pea, peb, w1a, w1b, b1, w2)


def _softmax_rows(s, mask):
    sm = jnp.where(mask, s, -BIG)
    mx = jnp.max(sm, axis=-1, keepdims=True)
    e = jnp.where(mask, jnp.exp(sm - mx), 0.0)
    return e / jnp.maximum(jnp.sum(e, axis=-1, keepdims=True), 1e-30)


def _nsa_prompt_kernel(qc_ref, qr_ref, g_ref, kck_ref, kcv_ref, kvs_ref, kvw_ref, oh_ref, pool_ref,
                       o_ref, sc_s, rank_s, *, t_len, kc_len, wl):
    qb = pl.program_id(1)
    start = qb * Q_BLOCK
    nc = kck_ref.shape[2]
    nj = pool_ref.shape[0]
    rows = GROUP * Q_BLOCK
    qpos = start + (lax.broadcasted_iota(jnp.int32, (rows, 1), 0) & (Q_BLOCK - 1))

    heads = range(N_KV)
    k_rows = [slice(h * HEAD_DIM, (h + 1) * HEAD_DIM) for h in heads]
    v_rows = [slice(KV_DIM + h * HEAD_DIM, KV_DIM + (h + 1) * HEAD_DIM) for h in heads]
    n_chunks = (start + Q_BLOCK + kc_len - 1) // kc_len
    n_full = n_chunks - 1
    k_diag = pl.multiple_of(n_full * kc_len, kc_len)
    base = pl.multiple_of(jnp.maximum(start + Q_BLOCK - wl, 0), Q_BLOCK)

    qpos1 = start + lax.broadcasted_iota(jnp.int32, (Q_BLOCK, 1), 0)
    mrow = lax.broadcasted_iota(jnp.int32, (Q_BLOCK, nc), 1)
    cbias = jnp.where((mrow >= 1) & (mrow * CMP_STRIDE + (CMP_STRIDE - 1) <= qpos1), 0.0, NEG)
    dbias = jnp.where(k_diag + lax.broadcasted_iota(jnp.int32, (Q_BLOCK, kc_len), 1) <= qpos1, 0.0, NEG)
    dist = qpos1 - (base + lax.broadcasted_iota(jnp.int32, (Q_BLOCK, wl), 1))
    wbias = jnp.where((dist >= 0) & (dist < WINDOW), 0.0, NEG)
    any_cmp = jnp.where(qpos >= 2 * CMP_STRIDE - 1, 1.0, 0.0)

    def add_bias(s, bias):
        return (s.reshape(GROUP, Q_BLOCK, s.shape[1]) + bias[None]).reshape(s.shape)

    def with_ones(vt):
        return jnp.concatenate([vt, jnp.ones(vt.shape, vt.dtype)], axis=0)

    def head_rows(ref, h):
        return jnp.concatenate([ref[:, (h * GROUP + g) * HEAD_DIM:(h * GROUP + g + 1) * HEAD_DIM]
                                for g in range(GROUP)], axis=0)

    def gate_rows(h, branch):
        gl = g_ref[...]
        cols = [(h * GROUP + g) * 3 + branch for g in range(GROUP)]
        return jnp.concatenate([gl[:, c:c + 1] for c in cols], axis=0)

    r_cmp, inv_cmp = [], []
    jidx = lax.broadcasted_iota(jnp.int32, (nj, Q_BLOCK), 0)
    lag = ((start + lax.broadcasted_iota(jnp.int32, (nj, Q_BLOCK), 1)) >> 6) - jidx
    forced = (jidx == 0) | ((lag >= 0) & (lag < N_LOCAL))
    for h in heads:
        s = add_bias(_nt(head_rows(qc_ref, h), kck_ref[0, h]), cbias)
        e = jnp.exp2(s - jnp.max(s, axis=-1, keepdims=True))
        r = _mm(e.astype(BF16), kcv_ref[0, h])
        inv = any_cmp / jnp.maximum(r[:, HEAD_DIM:HEAD_DIM + 1], 1e-30)
        p = e * inv
        psum = p[0:Q_BLOCK] + p[Q_BLOCK:2 * Q_BLOCK] + p[2 * Q_BLOCK:3 * Q_BLOCK] + p[3 * Q_BLOCK:]
        score_t = _nt(pool_ref[...], psum, precision=lax.Precision.HIGHEST)
        sc_s[h] = jnp.where(lag >= 0, jnp.where(forced, BIG, score_t), -BIG)
        rank_s[h] = jnp.zeros((nj, Q_BLOCK), F32)
        r_cmp.append(r)
        inv_cmp.append(inv)

    sub = lax.broadcasted_iota(jnp.int32, (SUBLANES, Q_BLOCK), 0)
    n_grp = nj // SUBLANES
    for h in heads:
        scv = sc_s[h]
        cnt = [jnp.zeros((SUBLANES, Q_BLOCK), F32)] * n_grp
        for i in range(nj):
            gi, ii = divmod(i, SUBLANES)
            si = scv[i:i + 1, :]
            for r in range(n_grp):
                blk = scv[r * SUBLANES:(r + 1) * SUBLANES]
                if r == gi:
                    beat = jnp.where(sub > ii, jnp.where(si >= blk, 1.0, 0.0), jnp.where(si > blk, 1.0, 0.0))
                else:
                    beat = jnp.where(si >= blk, 1.0, 0.0) if r > gi else jnp.where(si > blk, 1.0, 0.0)
                cnt[r] = cnt[r] + beat
        rank_s[h] = jnp.concatenate(cnt, axis=0)

    q4r = [head_rows(qr_ref, h) for h in heads]
    qaug = []
    for h in heads:
        bias_t = jnp.where(rank_s[h] < min(N_SELECT, nj), 0.0, NEG)
        pad_t = jnp.concatenate([jnp.zeros((HEAD_DIM, Q_BLOCK), F32), bias_t]
                                + ([jnp.zeros((HEAD_DIM - nj, Q_BLOCK), F32)] if nj < HEAD_DIM else []), axis=0)
        bias4 = jnp.concatenate([pad_t.T[:, HEAD_DIM:]] * GROUP, axis=0)
        qaug.append(jnp.concatenate([q4r[h].astype(F32), bias4], axis=1).astype(BF16))

    den = lambda r: jnp.maximum(r[:, HEAD_DIM:HEAD_DIM + 1], 1e-30)
    o_cw = []
    for h in heads:
        sw = add_bias(_mm(q4r[h], kvw_ref[0, k_rows[h], pl.ds(base, wl)]), wbias)
        ew = jnp.exp2(sw - jnp.max(sw, axis=-1, keepdims=True)).astype(BF16)
        r_win = _nt(ew, with_ones(kvw_ref[0, v_rows[h], pl.ds(base, wl)]))
        o_cw.append(r_cmp[h] * (gate_rows(h, 0) * inv_cmp[h]) + r_win * (gate_rows(h, 2) / den(r_win)))

    def chunk(k0, state, diagonal):
        out = []
        for h in heads:
            m_i, acc = state[h]
            kaug = jnp.concatenate([kvs_ref[0, k_rows[h], pl.ds(k0, kc_len)], oh_ref[:, pl.ds(k0, kc_len)]], axis=0)
            sc = _mm(qaug[h], kaug)
            if diagonal:
                sc = add_bias(sc, dbias)
            m_n = jnp.maximum(m_i, jnp.max(sc, axis=-1, keepdims=True))
            pe = jnp.exp2(sc - m_n).astype(BF16)
            acc = jnp.exp2(m_i - m_n) * acc + _nt(pe, with_ones(kvs_ref[0, v_rows[h], pl.ds(k0, kc_len)]))
            out.append((m_n, acc))
        return tuple(out)

    init = tuple((jnp.full((rows, 1), -3e38, F32), jnp.zeros((rows, LANES), F32)) for _ in heads)
    state = chunk(k_diag, init, True)
    state = lax.fori_loop(0, n_full, lambda c, st: chunk(pl.multiple_of(c * kc_len, kc_len), st, False), state)

    for h in heads:
        r_slc = state[h][1]
        o = o_cw[h] + r_slc * (gate_rows(h, 1) / den(r_slc))
        for g in range(GROUP):
            lanes = slice((h * GROUP + g) * HEAD_DIM, (h * GROUP + g + 1) * HEAD_DIM)
            o_ref[:, lanes] = o[g * Q_BLOCK:(g + 1) * Q_BLOCK, 0:HEAD_DIM].astype(o_ref.dtype)


def _nsa_prompt(qc, qr, gt, kck, kcv, kvst, kvwt, onehot_t, pool_t, t_len):
    b = kck.shape[0]
    nqb = t_len // Q_BLOCK
    kc_len = _div_tile(t_len, 1024)
    wl = min(WINDOW + Q_BLOCK, t_len)
    qspec = lambda w: pl.BlockSpec((Q_BLOCK, w), lambda i, j: (i * nqb + j, 0))
    seq = lambda a: pl.BlockSpec((1,) + a.shape[1:], lambda i, j: (i,) + (0,) * (a.ndim - 1))
    return pl.pallas_call(
        functools.partial(_nsa_prompt_kernel, t_len=t_len, kc_len=kc_len, wl=wl),
        grid=(b, nqb),
        in_specs=[qspec(qc.shape[1]), qspec(qr.shape[1]), qspec(gt.shape[1]), seq(kck), seq(kcv), seq(kvst), seq(kvwt),
                  pl.BlockSpec(onehot_t.shape, lambda i, j: (0, 0)), pl.BlockSpec(pool_t.shape, lambda i, j: (0, 0))],
        out_specs=qspec(qc.shape[1]),
        out_shape=jax.ShapeDtypeStruct(qc.shape, BF16),
        scratch_shapes=[pltpu.VMEM((N_KV,) + pool_t.shape[:1] + (Q_BLOCK,), F32)] * 2,
        compiler_params=_cparams(("arbitrary", "arbitrary")),
        name="nsa_prompt",
    )(qc, qr, gt, kck, kcv, kvst, kvwt, onehot_t, pool_t)


def _nsa_sample_kernel(pt_ref, qc_ref, qr_ref, g_ref, kc_ref, pool_ref, exp_ref, pmap_ref, new_ref, win_ref,
                       o_ref, buf, sem, bias_s, m_s, l_s, acc_s, ocmp_s, *, pp, past_len, n_tok, nj):
    nb, nt = pl.num_programs(0), pl.num_programs(1)
    j = pl.program_id(1)
    step = pl.program_id(0) * nt + j
    slot = step % 2
    rows = GROUP * N_KV * n_tok
    tk = pp * 128
    bpt = tk // SLC_BLOCK

    def fetch(st, sl):
        for p in range(pp):
            _page_copy(pt_ref, pool_ref, buf, sem, st * pp + p, sl, p, None).start()

    @pl.when(step == 0)
    def _():
        fetch(step, slot)

    @pl.when(step + 1 < nb * nt)
    def _():
        fetch(step + 1, 1 - slot)

    rid = lax.broadcasted_iota(jnp.int32, (rows, 1), 0)
    tok = rid % n_tok
    qpos = past_len + tok

    @pl.when(j == 0)
    def _():
        kc = kc_ref[0]
        nc = kc.shape[0]
        s = _nt(qc_ref[0], kc[:, 0:KV_DIM])
        mrow = lax.broadcasted_iota(jnp.int32, (rows, nc), 1)
        valid = (mrow >= 1) & (mrow * CMP_STRIDE + (CMP_STRIDE - 1) <= qpos)
        p = _softmax_rows(s, valid)
        ocmp_s[...] = _mm(p.astype(BF16), kc[:, KV_DIM:])
        r8 = N_KV * n_tok
        psum = p[0:r8] + p[r8:2 * r8] + p[2 * r8:3 * r8] + p[3 * r8:]
        score = jnp.dot(psum, pmap_ref[...], precision=lax.Precision.HIGHEST,
                        preferred_element_type=F32)
        njp = score.shape[1]
        jidx = lax.broadcasted_iota(jnp.int32, (r8, njp), 1)
        qp_r = past_len + lax.broadcasted_iota(jnp.int32, (r8, njp), 0) % n_tok
        lag = (qp_r >> 6) - jidx
        forced = (jidx == 0) | ((lag >= 0) & (lag < N_LOCAL))
        sc = jnp.where((lag >= 0) & (jidx < nj), jnp.where(forced, BIG, score), -BIG)
        lane = lax.broadcasted_iota(jnp.int32, (r8, LANES), 1)
        cnt = [jnp.zeros((r8, LANES), F32)] * (njp // LANES)
        for i in range(nj):
            si = sc[:, i:i + 1]
            for v in range(njp // LANES):
                blk = sc[:, v * LANES:(v + 1) * LANES]
                if v * LANES > i:
                    beat = jnp.where(si >= blk, 1.0, 0.0)
                elif (v + 1) * LANES <= i:
                    beat = jnp.where(si > blk, 1.0, 0.0)
                else:
                    beat = jnp.where(lane + v * LANES > i, jnp.where(si >= blk, 1.0, 0.0), jnp.where(si > blk, 1.0, 0.0))
                cnt[v] = cnt[v] + beat
        bias = jnp.where(jnp.concatenate(cnt, axis=1) < min(N_SELECT, nj), 0.0, NEG)
        for tj in range(njp // bpt):
            b8 = pltpu.roll(bias, (njp - bpt * tj) % njp, 1)[:, 0:LANES]
            bias_s[tj] = jnp.concatenate([b8] * GROUP, axis=0)
        m_s[...] = jnp.full(m_s.shape, -3e38, F32)
        l_s[...] = jnp.zeros(l_s.shape, F32)
        acc_s[...] = jnp.zeros(acc_s.shape, F32)

    for p in range(pp):
        _page_copy(pt_ref, pool_ref, buf, sem, step * pp + p, slot, p, None).wait()

    def online(sc, pv):
        m_i = m_s[...]
        m_n = jnp.maximum(m_i, jnp.max(sc, axis=-1, keepdims=True))
        alpha = jnp.exp(m_i - m_n)
        pe = jnp.exp(sc - m_n)
        l_s[...] = alpha * l_s[...] + jnp.sum(pe, axis=-1, keepdims=True)
        acc_s[...] = alpha * acc_s[...] + pv(pe.astype(BF16))
        m_s[...] = m_n

    qr = qr_ref[0]
    sc = jnp.concatenate([_mm(qr, buf[slot, p, 0:KV_DIM, :].astype(BF16)) for p in range(pp)], axis=1)
    sc = sc + _mm(bias_s[j].astype(BF16), exp_ref[...])

    def pv_pages(pe):
        o = _nt(pe[:, 0:128], buf[slot, 0, KV_DIM:, :].astype(BF16))
        for p in range(1, pp):
            o = o + _nt(pe[:, p * 128:(p + 1) * 128], buf[slot, p, KV_DIM:, :].astype(BF16))
        return o

    online(sc, pv_pages)

    @pl.when(j == nt - 1)
    def _():
        new = new_ref[0]
        nk = new.shape[0]
        kidx = lax.broadcasted_iota(jnp.int32, (rows, nk), 1)
        last_bias = bias_s[(past_len // SLC_BLOCK) // bpt][:, (past_len // SLC_BLOCK) % bpt:(past_len // SLC_BLOCK) % bpt + 1]
        scn = _nt(qr, new[:, 0:KV_DIM].astype(BF16)) + last_bias
        scn = jnp.where((kidx <= tok) & (kidx < n_tok), scn, NEG)
        online(scn, lambda pe: _mm(pe, new[:, KV_DIM:2 * KV_DIM].astype(BF16)))
        o_slc = acc_s[...] / jnp.maximum(l_s[...], 1e-30)

        nbuf = win_ref.shape[3]
        s1 = _mm(qr, win_ref[0, 0, 0:KV_DIM, :].astype(BF16))
        d1 = tok + nbuf - lax.broadcasted_iota(jnp.int32, (rows, nbuf), 1)
        ok1 = (d1 >= 0) & (d1 < WINDOW)
        s2 = _nt(qr, new[:, 2 * KV_DIM:3 * KV_DIM].astype(BF16))
        d2 = tok - kidx
        ok2 = (d2 >= 0) & (d2 < WINDOW) & (kidx < n_tok)
        s1 = jnp.where(ok1, s1, -BIG)
        s2 = jnp.where(ok2, s2, -BIG)
        mx = jnp.maximum(jnp.max(s1, axis=-1, keepdims=True), jnp.max(s2, axis=-1, keepdims=True))
        e1 = jnp.where(ok1, jnp.exp(s1 - mx), 0.0)
        e2 = jnp.where(ok2, jnp.exp(s2 - mx), 0.0)
        den = jnp.maximum(jnp.sum(e1, axis=-1, keepdims=True) + jnp.sum(e2, axis=-1, keepdims=True), 1e-30)
        o_win = (_nt(e1.astype(BF16), win_ref[0, 0, KV_DIM:, :].astype(BF16))
                 + _mm(e2.astype(BF16), new[:, 3 * KV_DIM:].astype(BF16))) / den
        gt = g_ref[0]
        o_ref[0] = gt[:, 0:1] * ocmp_s[...] + gt[:, 1:2] * o_slc + gt[:, 2:3] * o_win


def _nsa_sample(page_table, qc, qr, gt, kc, pool, expand, pmap, new, win, layer, pp, past_len, n_tok, nj):
    b, npg = page_table.shape
    nt = npg // pp
    rows = GROUP * N_KV * n_tok
    tk = pp * 128
    bpt = tk // SLC_BLOCK
    njp = pmap.shape[1]
    seq = lambda a: pl.BlockSpec((1,) + a.shape[1:], lambda i, j, pt: (i,) + (0,) * (a.ndim - 1))
    full = lambda a: pl.BlockSpec(a.shape, lambda i, j, pt: (0,) * a.ndim)
    gs = pltpu.PrefetchScalarGridSpec(
        num_scalar_prefetch=1,
        grid=(b, nt),
        in_specs=[seq(qc), seq(qr), seq(gt), seq(kc), pl.BlockSpec(memory_space=pl.ANY), full(expand), full(pmap),
                  seq(new), pl.BlockSpec((1, 1) + win.shape[2:], lambda i, j, pt: (layer, i, 0, 0))],
        out_specs=pl.BlockSpec((1, rows, LANES), lambda i, j, pt: (i, 0, 0)),
        scratch_shapes=[pltpu.VMEM((2, pp, 2 * KV_DIM, 128), F32), pltpu.SemaphoreType.DMA((2,)),
                        pltpu.VMEM((njp // bpt, rows, LANES), F32),
                        pltpu.VMEM((rows, 1), F32), pltpu.VMEM((rows, 1), F32),
                        pltpu.VMEM((rows, LANES), F32), pltpu.VMEM((rows, LANES), F32)],
    )
    return pl.pallas_call(
        functools.partial(_nsa_sample_kernel, pp=pp, past_len=past_len, n_tok=n_tok, nj=nj),
        grid_spec=gs,
        out_shape=jax.ShapeDtypeStruct((b, rows, LANES), F32),
        compiler_params=_cparams(("arbitrary", "arbitrary")),
        name="nsa_sample",
    )(page_table.reshape(-1), qc, qr, gt, kc, pool, expand, pmap, new, win)


def _merge_kernel(y_ref, o_ref, w_ref, g_ref, x_ref, out_ref):
    half = y_ref.shape[1]
    m = _mm(y_ref[...], w_ref[0:half, :]) + _mm(o_ref[...], w_ref[half:, :])
    out_ref[...] = x_ref[...] + _rms(m, g_ref[...])


def _merge(y, o, w, g, x, tm):
    m, d = x.shape
    half = y.shape[1]
    return pl.pallas_call(
        _merge_kernel,
        grid=(m // tm,),
        in_specs=[pl.BlockSpec((tm, half), lambda i: (i, 0)), pl.BlockSpec((tm, half), lambda i: (i, 0)),
                  pl.BlockSpec(w.shape, lambda i: (0, 0)), pl.BlockSpec((1, d), lambda i: (0, 0)),
                  pl.BlockSpec((tm, d), lambda i: (i, 0))],
        out_specs=pl.BlockSpec((tm, d), lambda i: (i, 0)),
        out_shape=jax.ShapeDtypeStruct((m, d), F32),
        compiler_params=_cparams(("arbitrary",)),
        name="merge",
    )(y, o, w, g, x)


def _ffn_kernel(x_ref, pre_ref, gpre_ref, gpost_ref, wu_ref, wg_ref, cw_ref, cb_ref, wd_ref,
                out_ref, tail_ref, xn_s, gpad, carry, acc, *, shift, padr):
    j, f = pl.program_id(1), pl.program_id(2)
    nf = pl.num_programs(2)
    tm = x_ref.shape[1]

    @pl.when(f == 0)
    def _():
        xn_s[...] = _rms(x_ref[0], gpre_ref[...]).astype(BF16)
        acc[...] = jnp.zeros(acc.shape, F32)

    @pl.when(j == 0)
    def _():
        gpad[0:padr, :] = pre_ref[0]

    @pl.when(j > 0)
    def _():
        gpad[0:padr, :] = carry[f]

    xn = xn_s[...]
    u = _mm(xn, wu_ref[...])
    gpad[padr:padr + tm, :] = _mm(xn, wg_ref[...])
    gc = cb_ref[...]
    for k in range(3):
        gc = gc + gpad[pl.ds(padr - (2 - k) * shift, tm), :] * cw_ref[k:k + 1, :]
    tail = gpad[tm:tm + padr, :]
    carry[f] = tail
    tail_ref[0, 0] = tail
    acc[...] += _mm((jax.nn.gelu(gc) * u).astype(BF16), wd_ref[...])

    @pl.when(f == nf - 1)
    def _():
        out_ref[0] = x_ref[0] + _rms(acc[...], gpost_ref[...])


def _ffn(x, pre, gpre, gpost, wu, wg, cw, cb, wd, tm, tf, shift):
    b, t, d = x.shape
    dff = wu.shape[1]
    padr = pre.shape[1]
    nt, nf = t // tm, dff // tf
    return pl.pallas_call(
        functools.partial(_ffn_kernel, shift=shift, padr=padr),
        grid=(b, nt, nf),
        in_specs=[pl.BlockSpec((1, tm, d), lambda i, j, f: (i, j, 0)),
                  pl.BlockSpec((1, padr, tf), lambda i, j, f: (i, 0, f)),
                  pl.BlockSpec((1, d), lambda i, j, f: (0, 0)),
                  pl.BlockSpec((1, d), lambda i, j, f: (0, 0)),
                  pl.BlockSpec((d, tf), lambda i, j, f: (0, f)),
                  pl.BlockSpec((d, tf), lambda i, j, f: (0, f)),
                  pl.BlockSpec((3, tf), lambda i, j, f: (0, f)),
                  pl.BlockSpec((1, tf), lambda i, j, f: (0, f)),
                  pl.BlockSpec((tf, d), lambda i, j, f: (f, 0))],
        out_specs=[pl.BlockSpec((1, tm, d), lambda i, j, f: (i, j, 0)),
                   pl.BlockSpec((1, 1, padr, tf), lambda i, j, f: (i, j, 0, f))],
        out_shape=[jax.ShapeDtypeStruct((b, t, d), F32), jax.ShapeDtypeStruct((b, nt, padr, dff), F32)],
        scratch_shapes=[pltpu.VMEM((tm, d), BF16), pltpu.VMEM((padr + tm, tf), F32),
                        pltpu.VMEM((nf, padr, tf), F32), pltpu.VMEM((tm, d), F32)],
        compiler_params=_cparams(("arbitrary", "arbitrary", "arbitrary")),
        name="ffn",
    )(x, pre, gpre, gpost, wu, wg, cw, cb, wd)


def _rot_cols(w):
    d, n = w.shape
    w4 = w.reshape(d, n // HEAD_DIM, 2, HEAD_DIM // 2)
    return jnp.concatenate([-w4[:, :, 1], w4[:, :, 0]], axis=2).reshape(d, n)


def _rope_tables(pos):
    half = HEAD_DIM // 2
    freq = ROPE_THETA ** (-jnp.arange(half, dtype=F32) / half)
    ang = pos.astype(F32)[:, None] * freq[None, :]
    cos, sin = jnp.cos(ang), jnp.sin(ang)
    return jnp.concatenate([cos] * 4, axis=1), jnp.concatenate([sin] * 4, axis=1)


def _block_diag(w):
    n, c, d = w.shape
    return jnp.einsum('ncd,nm->ncmd', w, jnp.eye(n, dtype=w.dtype)).reshape(n * c, n * d)


def _layer_weights(l, p):
    d_model = p['w_in'].shape[1]
    w = p['w_in'][l]
    sizes = [512, 512, 512, 256, 256, 256, 3 * N_HEADS]
    cuts = np.cumsum([0] + sizes)
    xr, gate, q, kvc, kvs, kvw, gl = [w[:, cuts[i]:cuts[i + 1]] for i in range(7)]
    gl_pad = jnp.pad(gl, ((0, 0), (0, LANES - gl.shape[1])))
    wcat = jnp.concatenate([xr, gate, q, _rot_cols(q), kvc, kvs, _rot_cols(kvs[:, :KV_DIM]),
                            kvw, _rot_cols(kvw[:, :KV_DIM]), gl_pad], axis=1).astype(BF16)
    wrow = jnp.concatenate([xr, gate, q, gl_pad], axis=1).astype(BF16)
    wt = jnp.concatenate([kvc, kvs, kvw], axis=1).T.astype(BF16)
    row = lambda v: v.reshape(1, -1)
    rg = (p['rg_conv_w'][l], row(p['rg_conv_b'][l]), _block_diag(p['rg_wa'][l]).astype(BF16),
          _block_diag(p['rg_wx'][l]).astype(BF16), row(p['rg_ba'][l]), row(p['rg_bx'][l]), row(p['rg_lambda'][l]))
    sel = lambda a, b: jnp.stack([a, a, b, b])
    eye2 = jnp.eye(N_KV, dtype=F32)
    w1 = jnp.stack([p['cmpk_w1'][l], p['cmpv_w1'][l]])
    w1bd = jnp.einsum('ksdf,he->skhdef', w1, eye2).reshape(2 * CMP_STRIDE, 2, KV_DIM, 2 * KV_DIM)
    pe = sel(p['cmpk_pe'][l], p['cmpv_pe'][l])
    pe = pe.transpose(1, 0, 2).reshape(2 * CMP_STRIDE, 4 * HEAD_DIM)
    b1 = sel(p['cmpk_b1'][l], p['cmpv_b1'][l]).reshape(1, -1)
    w2 = _block_diag(sel(p['cmpk_w2'][l], p['cmpv_w2'][l]))
    pairs = lambda w: w.reshape(CMP_STRIDE // 2, 2, 2, KV_DIM, 2 * KV_DIM).transpose(0, 2, 1, 3, 4).reshape(
        CMP_STRIDE // 2, 2, 2 * KV_DIM, 2 * KV_DIM).astype(BF16)
    cmpw = (pe[:CMP_STRIDE], pe[CMP_STRIDE:], pairs(w1bd[:CMP_STRIDE]), pairs(w1bd[CMP_STRIDE:]), b1, w2.astype(BF16))
    dff = p['ffn_w_in'].shape[2] // 2
    ffn = (p['ffn_w_in'][l][:, :dff].astype(BF16), p['ffn_w_in'][l][:, dff:].astype(BF16),
           p['ffn_conv_w'][l], row(p['ffn_conv_b'][l]), p['ffn_w_down'][l].astype(BF16))
    norms = tuple(row(p[k][l]) for k in ('norm_mix_pre', 'norm_mix_post', 'norm_ffn_pre', 'norm_ffn_post'))
    return dict(wcat=wcat, wrow=wrow, wt=wt, rg=rg, cmp=cmpw, w_out=p['w_out'][l].astype(BF16), ffn=ffn, norms=norms)


def _pool_map(n_slc_pad, n_rows):
    j = np.arange(n_slc_pad)[:, None]
    m = np.arange(n_rows)[None, :]
    r = SLC_BLOCK // CMP_STRIDE
    return jnp.asarray(((m >= r * j) & (m <= r * j + r)).astype(np.float32))


def _heads_major(a, b, t):
    a5 = a.reshape(b, t, 2, N_KV, HEAD_DIM)
    return a5[:, :, 0].transpose(0, 2, 1, 3), a5[:, :, 1].transpose(0, 2, 1, 3)


def _rows_to_state(a_t):
    b, _, t = a_t.shape
    return a_t.reshape(b, 2, N_KV, HEAD_DIM, t).transpose(0, 4, 1, 2, 3)


def _prompt_mixer(hp, lw, tabs):
    b, t, d = hp.shape
    g_pre, g_post = lw['norms'][0], lw['norms'][1]
    xr, gate, qc, qr, gl, kvct, kvst, kvstb, kvwt, kvwtb = _proj_t(
        hp.reshape(b * t, d), g_pre, lw['wrow'], lw['wt'], *tabs, b, _div_tile(t, 512))
    c = xr.shape[1]
    y, htail = _rglru(xr.reshape(b, t, c), gate.reshape(b, t, c), jnp.zeros((b, 8, c), F32),
                      jnp.zeros((b, 1, c), F32), *lw['rg'], _div_tile(t, 512))
    npg = t // 128
    ident = jnp.arange(b * npg, dtype=jnp.int32).reshape(b, npg)
    kcs = _compress(kvct, ident, lw['cmp'], _div_tile(npg, 32), False)
    kck, kcv = _heads_major(kcs, b, t // CMP_STRIDE)
    kcv = jnp.concatenate([kcv, jnp.ones_like(kcv)], axis=-1)
    nqb = t // Q_BLOCK
    n_slc = -(-t // SLC_BLOCK)
    onehot_t = (jnp.arange(HEAD_DIM)[:, None] == jnp.arange(t)[None, :] // SLC_BLOCK).astype(BF16)
    o = _nsa_prompt(qc, qr, gl, kck, kcv, kvstb, kvwtb, onehot_t, _pool_map(n_slc, t // CMP_STRIDE), t)
    hp = _merge(y.reshape(b * t, c), o, lw['w_out'], g_post, hp.reshape(b * t, d), _div_tile(b * t, 512))
    win_buf = min(WINDOW, t)
    st = (_rows_to_state(kvct), _rows_to_state(kvst), _rows_to_state(kvwt[:, :, t - win_buf:]), htail[:, 7],
          xr.reshape(b, t, c)[:, t - 3:])
    return hp.reshape(b, t, d), st


def _sample_mixer(hs, lw, s_cos, s_sin, cmp_pool, slc_pool, page_table, win_t, layer, past_len, rg_h0, rg_conv0):
    b, t, d = hs.shape
    g_pre, g_post = lw['norms'][0], lw['norms'][1]
    xr, gate, qc, qr, kvc, kvs, kvw, gl = _proj(hs.reshape(b * t, d), g_pre, lw['wcat'], s_cos, s_sin, b * t)
    c = xr.shape[1]
    padt = lambda a: jnp.pad(a.reshape(b, t, -1), ((0, 0), (0, 8 - t), (0, 0)))
    pre8 = jnp.pad(rg_conv0, ((0, 0), (5, 0), (0, 0)))
    y8, htail = _rglru(padt(xr), padt(gate), pre8, rg_h0.reshape(b, 1, c), *lw['rg'], 8)
    y = y8[:, :t].reshape(b * t, c)
    npg = page_table.shape[1]
    kcs = _compress(cmp_pool, page_table, lw['cmp'], _div_tile(npg, 32), True)
    nj = -(-(past_len + t) // SLC_BLOCK)
    pp = _div_tile(npg, 16)
    njp = -(-nj // LANES) * LANES
    rows = GROUP * N_KV * t

    def qrows(a):
        a5 = a.reshape(b, t, N_KV, GROUP, HEAD_DIM).transpose(0, 3, 2, 1, 4)
        z = jnp.zeros_like(a5[:, :, 0])
        top = jnp.concatenate([a5[:, :, 0], z], axis=-1)
        bot = jnp.concatenate([z, a5[:, :, 1]], axis=-1)
        return jnp.stack([top, bot], axis=2).reshape(b, rows, LANES)

    gts = gl[:, :3 * N_HEADS].reshape(b, t, N_KV, GROUP, 3).transpose(0, 3, 2, 1, 4).reshape(b, rows, 3)
    gts = jnp.pad(gts, ((0, 0), (0, 0), (0, 5)))
    expand = (jnp.arange(LANES)[:, None] == jnp.arange(pp * 128)[None, :] // SLC_BLOCK).astype(BF16)
    new = jnp.concatenate([kvs.reshape(b, t, 256), kvw.reshape(b, t, 256)], axis=-1)
    new = jnp.pad(new, ((0, 0), (0, 8 - t), (0, 0)))
    o32 = _nsa_sample(page_table, qrows(qc), qrows(qr), gts, kcs, slc_pool, expand, _pool_map(njp, kcs.shape[1]).T,
                      new, win_t, layer, pp, past_len, t, nj)
    o5 = o32.reshape(b, GROUP, N_KV, t, N_KV, HEAD_DIM)
    o = jnp.stack([o5[:, :, 0, :, 0], o5[:, :, 1, :, 1]], axis=1)
    o = o.transpose(0, 3, 1, 2, 4).reshape(b * t, N_HEADS * HEAD_DIM).astype(BF16)
    hs = _merge(y, o, lw['w_out'], g_post, hs.reshape(b * t, d), b * t)
    nbuf = win_t.shape[3]
    win_state_t = jnp.concatenate([win_t[layer], kvw.reshape(b, t, 256).transpose(0, 2, 1)], axis=2)[:, :, -nbuf:]
    st = (kvc.reshape(b, t, 2, N_KV, HEAD_DIM), kvs.reshape(b, t, 2, N_KV, HEAD_DIM), _rows_to_state(win_state_t),
          htail[:, (t - 1) % 8], jnp.concatenate([rg_conv0, xr.reshape(b, t, c)], axis=1)[:, -3:])
    return hs.reshape(b, t, d), st


def kernel(x_prompt, x_sample, cache_cmp_kv, cache_slc_kv, cache_win_kv, state_rg_h, state_rg_conv, state_ffn_conv, page_table, norm_mix_pre, norm_mix_post, norm_ffn_pre, norm_ffn_post, w_in, rg_conv_w, rg_conv_b, rg_wa, rg_ba, rg_wx, rg_bx, rg_lambda, cmpk_pe, cmpk_w1, cmpk_b1, cmpk_w2, cmpv_pe, cmpv_w1, cmpv_b1, cmpv_w2, w_out, ffn_w_in, ffn_conv_w, ffn_conv_b, ffn_w_down):
    params = dict(norm_mix_pre=norm_mix_pre, norm_mix_post=norm_mix_post, norm_ffn_pre=norm_ffn_pre,
                  norm_ffn_post=norm_ffn_post, w_in=w_in, rg_conv_w=rg_conv_w, rg_conv_b=rg_conv_b, rg_wa=rg_wa,
                  rg_ba=rg_ba, rg_wx=rg_wx, rg_bx=rg_bx, rg_lambda=rg_lambda, cmpk_pe=cmpk_pe, cmpk_w1=cmpk_w1,
                  cmpk_b1=cmpk_b1, cmpk_w2=cmpk_w2, cmpv_pe=cmpv_pe, cmpv_w1=cmpv_w1, cmpv_b1=cmpv_b1,
                  cmpv_w2=cmpv_w2, w_out=w_out, ffn_w_in=ffn_w_in, ffn_conv_w=ffn_conv_w, ffn_conv_b=ffn_conv_b,
                  ffn_w_down=ffn_w_down)
    depth = w_in.shape[0]
    bp, tp, d = x_prompt.shape
    bs, ts, _ = x_sample.shape
    past_len = page_table.shape[1] * cache_cmp_kv.shape[2]
    dff = ffn_w_in.shape[2] // 2
    p_cos, p_sin = _rope_tables(jnp.arange(tp))
    p_tabs = (p_cos, p_sin, p_cos.T, p_sin.T)
    s_cos, s_sin = _rope_tables(past_len + jnp.arange(ts))
    s_cos, s_sin = jnp.tile(s_cos, (bs, 1)), jnp.tile(s_sin, (bs, 1))
    n_pool, page = cache_cmp_kv.shape[1], cache_cmp_kv.shape[2]
    pages_t = lambda c: c.transpose(0, 1, 3, 4, 5, 2).reshape(depth * n_pool, 2 * KV_DIM, page)
    cmp_pool, slc_pool = pages_t(cache_cmp_kv), pages_t(cache_slc_kv)
    win_t = cache_win_kv.transpose(0, 1, 3, 4, 5, 2).reshape(depth, bs, 2 * KV_DIM, cache_win_kv.shape[2])
    hp, hs = x_prompt, x_sample
    states_p, states_s = [], []
    for l in range(depth):
        lw = _layer_weights(l, params)
        g_fpre, g_fpost = lw['norms'][2], lw['norms'][3]
        hp, st = _prompt_mixer(hp, lw, p_tabs)
        tmf = _div_tile(tp, 512)
        hp, tail = _ffn(hp, jnp.zeros((bp, 8, dff), F32), g_fpre, g_fpost, *lw['ffn'], tmf, _div_tile(dff, 1024), 1)
        states_p.append(st + (tail[:, -1, 6:],))
        hs, st = _sample_mixer(hs, lw, s_cos, s_sin, cmp_pool, slc_pool, page_table + l * n_pool, win_t, l,
                               past_len, state_rg_h[l], state_rg_conv[l])
        x_tm = hs.transpose(1, 0, 2).reshape(1, ts * bs, d)
        pre_tm = state_ffn_conv[l].transpose(1, 0, 2).reshape(1, 2 * bs, dff)
        out_tm, tail = _ffn(x_tm, pre_tm, g_fpre, g_fpost, *lw['ffn'], ts * bs, _div_tile(dff, 1024), bs)
        hs = out_tm.reshape(ts, bs, d).transpose(1, 0, 2)
        states_s.append(st + (tail.reshape(2, bs, dff).transpose(1, 0, 2),))
    p_cmp, p_slc, p_win, p_rgh, p_rgc, p_ffc = [jnp.stack(z) for z in zip(*states_p)]
    s_cmp, s_slc, s_win, s_rgh, s_rgc, s_ffc = [jnp.stack(z) for z in zip(*states_s)]
    return (hp, hs, p_cmp, s_cmp, p_slc, s_slc, p_win, s_win, p_rgh, s_rgh, p_rgc, s_rgc, p_ffc, s_ffc)
```
